```python
import math
import jax, jax.numpy as jnp
from jax import lax
import numpy as np

D_MODEL = 1024
BATCH = 32
SEQ = 256
DEPTH = 4
DEC_BATCH = 4
DEC_SEQ = 1024
PAST_LEN = 512

GRID_W = 64
N_EVEN = (DEPTH + 1) // 2
N_ODD = DEPTH // 2
N_MOD = 9
D_FF = 2816
EPS = 1e-6
ROPE_BASE = 10000.0
NEG = -1e30
MLA_HEADS = 8
MLA_NOPE = 64
MLA_ROPE = 32
MLA_QK = MLA_NOPE + MLA_ROPE
MLA_V = 64
MLA_Q_LORA = 256
MLA_KV_LORA = 128
M_HEADS = 4
M_DK = 64
M_DV = 128
M_CHUNK = 128
S_HEADS = 16
S_KV_HEADS = 4
S_GROUP = S_HEADS // S_KV_HEADS
S_HEAD_DIM = 64
S_WINDOW = 128
S_BLOCK = 128
DENSE_Q_BLOCK = 128
DENSE_SWEEP_KEYS = 2048

EVEN_SPLITS = (MLA_Q_LORA, MLA_KV_LORA, MLA_ROPE, M_HEADS * M_DK, M_HEADS * M_DK, M_HEADS * M_DV, M_HEADS * M_DV, 4 * M_HEADS)
EVEN_SPLIT_IDX = tuple(int(v) for v in np.cumsum(EVEN_SPLITS)[:-1])
EVEN_IN = sum(EVEN_SPLITS)
EVEN_MIX = MLA_HEADS * MLA_V + M_HEADS * M_DV
ODD_IN = (S_HEADS + 2 * S_KV_HEADS) * S_HEAD_DIM
ODD_MIX = S_HEADS * S_HEAD_DIM

kernel_name = 'hybrid_mla_mlstm_swa_flow_step'


def rmsnorm(x, g):
    x32 = x.astype(jnp.float32)
    y = x32 * lax.rsqrt(jnp.mean(x32 * x32, -1, keepdims=True) + EPS)
    return (y * g.astype(jnp.float32)).astype(x.dtype)


def swiglu(h, wg, wu, wd):
    return (jax.nn.silu(h @ wg) * (h @ wu)) @ wd


def adaln(cond, w, b):
    mod = jax.nn.silu(cond) @ w + b
    return mod.reshape(cond.shape[0], N_MOD, D_MODEL)


def modulate(x, g_pre, shift, scale):
    return (rmsnorm(x, g_pre) * (1 + scale[:, None]) + shift[:, None]).astype(x.dtype)


def add_residual(x, y, g_post, gate, weight):
    return (x + weight * gate[:, None] * rmsnorm(y, g_post)).astype(x.dtype)


def axial_rope(t_len, rot_dim):
    rows = t_len // GRID_W
    nf = rot_dim // 4
    inv = ROPE_BASE ** (-jnp.arange(nf, dtype=jnp.float32) / nf)
    row = jnp.repeat(jnp.arange(rows, dtype=jnp.float32), GRID_W)
    col = jnp.tile(jnp.arange(GRID_W, dtype=jnp.float32), rows)
    ang = jnp.stack([row[:, None] * inv, col[:, None] * inv], axis=1)
    return jnp.cos(ang), jnp.sin(ang)


def apply_rope(x, cos, sin):
    shp = x.shape
    xr = x.astype(jnp.float32).reshape(shp[:-1] + (2, 2, shp[-1] // 4))
    x1, x2 = xr[..., 0, :], xr[..., 1, :]
    out = jnp.stack([x1 * cos - x2 * sin, x2 * cos + x1 * sin], axis=-2)
    return out.reshape(shp).astype(x.dtype)


def _attend_block(qb, k, v, scale, sink):
    s = jnp.einsum('bhgqd,bhkd->bhgqk', qb, k).astype(jnp.float32) * scale
    if sink is None:
        p = jax.nn.softmax(s, axis=-1)
    else:
        sk = jnp.broadcast_to(sink.astype(jnp.float32)[None, :, :, None, None], s.shape[:-1] + (1,))
        p = jax.nn.softmax(jnp.concatenate([s, sk], -1), axis=-1)[..., :-1]
    return jnp.einsum('bhgqk,bhkd->bhgqd', p.astype(v.dtype), v)


def dense_attend(q, k, v, scale, sink=None):
    if k.shape[2] < DENSE_SWEEP_KEYS:
        return _attend_block(q, k, v, scale, sink)
    b, hk, g, tq, d = q.shape
    nq = tq // DENSE_Q_BLOCK
    qs = jnp.moveaxis(q.reshape(b, hk, g, nq, DENSE_Q_BLOCK, d), 3, 0)
    out = lax.map(lambda qb: _attend_block(qb, k, v, scale, sink), qs)
    return jnp.moveaxis(out, 0, 3).reshape(b, hk, g, tq, v.shape[-1])


def banded_attend(q, k, v, ck, cv, scale, sink):
    b, hk, g, t, d = q.shape
    nb = t // S_BLOCK
    qb = q.reshape(b, hk, g, nb, S_BLOCK, d)

    def bands(a):
        ap = jnp.pad(a, ((0, 0), (0, 0), (S_BLOCK, S_BLOCK), (0, 0))).reshape(b, hk, nb + 2, S_BLOCK, a.shape[-1])
        return jnp.concatenate([ap[:, :, :-2], ap[:, :, 1:-1], ap[:, :, 2:]], axis=3)

    kb, vb = bands(k), bands(v)
    blk = jnp.arange(nb)[:, None, None] * S_BLOCK
    qpos = blk + jnp.arange(S_BLOCK)[None, :, None]
    kpos = blk - S_BLOCK + jnp.arange(3 * S_BLOCK)[None, None, :]
    allowed = (jnp.abs(qpos - kpos) <= S_WINDOW) & (kpos >= 0) & (kpos < t)
    s_loc = jnp.where(allowed, jnp.einsum('bhgnqd,bhnkd->bhgnqk', qb, kb).astype(jnp.float32) * scale, NEG)
    s_ctx = jnp.einsum('bhgnqd,bhkd->bhgnqk', qb, ck).astype(jnp.float32) * scale
    s_sink = jnp.broadcast_to(sink.astype(jnp.float32)[None, :, :, None, None, None], s_loc.shape[:-1] + (1,))
    p = jax.nn.softmax(jnp.concatenate([s_ctx, s_loc, s_sink], -1), axis=-1)
    lc = ck.shape[2]
    p_ctx = p[..., :lc].astype(v.dtype)
    p_loc = p[..., lc:lc + 3 * S_BLOCK].astype(v.dtype)
    o = jnp.einsum('bhgnqk,bhkd->bhgnqd', p_ctx, cv) + jnp.einsum('bhgnqk,bhnkd->bhgnqd', p_loc, vb)
    return o.reshape(b, hk, g, t, d)


def mlstm_scan(q, k, v, log_i, log_f, c0, n0, m0):
    b_, h_, t_, _ = q.shape
    nc = t_ // M_CHUNK

    def chunks(a):
        return jnp.moveaxis(a.reshape((b_, h_, nc, M_CHUNK) + a.shape[3:]), 2, 0)

    causal = jnp.tril(jnp.ones((M_CHUNK, M_CHUNK), bool))

    def step(carry, inp):
        cm, nv, m = carry
        qc, kc, vc, li, lf = inp
        bcum = jnp.cumsum(lf, axis=-1)
        dmat = jnp.where(causal, bcum[..., :, None] - bcum[..., None, :] + li[..., None, :], NEG)
        inter = m[..., None] + bcum
        mt = jnp.maximum(inter, dmat.max(-1))
        w_inter = jnp.exp(inter - mt)
        a = jnp.einsum('bhtd,bhsd->bhts', qc, kc) * jnp.exp(dmat - mt[..., None])
        num = w_inter[..., None] * jnp.einsum('bhtd,bhde->bhte', qc, cm) + jnp.einsum('bhts,bhse->bhte', a, vc)
        den = w_inter * jnp.einsum('bhtd,bhd->bht', qc, nv) + a.sum(-1)
        hc = num / jnp.maximum(jnp.abs(den), jnp.exp(-mt))[..., None]
        m_new = mt[..., -1]
        w_s = jnp.exp(bcum[..., -1:] - bcum + li - m_new[..., None])
        w_c = jnp.exp(m + bcum[..., -1] - m_new)
        c_new = w_c[..., None, None] * cm + jnp.einsum('bhs,bhsd,bhse->bhde', w_s, kc, vc)
        n_new = w_c[..., None] * nv + jnp.einsum('bhs,bhsd->bhd', w_s, kc)
        return (c_new, n_new, m_new), hc

    (cm, nv, m), hs = lax.scan(step, (c0, n0, m0), tuple(chunks(a) for a in (q, k, v, log_i, log_f)))
    h = jnp.moveaxis(hs, 0, 2).reshape(b_, h_, t_, v.shape[-1])
    return h, cm, nv, m


def even_mixer(h, w_in, g_qa, g_kva, w_qb, w_kvb, gate_b, g_mout, w_out,
               ctx_ckv=None, ctx_krope=None, m_state=None, rope=None):
    bsz, t_len, _ = h.shape
    q_a, kv_a, k_rope, mq, mk, mv, mo, mg = jnp.split(h @ w_in, EVEN_SPLIT_IDX, axis=-1)
    ckv = rmsnorm(kv_a, g_kva)
    q = jnp.moveaxis((rmsnorm(q_a, g_qa) @ w_qb).reshape(bsz, t_len, MLA_HEADS, MLA_QK), 1, 2)
    if ctx_ckv is None:
        keys_ckv, keys_kr = ckv, k_rope
    else:
        q = jnp.concatenate([q[..., :MLA_NOPE], apply_rope(q[..., MLA_NOPE:], *rope)], -1)
        keys_ckv = jnp.concatenate([ctx_ckv, ckv], 1)
        keys_kr = jnp.concatenate([ctx_krope, apply_rope(k_rope, *rope)], 1)
    t_keys = keys_ckv.shape[1]
    kv = jnp.moveaxis((keys_ckv @ w_kvb).reshape(bsz, t_keys, MLA_HEADS, MLA_NOPE + MLA_V), 1, 2)
    k = jnp.concatenate([kv[..., :MLA_NOPE],
                         jnp.broadcast_to(keys_kr[:, None], (bsz, MLA_HEADS, t_keys, MLA_ROPE))], -1)
    attn = dense_attend(q[:, :, None], k, kv[..., MLA_NOPE:], MLA_QK ** -0.5)[:, :, 0]
    attn = jnp.moveaxis(attn, 1, 2).reshape(bsz, t_len, MLA_HEADS * MLA_V)
    if m_state is None:
        c0 = jnp.zeros((bsz, 2, M_HEADS, M_DK, M_DV), jnp.float32)
        n0 = jnp.zeros((bsz, 2, M_HEADS, M_DK), jnp.float32)
        m0 = jnp.zeros((bsz, 2, M_HEADS), jnp.float32)
    else:
        c0, n0, m0 = (s.astype(jnp.float32) for s in m_state)

    def heads(a, d):
        return jnp.moveaxis(a.astype(jnp.float32).reshape(bsz, t_len, M_HEADS, d), 1, 2)

    mqh, mkh, mvh = heads(mq, M_DK), heads(mk, M_DK) * (M_DK ** -0.5), heads(mv, M_DV)
    gates = jnp.transpose((mg + gate_b).astype(jnp.float32).reshape(bsz, t_len, 4, M_HEADS), (2, 0, 3, 1))
    h_f, cf, nf_, mf = mlstm_scan(mqh, mkh, mvh, gates[0], jax.nn.log_sigmoid(gates[1]),
                                  c0[:, 0], n0[:, 0], m0[:, 0])
    fl = lambda a: jnp.flip(a, axis=2)
    h_b, cb, nb_, mb = mlstm_scan(fl(mqh), fl(mkh), fl(mvh), fl(gates[2]), fl(jax.nn.log_sigmoid(gates[3])),
                                  c0[:, 1], n0[:, 1], m0[:, 1])
    hm = h_f + fl(h_b)
    hm = hm * lax.rsqrt(jnp.mean(hm * hm, -1, keepdims=True) + EPS)
    hm = jnp.moveaxis(hm, 1, 2).reshape(bsz, t_len, M_HEADS * M_DV) * g_mout.astype(jnp.float32)
    m_out = (jax.nn.sigmoid(mo.astype(jnp.float32)) * hm).astype(h.dtype)
    y = jnp.concatenate([attn, m_out], -1) @ w_out
    state = (ckv, k_rope, jnp.stack([cf, cb], 1), jnp.stack([nf_, nb_], 1), jnp.stack([mf, mb], 1))
    return y, state


def odd_mixer(h, w_in, sink, w_out, ctx_k=None, ctx_v=None, rope=None):
    bsz, t_len, _ = h.shape
    q, k, v = jnp.split(h @ w_in, [S_HEADS * S_HEAD_DIM, (S_HEADS + S_KV_HEADS) * S_HEAD_DIM], axis=-1)
    q = jnp.moveaxis(q.reshape(bsz, t_len, S_KV_HEADS, S_GROUP, S_HEAD_DIM), 1, 3)
    k = jnp.moveaxis(k.reshape(bsz, t_len, S_KV_HEADS, S_HEAD_DIM), 1, 2)
    v = jnp.moveaxis(v.reshape(bsz, t_len, S_KV_HEADS, S_HEAD_DIM), 1, 2)
    sk = sink.reshape(S_KV_HEADS, S_GROUP)
    scale = S_HEAD_DIM ** -0.5
    if ctx_k is None:
        o = dense_attend(q, k, v, scale, sk)
    else:
        q = apply_rope(q, *rope)
        k = apply_rope(k, *rope)
        o = banded_attend(q, k, v, ctx_k, ctx_v, scale, sk)
    o = jnp.moveaxis(o, 3, 1).reshape(bsz, t_len, S_HEADS * S_HEAD_DIM)
    return o @ w_out, (k, v)


def setup_inputs(seed: int = 0) -> dict:
    key = jax.random.key(seed)
    ks = iter(jax.random.split(key, 32))
    nrm = lambda shape, s=1.0: jax.random.normal(next(ks), shape, jnp.float32) * s
    gain = lambda shape: 1.0 + nrm(shape, 0.02)
    fb = jnp.linspace(3.0, 6.0, M_HEADS)
    zb = jnp.zeros((M_HEADS,))
    gate_off = jnp.concatenate([zb, fb, zb, fb])
    return {
        'x_prompt': nrm((BATCH, SEQ, D_MODEL)),
        'x_sample': nrm((DEC_BATCH, DEC_SEQ, D_MODEL)),
        'cache_mla_ckv': nrm((DEC_BATCH, N_EVEN, PAST_LEN, MLA_KV_LORA)),
        'cache_mla_krope': nrm((DEC_BATCH, N_EVEN, PAST_LEN, MLA_ROPE)),
        'state_mlstm_C': nrm((DEC_BATCH, N_EVEN, 2, M_HEADS, M_DK, M_DV)),
        'state_mlstm_n': nrm((DEC_BATCH, N_EVEN, 2, M_HEADS, M_DK)),
        'state_mlstm_m': nrm((DEC_BATCH, N_EVEN, 2, M_HEADS), 0.5),
        'cache_swa_k': nrm((DEC_BATCH, N_ODD, S_KV_HEADS, PAST_LEN, S_HEAD_DIM)),
        'cache_swa_v': nrm((DEC_BATCH, N_ODD, S_KV_HEADS, PAST_LEN, S_HEAD_DIM)),
        'c': nrm((DEC_BATCH, D_MODEL)),
        'c_ctx': nrm((D_MODEL,)),
        'ada_w': nrm((DEPTH, D_MODEL, N_MOD * D_MODEL), D_MODEL ** -0.5),
        'ada_b': nrm((DEPTH, N_MOD * D_MODEL), 0.02),
        'norm_g': gain((DEPTH, 6, D_MODEL)),
        'ffn_w_gate': nrm((DEPTH, 2, D_MODEL, D_FF), D_MODEL ** -0.5),
        'ffn_w_up': nrm((DEPTH, 2, D_MODEL, D_FF), D_MODEL ** -0.5),
        'ffn_w_down': nrm((DEPTH, 2, D_FF, D_MODEL), D_FF ** -0.5),
        'even_w_in': nrm((N_EVEN, D_MODEL, EVEN_IN), D_MODEL ** -0.5),
        'mla_g_qa': gain((N_EVEN, MLA_Q_LORA)),
        'mla_g_kva': gain((N_EVEN, MLA_KV_LORA)),
        'mla_w_qb': nrm((N_EVEN, MLA_Q_LORA, MLA_HEADS * MLA_QK), MLA_Q_LORA ** -0.5),
        'mla_w_kvb': nrm((N_EVEN, MLA_KV_LORA, MLA_HEADS * (MLA_NOPE + MLA_V)), MLA_KV_LORA ** -0.5),
        'mlstm_gate_b': gate_off[None] + nrm((N_EVEN, 4 * M_HEADS), 0.1),
        'mlstm_g_out': gain((N_EVEN, M_HEADS * M_DV)),
        'even_w_out': nrm((N_EVEN, EVEN_MIX, D_MODEL), EVEN_MIX ** -0.5),
        'odd_w_in': nrm((N_ODD, D_MODEL, ODD_IN), D_MODEL ** -0.5),
        'swa_sink': nrm((N_ODD, S_HEADS), 0.5),
        'odd_w_out': nrm((N_ODD, ODD_MIX, D_MODEL), ODD_MIX ** -0.5),
    }


def reference(x_prompt, x_sample, cache_mla_ckv, cache_mla_krope, state_mlstm_C, state_mlstm_n,
              state_mlstm_m, cache_swa_k, cache_swa_v, c, c_ctx, ada_w, ada_b, norm_g, ffn_w_gate,
              ffn_w_up, ffn_w_down, even_w_in, mla_g_qa, mla_g_kva, mla_w_qb, mla_w_kvb, mlstm_gate_b,
              mlstm_g_out, even_w_out, odd_w_in, swa_sink, odd_w_out):
    t_lat = x_sample.shape[1]
    rope_mla = axial_rope(t_lat, MLA_ROPE)
    rope_swa = axial_rope(t_lat, S_HEAD_DIM)
    xc, xl = x_prompt, x_sample
    ckv_l, kr_l, mc_l, mn_l, mm_l, sk_l, sv_l = [], [], [], [], [], [], []
    for l in range(DEPTH):
        mc = adaln(c_ctx[None], ada_w[l], ada_b[l])
        ml = adaln(c, ada_w[l], ada_b[l])
        g = norm_g[l]
        xc = add_residual(xc, swiglu(modulate(xc, g[0], mc[:, 0], mc[:, 1]), ffn_w_gate[l, 0], ffn_w_up[l, 0],
                                     ffn_w_down[l, 0]), g[1], mc[:, 2], 0.5)
        xl = add_residual(xl, swiglu(modulate(xl, g[0], ml[:, 0], ml[:, 1]), ffn_w_gate[l, 0], ffn_w_up[l, 0],
                                     ffn_w_down[l, 0]), g[1], ml[:, 2], 0.5)
        hc = modulate(xc, g[2], mc[:, 3], mc[:, 4])
        hl = modulate(xl, g[2], ml[:, 3], ml[:, 4])
        i = l // 2
        if l % 2 == 0:
            ew = (even_w_in[i], mla_g_qa[i], mla_g_kva[i], mla_w_qb[i], mla_w_kvb[i], mlstm_gate_b[i],
                  mlstm_g_out[i], even_w_out[i])
            yc, (ckv, kr, s_c, s_n, s_m) = even_mixer(hc, *ew)
            yl, _ = even_mixer(hl, *ew, ctx_ckv=cache_mla_ckv[:, i], ctx_krope=cache_mla_krope[:, i],
                               m_state=(state_mlstm_C[:, i], state_mlstm_n[:, i], state_mlstm_m[:, i]),
                               rope=rope_mla)
            ckv_l.append(ckv)
            kr_l.append(kr)
            mc_l.append(s_c)
            mn_l.append(s_n)
            mm_l.append(s_m)
        else:
            ow = (odd_w_in[i], swa_sink[i], odd_w_out[i])
            yc, (kc, vc) = odd_mixer(hc, *ow)
            yl, _ = odd_mixer(hl, *ow, ctx_k=cache_swa_k[:, i], ctx_v=cache_swa_v[:, i], rope=rope_swa)
            sk_l.append(kc)
            sv_l.append(vc)
        xc = add_residual(xc, yc, g[3], mc[:, 5], 1.0)
        xl = add_residual(xl, yl, g[3], ml[:, 5], 1.0)
        xc = add_residual(xc, swiglu(modulate(xc, g[4], mc[:, 6], mc[:, 7]), ffn_w_gate[l, 1], ffn_w_up[l, 1],
                                     ffn_w_down[l, 1]), g[5], mc[:, 8], 0.5)
        xl = add_residual(xl, swiglu(modulate(xl, g[4], ml[:, 6], ml[:, 7]), ffn_w_gate[l, 1], ffn_w_up[l, 1],
                                     ffn_w_down[l, 1]), g[5], ml[:, 8], 0.5)
    new_ckv = jnp.stack(ckv_l, 1)
    new_krope = jnp.stack(kr_l, 1)
    new_mC = jnp.stack(mc_l, 1)
    new_mn = jnp.stack(mn_l, 1)
    new_mm = jnp.stack(mm_l, 1)
    new_sk = jnp.stack(sk_l, 1)
    new_sv = jnp.stack(sv_l, 1)
    return (xc, xl, new_ckv, new_krope, new_mC, new_mn, new_mm, new_sk, new_sv)
```

```python
import functools

import jax
import jax.numpy as jnp
import numpy as np
from jax import lax
from jax.experimental import pallas as pl
from jax.experimental.pallas import tpu as pltpu

F32 = jnp.float32
BF16 = jnp.bfloat16

D_MODEL = 1024
BATCH = 32
SEQ = 256
DEPTH = 4
DEC_BATCH = 4
DEC_SEQ = 1024
PAST_LEN = 512
GRID_W = 64
N_MOD = 9
D_FF = 2816
EPS = 1e-6
ROPE_BASE = 10000.0
NEG = -1e30
MLA_HEADS = 8
MLA_NOPE = 64
MLA_ROPE = 32
MLA_QK = MLA_NOPE + MLA_ROPE
MLA_V = 64
MLA_Q_LORA = 256
MLA_KV_LORA = 128
M_HEADS = 4
M_DK = 64
M_DV = 128
M_CHUNK = 128
S_HEADS = 16
S_KV_HEADS = 4
S_GROUP = S_HEADS // S_KV_HEADS
S_HEAD_DIM = 64
S_WINDOW = 128
S_BLOCK = 128

N_CTX = BATCH * SEQ
N_LAT = DEC_BATCH * DEC_SEQ
N_TOK = N_CTX + N_LAT
N_GROUPS = 1 + DEC_BATCH
COND_ROWS = 8

LANES = 128
FF_CHUNK = 256
N_FF_CHUNKS = D_FF // FF_CHUNK
assert N_FF_CHUNKS * FF_CHUNK == D_FF

TM = 512
ADA_TN = 1536
VMEM_LIMIT = 52 * 1024 * 1024

E_QA, E_KVA, E_KR, E_MQ, E_MK, E_MV, E_MO, E_MG, E_COLS = 0, 256, 384, 512, 768, 1024, 1536, 2048, 2176
KR_LANE = MLA_NOPE


def _cparams(*sem):
    return pltpu.CompilerParams(dimension_semantics=sem, vmem_limit_bytes=VMEM_LIMIT)


def _group_of_block(i, tm):
    n_ctx = N_CTX // tm
    per_lat = DEC_SEQ // tm
    return jnp.where(i < n_ctx, 0, 1 + (i - n_ctx) // per_lat)


def _rope_block(i, tm):
    n_ctx = N_CTX // tm
    per_lat = DEC_SEQ // tm
    return jnp.where(i < n_ctx, 0, 1 + (i - n_ctx) % per_lat)


def _rms(x, g):
    return x * lax.rsqrt(jnp.mean(x * x, axis=-1, keepdims=True) + EPS) * g


def _modulated(x, g_pre, mod_ref, sub):
    shift = mod_ref[0, 3 * sub:3 * sub + 1, :]
    scale = mod_ref[0, 3 * sub + 1:3 * sub + 2, :]
    return _rms(x, g_pre) * (1.0 + scale) + shift


def _silu(x):
    return x * jax.nn.sigmoid(x)


def _swap_halves(x, half):
    lane = lax.broadcasted_iota(jnp.int32, x.shape, 1)
    up = pltpu.roll(x, LANES - half, 1)
    down = pltpu.roll(x, half, 1)
    return jnp.where(lane % (2 * half) < half, up, down)


def _rope_tile(x, cos, sin, half):
    return x * cos + _swap_halves(x, half) * sin


def _adaln_kernel(cond_ref, w_ref, b_ref, o_ref):
    s = _silu(cond_ref[...]).astype(BF16)
    o_ref[0] = jnp.dot(s, w_ref[0].astype(BF16), preferred_element_type=F32) + b_ref[0]


def _adaln(cond, ada_w, ada_b):
    n_out = N_MOD * D_MODEL
    out = pl.pallas_call(
        _adaln_kernel,
        grid=(DEPTH, n_out // ADA_TN),
        in_specs=[
            pl.BlockSpec((COND_ROWS, D_MODEL), lambda l, j: (0, 0)),
            pl.BlockSpec((1, D_MODEL, ADA_TN), lambda l, j: (l, 0, j)),
            pl.BlockSpec((1, 1, ADA_TN), lambda l, j: (l, 0, j)),
        ],
        out_specs=pl.BlockSpec((1, COND_ROWS, ADA_TN), lambda l, j: (l, 0, j)),
        out_shape=jax.ShapeDtypeStruct((DEPTH, COND_ROWS, n_out), F32),
        compiler_params=_cparams("arbitrary", "arbitrary"),
    )(cond, ada_w, ada_b.reshape(DEPTH, 1, n_out))
    return out.reshape(DEPTH, COND_ROWS, N_MOD, D_MODEL)


def _ffn_kernel(sub, x_ref, mod_ref, g_ref, wg_ref, wu_ref, wd_ref, o_ref, acc_ref):
    x = x_ref[...]
    h = _modulated(x, g_ref[0:1, :], mod_ref, sub).astype(BF16)
    acc_ref[...] = jnp.zeros_like(acc_ref)

    def chunk(c, carry):
        gate = jnp.dot(h, wg_ref[c], preferred_element_type=F32)
        up = jnp.dot(h, wu_ref[c], preferred_element_type=F32)
        a = (_silu(gate) * up).astype(BF16)
        acc_ref[...] += jnp.dot(a, wd_ref[c], preferred_element_type=F32)
        return carry

    lax.fori_loop(0, N_FF_CHUNKS, chunk, 0)
    gate_row = mod_ref[0, 3 * sub + 2:3 * sub + 3, :]
    o_ref[...] = x + 0.5 * gate_row * _rms(acc_ref[...], g_ref[1:2, :])


def _ffn(x, mod_l, g2, wg, wu, wd, sub):
    const3 = lambda i: (0, 0, 0)
    return pl.pallas_call(
        functools.partial(_ffn_kernel, sub),
        grid=(N_TOK // TM,),
        in_specs=[
            pl.BlockSpec((TM, D_MODEL), lambda i: (i, 0)),
            pl.BlockSpec((1, N_MOD, D_MODEL), lambda i: (_group_of_block(i, TM), 0, 0)),
            pl.BlockSpec((2, D_MODEL), lambda i: (0, 0)),
            pl.BlockSpec((N_FF_CHUNKS, D_MODEL, FF_CHUNK), const3, pipeline_mode=pl.Buffered(1)),
            pl.BlockSpec((N_FF_CHUNKS, D_MODEL, FF_CHUNK), const3, pipeline_mode=pl.Buffered(1)),
            pl.BlockSpec((N_FF_CHUNKS, FF_CHUNK, D_MODEL), const3, pipeline_mode=pl.Buffered(1)),
        ],
        out_specs=pl.BlockSpec((TM, D_MODEL), lambda i: (i, 0)),
        out_shape=jax.ShapeDtypeStruct((N_TOK, D_MODEL), F32),
        scratch_shapes=[pltpu.VMEM((TM, D_MODEL), F32)],
        compiler_params=_cparams("arbitrary"),
    )(x, mod_l, g2, wg, wu, wd)


def _ffn_weights(wg, wu, wd):
    wg_c = wg.astype(BF16).reshape(D_MODEL, N_FF_CHUNKS, FF_CHUNK).transpose(1, 0, 2)
    wu_c = wu.astype(BF16).reshape(D_MODEL, N_FF_CHUNKS, FF_CHUNK).transpose(1, 0, 2)
    wd_c = wd.astype(BF16).reshape(N_FF_CHUNKS, FF_CHUNK, D_MODEL)
    return wg_c, wu_c, wd_c


def _proj_res_kernel(n_in, *refs):
    a_refs = refs[:n_in]
    w_refs = refs[n_in:2 * n_in]
    x_ref, mod_ref, g_ref, o_ref = refs[2 * n_in:]
    y = jnp.dot(a_refs[0][...].astype(BF16), w_refs[0][...], preferred_element_type=F32)
    for a_ref, w_ref in zip(a_refs[1:], w_refs[1:]):
        y = y + jnp.dot(a_ref[...].astype(BF16), w_ref[...], preferred_element_type=F32)
    gate_row = mod_ref[0, 5:6, :]
    o_ref[...] = x_ref[...] + gate_row * _rms(y, g_ref[...])


def _proj_res(acts, weights, x, mod_l, g_post):
    n_in = len(acts)
    in_specs = [pl.BlockSpec((TM, a.shape[1]), lambda i: (i, 0)) for a in acts]
    in_specs += [pl.BlockSpec(w.shape, lambda i: (0, 0)) for w in weights]
    in_specs += [
        pl.BlockSpec((TM, D_MODEL), lambda i: (i, 0)),
        pl.BlockSpec((1, N_MOD, D_MODEL), lambda i: (_group_of_block(i, TM), 0, 0)),
        pl.BlockSpec((1, D_MODEL), lambda i: (0, 0)),
    ]
    return pl.pallas_call(
        functools.partial(_proj_res_kernel, n_in),
        grid=(N_TOK // TM,),
        in_specs=in_specs,
        out_specs=pl.BlockSpec((TM, D_MODEL), lambda i: (i, 0)),
        out_shape=jax.ShapeDtypeStruct((N_TOK, D_MODEL), F32),
        compiler_params=_cparams("arbitrary"),
    )(*acts, *weights, x, mod_l, g_post)


def _even_in_kernel(x_ref, mod_ref, g_ref, wp_ref, gqa_ref, gkva_ref, wq_ref, cos_ref, sin_ref,
                    q_ref, ckv_ref, kr_ref, mq_ref, mk_ref, mv_ref, mo_ref, mg_ref):
    h = _modulated(x_ref[...], g_ref[...], mod_ref, 1).astype(BF16)
    p = jnp.dot(h, wp_ref[...], preferred_element_type=F32)
    cos = cos_ref[...]
    sin = sin_ref[...]
    half = MLA_ROPE // 4
    qn = _rms(p[:, E_QA:E_QA + MLA_Q_LORA], gqa_ref[...]).astype(BF16)
    q = jnp.dot(qn, wq_ref[...], preferred_element_type=F32)
    for hd in range(MLA_HEADS):
        sl = slice(hd * LANES, (hd + 1) * LANES)
        q_ref[:, sl] = _rope_tile(q[:, sl], cos, sin, half)
    ckv_ref[...] = _rms(p[:, E_KVA:E_KVA + MLA_KV_LORA], gkva_ref[...])
    kr_ref[...] = _rope_tile(p[:, E_KR:E_KR + LANES], cos, sin, half)
    mq_ref[...] = p[:, E_MQ:E_MK]
    mk_ref[...] = p[:, E_MK:E_MV]
    mv_ref[...] = p[:, E_MV:E_MO]
    mo_ref[...] = p[:, E_MO:E_MG]
    mg_ref[...] = p[:, E_MG:E_COLS]


def _even_in(x, mod_l, g_pre, wp, g_qa, g_kva, wq, cos_t, sin_t):
    row = lambda i: (i, 0)
    const = lambda i: (0, 0)
    widths = (MLA_HEADS * LANES, LANES, LANES, M_HEADS * M_DK, M_HEADS * M_DK, M_HEADS * M_DV,
              M_HEADS * M_DV, LANES)
    return pl.pallas_call(
        _even_in_kernel,
        grid=(N_TOK // TM,),
        in_specs=[
            pl.BlockSpec((TM, D_MODEL), row),
            pl.BlockSpec((1, N_MOD, D_MODEL), lambda i: (_group_of_block(i, TM), 0, 0)),
            pl.BlockSpec((1, D_MODEL), const),
            pl.BlockSpec((D_MODEL, E_COLS), const),
            pl.BlockSpec((1, MLA_Q_LORA), const),
            pl.BlockSpec((1, MLA_KV_LORA), const),
            pl.BlockSpec((MLA_Q_LORA, MLA_HEADS * LANES), const),
            pl.BlockSpec((TM, LANES), lambda i: (_rope_block(i, TM), 0)),
            pl.BlockSpec((TM, LANES), lambda i: (_rope_block(i, TM), 0)),
        ],
        out_specs=[pl.BlockSpec((TM, w), row) for w in widths],
        out_shape=[jax.ShapeDtypeStruct((N_TOK, w), F32) for w in widths],
        compiler_params=_cparams("arbitrary"),
    )(x, mod_l, g_pre, wp, g_qa, g_kva, wq, cos_t, sin_t)


def _mla_kernel(t_own, n_cache, tq, *refs):
    if n_cache:
        q_ref, ckv_ref, kr_ref, cckv_ref, ckr_ref, wk_ref, wv_ref, o_ref, ks_ref, vs_ref = refs
    else:
        q_ref, ckv_ref, kr_ref, wk_ref, wv_ref, o_ref, ks_ref, vs_ref = refs
    t_keys = n_cache + t_own

    @pl.when(pl.program_id(1) == 0)
    def _expand():
        if n_cache:
            ckv = jnp.concatenate([cckv_ref[0], ckv_ref[...]], axis=0)
            kr = jnp.concatenate([ckr_ref[0], kr_ref[...]], axis=0)
        else:
            ckv, kr = ckv_ref[...], kr_ref[...]
        ckv16 = ckv.astype(BF16)
        for hd in range(MLA_HEADS):
            ks_ref[hd] = (jnp.dot(ckv16, wk_ref[hd], preferred_element_type=F32) + kr).astype(BF16)
        for pr in range(MLA_HEADS // 2):
            vs_ref[pr] = jnp.dot(ckv16, wv_ref[pr], preferred_element_type=F32).astype(BF16)

    scale = MLA_QK ** -0.5
    lane = lax.broadcasted_iota(jnp.int32, (tq, LANES), 1)
    for pr in range(MLA_HEADS // 2):
        outs = []
        for j in range(2):
            hd = 2 * pr + j
            qh = q_ref[:, hd * LANES:(hd + 1) * LANES].astype(BF16)
            s = lax.dot_general(qh, ks_ref[hd], (((1,), (1,)), ((), ())), preferred_element_type=F32) * scale
            m = jnp.max(s, axis=-1, keepdims=True)
            e = jnp.exp(s - m)
            l = jnp.sum(e, axis=-1, keepdims=True)
            outs.append(jnp.dot(e.astype(BF16), vs_ref[pr], preferred_element_type=F32) / l)
        o_ref[:, pr * LANES:(pr + 1) * LANES] = jnp.where(lane < MLA_V, outs[0], outs[1])
    del t_keys


def _mla(q, ckv, kr, wk, wv, *, row0, n_batch, t_own, tq, cache=None):
    n_cache = 0 if cache is None else cache[0].shape[1]
    nq = t_own // tq
    qb0 = row0 // tq
    kb0 = row0 // t_own
    in_specs = [
        pl.BlockSpec((tq, MLA_HEADS * LANES), lambda b, j: (qb0 + b * nq + j, 0)),
        pl.BlockSpec((t_own, LANES), lambda b, j: (kb0 + b, 0)),
        pl.BlockSpec((t_own, LANES), lambda b, j: (kb0 + b, 0)),
    ]
    args = [q, ckv, kr]
    if n_cache:
        in_specs += [pl.BlockSpec((1, n_cache, LANES), lambda b, j: (b, 0, 0))] * 2
        args += list(cache)
    in_specs += [
        pl.BlockSpec((MLA_HEADS, LANES, LANES), lambda b, j: (0, 0, 0)),
        pl.BlockSpec((MLA_HEADS // 2, LANES, LANES), lambda b, j: (0, 0, 0)),
    ]
    args += [wk, wv]
    t_keys = n_cache + t_own
    return pl.pallas_call(
        functools.partial(_mla_kernel, t_own, n_cache, tq),
        grid=(n_batch, nq),
        in_specs=in_specs,
        out_specs=pl.BlockSpec((tq, MLA_HEADS * MLA_V), lambda b, j: (b * nq + j, 0)),
        out_shape=jax.ShapeDtypeStruct((n_batch * t_own, MLA_HEADS * MLA_V), F32),
        scratch_shapes=[pltpu.VMEM((MLA_HEADS, t_keys, LANES), BF16),
                        pltpu.VMEM((MLA_HEADS // 2, t_keys, LANES), BF16)],
        compiler_params=_cparams("arbitrary", "arbitrary"),
    )(*args)


def _log_sigmoid(x):
    return jnp.minimum(x, 0.0) - jnp.log1p(jnp.exp(-jnp.abs(x)))


def _mlstm_chunk(rev, r0, hd, gates, gates_t, cum, cum_t, mq_ref, mk_ref, mv_ref, c_ref, n_ref, m_ref, h_ref):
    L = M_CHUNK
    st = (M_HEADS if rev else 0) + hd
    gi = (2 * M_HEADS if rev else 0) + hd
    gf = gi + M_HEADS
    edge = 0 if rev else L - 1
    rows = pl.ds(r0, L)
    q = mq_ref[rows, hd * M_DK:(hd + 1) * M_DK]
    k = mk_ref[rows, hd * M_DK:(hd + 1) * M_DK] * (M_DK ** -0.5)
    v = mv_ref[rows, hd * M_DV:(hd + 1) * M_DV]
    q16, k16, v16 = q.astype(BF16), k.astype(BF16), v.astype(BF16)
    c_prev = c_ref[0, st]
    n_prev = n_ref[0, st:st + 1, :]
    m_prev = m_ref[0, st:st + 1, 0:1]

    b_col = cum[:, gf:gf + 1]
    b_row = cum_t[gf:gf + 1, :]
    li_row = gates_t[gi:gi + 1, :]
    li_col = gates[:, gi:gi + 1]
    t_idx = lax.broadcasted_iota(jnp.int32, (L, L), 0)
    s_idx = lax.broadcasted_iota(jnp.int32, (L, L), 1)
    allowed = (s_idx >= t_idx) if rev else (s_idx <= t_idx)
    dmat = jnp.where(allowed, b_col - b_row + li_row, NEG)
    inter = m_prev + b_col
    mt = jnp.maximum(inter, jnp.max(dmat, axis=-1, keepdims=True))
    w_inter = jnp.exp(inter - mt)
    qk = lax.dot_general(q16, k16, (((1,), (1,)), ((), ())), preferred_element_type=F32)
    a = qk * jnp.exp(dmat - mt)
    num = w_inter * jnp.dot(q16, c_prev.astype(BF16), preferred_element_type=F32) \
        + jnp.dot(a.astype(BF16), v16, preferred_element_type=F32)
    den = w_inter * jnp.sum(q * n_prev, axis=-1, keepdims=True) + jnp.sum(a, axis=-1, keepdims=True)
    h_ref[rows, hd * M_DV:(hd + 1) * M_DV] = num / jnp.maximum(jnp.abs(den), jnp.exp(-mt))

    m_new = mt[edge:edge + 1, :]
    b_last = b_col[edge:edge + 1, :]
    w_s = jnp.exp(b_last - b_col + li_col - m_new)
    w_c = jnp.exp(m_prev + b_last - m_new)
    kw = k * w_s
    c_ref[0, st] = w_c * c_prev + lax.dot_general(kw.astype(BF16), v16, (((0,), (0,)), ((), ())),
                                                  preferred_element_type=F32)
    n_ref[0, st:st + 1, :] = w_c * n_prev + jnp.sum(kw, axis=0, keepdims=True)
    m_ref[0, st:st + 1, :] = jnp.broadcast_to(m_new, (1, LANES))


def _mlstm_kernel(t_len, has_state, *refs):
    if has_state:
        (mq_ref, mk_ref, mv_ref, mo_ref, mg_ref, gb_ref, gout_ref, c0_ref, n0_ref, m0_ref,
         out_ref, c_ref, n_ref, m_ref, hf_ref, hb_ref) = refs
        c_ref[...] = c0_ref[...]
        n_ref[...] = n0_ref[...]
        m_ref[...] = m0_ref[...]
    else:
        (mq_ref, mk_ref, mv_ref, mo_ref, mg_ref, gb_ref, gout_ref,
         out_ref, c_ref, n_ref, m_ref, hf_ref, hb_ref) = refs
        c_ref[...] = jnp.zeros_like(c_ref)
        n_ref[...] = jnp.zeros_like(n_ref)
        m_ref[...] = jnp.zeros_like(m_ref)
    L = M_CHUNK
    nc = t_len // L
    r_idx = lax.broadcasted_iota(jnp.int32, (L, L), 0)
    c_idx = lax.broadcasted_iota(jnp.int32, (L, L), 1)
    tril = (c_idx <= r_idx).astype(F32)
    triu = (c_idx >= r_idx).astype(F32)
    lane = lax.broadcasted_iota(jnp.int32, (L, LANES), 1)
    is_forget = (lane % (2 * M_HEADS)) >= M_HEADS
    hi = lax.Precision.HIGHEST

    def gate_sums(r0, rev):
        g = mg_ref[pl.ds(r0, L), :] + gb_ref[...]
        g = jnp.where(is_forget, _log_sigmoid(g), g)
        g_t = g.T
        if rev:
            cum = jnp.dot(triu, g, precision=hi, preferred_element_type=F32)
            cum_t = jnp.dot(g_t, tril, precision=hi, preferred_element_type=F32)
        else:
            cum = jnp.dot(tril, g, precision=hi, preferred_element_type=F32)
            cum_t = jnp.dot(g_t, triu, precision=hi, preferred_element_type=F32)
        return g, g_t, cum, cum_t

    def step(c, carry):
        for rev in (False, True):
            r0 = pl.multiple_of((nc - 1 - c if rev else c) * L, L)
            g, g_t, cum, cum_t = gate_sums(r0, rev)
            for hd in range(M_HEADS):
                _mlstm_chunk(rev, r0, hd, g, g_t, cum, cum_t, mq_ref, mk_ref, mv_ref,
                             c_ref, n_ref, m_ref, hb_ref if rev else hf_ref)
        return carry

    lax.fori_loop(0, nc, step, 0)
    for hd in range(M_HEADS):
        sl = slice(hd * M_DV, (hd + 1) * M_DV)
        hm = hf_ref[:, sl] + hb_ref[:, sl]
        hm = hm * lax.rsqrt(jnp.mean(hm * hm, axis=-1, keepdims=True) + EPS) * gout_ref[:, sl]
        out_ref[:, sl] = jax.nn.sigmoid(mo_ref[:, sl]) * hm


def _mlstm(mq, mk, mv, mo, mg, gate_b, g_out, *, row0, n_batch, t_len, state=None):
    kb0 = row0 // t_len
    row = lambda b: (kb0 + b, 0)
    const = lambda b: (0, 0)
    st4 = lambda b: (b, 0, 0, 0)
    st3 = lambda b: (b, 0, 0)
    n_st = 2 * M_HEADS
    in_specs = [
        pl.BlockSpec((t_len, M_HEADS * M_DK), row),
        pl.BlockSpec((t_len, M_HEADS * M_DK), row),
        pl.BlockSpec((t_len, M_HEADS * M_DV), row),
        pl.BlockSpec((t_len, M_HEADS * M_DV), row),
        pl.BlockSpec((t_len, LANES), row),
        pl.BlockSpec((1, LANES), const),
        pl.BlockSpec((1, M_HEADS * M_DV), const),
    ]
    args = [mq, mk, mv, mo, mg, gate_b, g_out]
    if state is not None:
        in_specs += [pl.BlockSpec((1, n_st, M_DK, M_DV), st4), pl.BlockSpec((1, n_st, M_DK), st3),
                     pl.BlockSpec((1, n_st, LANES), st3)]
        args += list(state)
    return pl.pallas_call(
        functools.partial(_mlstm_kernel, t_len, state is not None),
        grid=(n_batch,),
        in_specs=in_specs,
        out_specs=[pl.BlockSpec((t_len, M_HEADS * M_DV), lambda b: (b, 0)),
                   pl.BlockSpec((1, n_st, M_DK, M_DV), st4), pl.BlockSpec((1, n_st, M_DK), st3),
                   pl.BlockSpec((1, n_st, LANES), st3)],
        out_shape=[jax.ShapeDtypeStruct((n_batch * t_len, M_HEADS * M_DV), F32),
                   jax.ShapeDtypeStruct((n_batch, n_st, M_DK, M_DV), F32),
                   jax.ShapeDtypeStruct((n_batch, n_st, M_DK), F32),
                   jax.ShapeDtypeStruct((n_batch, n_st, LANES), F32)],
        scratch_shapes=[pltpu.VMEM((t_len, M_HEADS * M_DV), F32), pltpu.VMEM((t_len, M_HEADS * M_DV), F32)],
        compiler_params=_cparams("arbitrary"),
    )(*args)


def _odd_in_kernel(x_ref, mod_ref, g_ref, wp_ref, cos_ref, sin_ref, q_ref, k_ref, v_ref):
    h = _modulated(x_ref[...], g_ref[...], mod_ref, 1).astype(BF16)
    p = jnp.dot(h, wp_ref[...], preferred_element_type=F32)
    cos = cos_ref[...]
    sin = sin_ref[...]
    half = S_HEAD_DIM // 4
    n_q = S_HEADS * S_HEAD_DIM
    n_k = S_KV_HEADS * LANES
    for t in range(n_q // LANES):
        sl = slice(t * LANES, (t + 1) * LANES)
        q_ref[:, sl] = _rope_tile(p[:, sl], cos, sin, half)
    for t in range(n_k // LANES):
        k_ref[:, t * LANES:(t + 1) * LANES] = _rope_tile(p[:, n_q + t * LANES:n_q + (t + 1) * LANES], cos, sin, half)
    v_ref[...] = p[:, n_q + n_k:]


def _odd_in(x, mod_l, g_pre, wp, cos_t, sin_t):
    row = lambda i: (i, 0)
    const = lambda i: (0, 0)
    widths = (S_HEADS * S_HEAD_DIM, S_KV_HEADS * LANES, S_KV_HEADS * LANES)
    return pl.pallas_call(
        _odd_in_kernel,
        grid=(N_TOK // TM,),
        in_specs=[
            pl.BlockSpec((TM, D_MODEL), row),
            pl.BlockSpec((1, N_MOD, D_MODEL), lambda i: (_group_of_block(i, TM), 0, 0)),
            pl.BlockSpec((1, D_MODEL), const),
            pl.BlockSpec((D_MODEL, sum(widths)), const),
            pl.BlockSpec((TM, LANES), lambda i: (_rope_block(i, TM), 0)),
            pl.BlockSpec((TM, LANES), lambda i: (_rope_block(i, TM), 0)),
        ],
        out_specs=[pl.BlockSpec((TM, w), row) for w in widths],
        out_shape=[jax.ShapeDtypeStruct((N_TOK, w), F32) for w in widths],
        compiler_params=_cparams("arbitrary"),
    )(x, mod_l, g_pre, wp, cos_t, sin_t)


def _gqa_heads(q_ref, keys, vals, sink_ref, mask, o_ref, tq):
    scale = S_HEAD_DIM ** -0.5
    lane = lax.broadcasted_iota(jnp.int32, (tq, LANES), 1)
    low = lane < S_HEAD_DIM
    for g in range(S_KV_HEADS):
        k2, v2 = keys(g), vals(g)
        for pr in range(S_GROUP // 2):
            col = (g * S_GROUP + 2 * pr) * S_HEAD_DIM
            qpair = q_ref[:, col:col + LANES]
            outs = []
            for j in range(2):
                qj = jnp.where(low if j == 0 else jnp.logical_not(low), qpair, 0.0).astype(BF16)
                s = lax.dot_general(qj, k2, (((1,), (1,)), ((), ())), preferred_element_type=F32) * scale
                if mask is not None:
                    s = jnp.where(mask, s, NEG)
                sk = sink_ref[g * S_GROUP + 2 * pr + j]
                m = jnp.maximum(jnp.max(s, axis=-1, keepdims=True), sk)
                e = jnp.exp(s - m)
                l = jnp.sum(e, axis=-1, keepdims=True) + jnp.exp(sk - m)
                outs.append(jnp.dot(e.astype(BF16), v2, preferred_element_type=F32) / l)
            o_ref[:, col:col + LANES] = jnp.where(low, outs[0], outs[1])


def _gqa_ctx_kernel(sink_ref, q_ref, k_ref, v_ref, o_ref):
    keys = lambda g: k_ref[:, g * LANES:(g + 1) * LANES].astype(BF16)
    vals = lambda g: v_ref[:, g * LANES:(g + 1) * LANES].astype(BF16)
    _gqa_heads(q_ref, keys, vals, sink_ref, None, o_ref, SEQ)


def _gqa_ctx(q, kd, vd, sink):
    row = lambda b: (b, 0)
    return pl.pallas_call(
        _gqa_ctx_kernel,
        grid=(BATCH,),
        in_specs=[
            pl.BlockSpec(memory_space=pltpu.SMEM),
            pl.BlockSpec((SEQ, S_HEADS * S_HEAD_DIM), row),
            pl.BlockSpec((SEQ, S_KV_HEADS * LANES), row),
            pl.BlockSpec((SEQ, S_KV_HEADS * LANES), row),
        ],
        out_specs=pl.BlockSpec((SEQ, S_HEADS * S_HEAD_DIM), row),
        out_shape=jax.ShapeDtypeStruct((N_CTX, S_HEADS * S_HEAD_DIM), F32),
        compiler_params=_cparams("arbitrary"),
    )(sink, q, kd, vd)


def _gqa_lat_kernel(sink_ref, q_ref, k_ref, v_ref, ck_ref, cv_ref, o_ref):
    n = pl.program_id(1)
    nb = DEC_SEQ // S_BLOCK
    B = S_BLOCK
    prev0 = pl.multiple_of(jnp.maximum(n - 1, 0) * B, B)
    cur0 = pl.multiple_of(n * B, B)
    next0 = pl.multiple_of(jnp.minimum(n + 1, nb - 1) * B, B)
    t_idx = lax.broadcasted_iota(jnp.int32, (B, PAST_LEN + 3 * B), 0)
    c_idx = lax.broadcasted_iota(jnp.int32, (B, PAST_LEN + 3 * B), 1)
    s_prev = c_idx - PAST_LEN
    s_next = c_idx - (PAST_LEN + 2 * B)
    in_prev = (c_idx >= PAST_LEN) & (c_idx < PAST_LEN + B)
    in_next = c_idx >= PAST_LEN + 2 * B
    far = jnp.int32(4 * B)
    bad_prev = in_prev & (s_prev < t_idx + jnp.where(n == 0, far, 0))
    bad_next = in_next & (s_next + jnp.where(n == nb - 1, far, 0) > t_idx)
    mask = jnp.logical_not(bad_prev | bad_next)

    def gather(ref, cache_ref, g):
        sl = slice(g * LANES, (g + 1) * LANES)
        return jnp.concatenate([cache_ref[0, g], ref[pl.ds(prev0, B), sl], ref[pl.ds(cur0, B), sl],
                                ref[pl.ds(next0, B), sl]], axis=0).astype(BF16)

    keys = lambda g: gather(k_ref, ck_ref, g)
    vals = lambda g: gather(v_ref, cv_ref, g)
    _gqa_heads(q_ref, keys, vals, sink_ref, mask, o_ref, B)


def _gqa_lat(q, kd, vd, cache_k2, cache_v2, sink):
    nb = DEC_SEQ // S_BLOCK
    qb0 = N_CTX // S_BLOCK
    kb0 = N_CTX // DEC_SEQ
    return pl.pallas_call(
        _gqa_lat_kernel,
        grid=(DEC_BATCH, nb),
        in_specs=[
            pl.BlockSpec(memory_space=pltpu.SMEM),
            pl.BlockSpec((S_BLOCK, S_HEADS * S_HEAD_DIM), lambda b, n: (qb0 + b * nb + n, 0)),
            pl.BlockSpec((DEC_SEQ, S_KV_HEADS * LANES), lambda b, n: (kb0 + b, 0)),
            pl.BlockSpec((DEC_SEQ, S_KV_HEADS * LANES), lambda b, n: (kb0 + b, 0)),
            pl.BlockSpec((1, S_KV_HEADS, PAST_LEN, LANES), lambda b, n: (b, 0, 0, 0)),
            pl.BlockSpec((1, S_KV_HEADS, PAST_LEN, LANES), lambda b, n: (b, 0, 0, 0)),
        ],
        out_specs=pl.BlockSpec((S_BLOCK, S_HEADS * S_HEAD_DIM), lambda b, n: (b * nb + n, 0)),
        out_shape=jax.ShapeDtypeStruct((N_LAT, S_HEADS * S_HEAD_DIM), F32),
        compiler_params=_cparams("arbitrary", "arbitrary"),
    )(sink, q, kd, vd, cache_k2, cache_v2)


def _rope_tables(rot_dim, lane_off, reps):
    nf = rot_dim // 4
    inv = ROPE_BASE ** (-jnp.arange(nf, dtype=F32) / nf)
    rows = DEC_SEQ // GRID_W
    row = jnp.repeat(jnp.arange(rows, dtype=F32), GRID_W)
    col = jnp.tile(jnp.arange(GRID_W, dtype=F32), rows)
    ang_r, ang_c = row[:, None] * inv, col[:, None] * inv
    cos_g = jnp.concatenate([jnp.cos(ang_r), jnp.cos(ang_r), jnp.cos(ang_c), jnp.cos(ang_c)], axis=1)
    sin_g = jnp.concatenate([-jnp.sin(ang_r), jnp.sin(ang_r), -jnp.sin(ang_c), jnp.sin(ang_c)], axis=1)
    cos_t = jnp.ones((DEC_SEQ, LANES), F32)
    sin_t = jnp.zeros((DEC_SEQ, LANES), F32)
    for r in range(reps):
        lo = lane_off + r * rot_dim
        cos_t = cos_t.at[:, lo:lo + rot_dim].set(cos_g)
        sin_t = sin_t.at[:, lo:lo + rot_dim].set(sin_g)
    cos_t = jnp.concatenate([jnp.ones((TM, LANES), F32), cos_t], axis=0)
    sin_t = jnp.concatenate([jnp.zeros((TM, LANES), F32), sin_t], axis=0)
    return cos_t, sin_t


def _even_weights(w_in, w_qb, w_kvb, gate_b):
    z = lambda n: jnp.zeros((D_MODEL, n), F32)
    idx = np.cumsum([MLA_Q_LORA, MLA_KV_LORA, MLA_ROPE, M_HEADS * M_DK, M_HEADS * M_DK, M_HEADS * M_DV,
                     M_HEADS * M_DV])
    q_a, kv_a, k_rope, mq, mk, mv, mo, mg = jnp.split(w_in, idx, axis=1)
    wp = jnp.concatenate([q_a, kv_a, z(KR_LANE), k_rope, z(LANES - KR_LANE - MLA_ROPE), mq, mk, mv, mo, mg,
                          z(LANES - 4 * M_HEADS)], axis=1).astype(BF16)
    wq = jnp.pad(w_qb.reshape(MLA_Q_LORA, MLA_HEADS, MLA_QK), ((0, 0), (0, 0), (0, LANES - MLA_QK)))
    wq = wq.reshape(MLA_Q_LORA, MLA_HEADS * LANES).astype(BF16)
    kvb = w_kvb.reshape(MLA_KV_LORA, MLA_HEADS, MLA_NOPE + MLA_V)
    wk = jnp.pad(kvb[:, :, :MLA_NOPE], ((0, 0), (0, 0), (0, LANES - MLA_NOPE))).transpose(1, 0, 2).astype(BF16)
    wv = kvb[:, :, MLA_NOPE:].reshape(MLA_KV_LORA, MLA_HEADS // 2, 2 * MLA_V).transpose(1, 0, 2).astype(BF16)
    gb = jnp.pad(gate_b, (0, LANES - 4 * M_HEADS)).reshape(1, LANES)
    return wp, wq, wk, wv, gb


def _dup_heads(w):
    w3 = w.reshape(D_MODEL, S_KV_HEADS, S_HEAD_DIM)
    return jnp.concatenate([w3, w3], axis=-1).reshape(D_MODEL, S_KV_HEADS * LANES)


def _odd_weights(w_in):
    n_q = S_HEADS * S_HEAD_DIM
    n_kv = S_KV_HEADS * S_HEAD_DIM
    return jnp.concatenate([w_in[:, :n_q], _dup_heads(w_in[:, n_q:n_q + n_kv]), _dup_heads(w_in[:, n_q + n_kv:])],
                           axis=1).astype(BF16)


def _undup(a):
    return a.reshape(BATCH, SEQ, S_KV_HEADS, 2, S_HEAD_DIM)[:, :, :, 0].transpose(0, 2, 1, 3)


def kernel(x_prompt, x_sample, cache_mla_ckv, cache_mla_krope, state_mlstm_C, state_mlstm_n, state_mlstm_m,
           cache_swa_k, cache_swa_v, c, c_ctx, ada_w, ada_b, norm_g, ffn_w_gate, ffn_w_up, ffn_w_down,
           even_w_in, mla_g_qa, mla_g_kva, mla_w_qb, mla_w_kvb, mlstm_gate_b, mlstm_g_out, even_w_out,
           odd_w_in, swa_sink, odd_w_out):
    x = jnp.concatenate([x_prompt.reshape(N_CTX, D_MODEL), x_sample.reshape(N_LAT, D_MODEL)], axis=0)
    cond = jnp.concatenate([c_ctx[None], c, jnp.zeros((COND_ROWS - N_GROUPS, D_MODEL), F32)], axis=0)
    mod = _adaln(cond, ada_w, ada_b)
    cos_e, sin_e = _rope_tables(MLA_ROPE, KR_LANE, 1)
    cos_o, sin_o = _rope_tables(S_HEAD_DIM, 0, LANES // S_HEAD_DIM)

    ckv_l, kr_l, mc_l, mn_l, mm_l, sk_l, sv_l = [], [], [], [], [], [], []
    for l in range(DEPTH):
        mod_l = mod[l]
        g = norm_g[l]
        i = l // 2
        x = _ffn(x, mod_l, g[0:2], *_ffn_weights(ffn_w_gate[l, 0], ffn_w_up[l, 0], ffn_w_down[l, 0]), 0)
        if l % 2 == 0:
            wp, wq, wk, wv, gb = _even_weights(even_w_in[i], mla_w_qb[i], mla_w_kvb[i], mlstm_gate_b[i])
            q, ckv, kr, mq, mk, mv, mo, mg = _even_in(
                x, mod_l, g[2:3], wp, mla_g_qa[i].reshape(1, -1), mla_g_kva[i].reshape(1, -1), wq, cos_e, sin_e)
            cache_kr = jnp.pad(cache_mla_krope[:, i], ((0, 0), (0, 0), (KR_LANE, LANES - KR_LANE - MLA_ROPE)))
            att_c = _mla(q, ckv, kr, wk, wv, row0=0, n_batch=BATCH, t_own=SEQ, tq=SEQ)
            att_l = _mla(q, ckv, kr, wk, wv, row0=N_CTX, n_batch=DEC_BATCH, t_own=DEC_SEQ, tq=256,
                         cache=(cache_mla_ckv[:, i], cache_kr))
            g_out = mlstm_g_out[i].reshape(1, -1)
            mo_c, s_c, s_n, s_m = _mlstm(mq, mk, mv, mo, mg, gb, g_out, row0=0, n_batch=BATCH, t_len=SEQ)
            n_st = 2 * M_HEADS
            state = (state_mlstm_C[:, i].reshape(DEC_BATCH, n_st, M_DK, M_DV),
                     state_mlstm_n[:, i].reshape(DEC_BATCH, n_st, M_DK),
                     jnp.broadcast_to(state_mlstm_m[:, i].reshape(DEC_BATCH, n_st, 1), (DEC_BATCH, n_st, LANES)))
            mo_l, _, _, _ = _mlstm(mq, mk, mv, mo, mg, gb, g_out, row0=N_CTX, n_batch=DEC_BATCH, t_len=DEC_SEQ,
                                   state=state)
            att = jnp.concatenate([att_c, att_l], axis=0)
            m_out = jnp.concatenate([mo_c, mo_l], axis=0)
            w_out = even_w_out[i].astype(BF16)
            n_att = MLA_HEADS * MLA_V
            x = _proj_res([att, m_out], [w_out[:n_att], w_out[n_att:]], x, mod_l, g[3:4])
            ckv_l.append(ckv[:N_CTX].reshape(BATCH, SEQ, MLA_KV_LORA))
            kr_l.append(kr[:N_CTX, KR_LANE:KR_LANE + MLA_ROPE].reshape(BATCH, SEQ, MLA_ROPE))
            mc_l.append(s_c.reshape(BATCH, 2, M_HEADS, M_DK, M_DV))
            mn_l.append(s_n.reshape(BATCH, 2, M_HEADS, M_DK))
            mm_l.append(s_m[:, :, 0].reshape(BATCH, 2, M_HEADS))
        else:
            q, kd, vd = _odd_in(x, mod_l, g[2:3], _odd_weights(odd_w_in[i]), cos_o, sin_o)
            dup = lambda a: jnp.concatenate([a, a], axis=-1)
            o_c = _gqa_ctx(q, kd, vd, swa_sink[i])
            o_l = _gqa_lat(q, kd, vd, dup(cache_swa_k[:, i]), dup(cache_swa_v[:, i]), swa_sink[i])
            o = jnp.concatenate([o_c, o_l], axis=0)
            x = _proj_res([o], [odd_w_out[i].astype(BF16)], x, mod_l, g[3:4])
            sk_l.append(_undup(kd[:N_CTX]))
            sv_l.append(_undup(vd[:N_CTX]))
        x = _ffn(x, mod_l, g[4:6], *_ffn_weights(ffn_w_gate[l, 1], ffn_w_up[l, 1], ffn_w_down[l, 1]), 2)

    return (x[:N_CTX].reshape(BATCH, SEQ, D_MODEL), x[N_CTX:].reshape(DEC_BATCH, DEC_SEQ, D_MODEL),
            jnp.stack(ckv_l, 1), jnp.stack(kr_l, 1), jnp.stack(mc_l, 1), jnp.stack(mn_l, 1), jnp.stack(mm_l, 1),
            jnp.stack(sk_l, 1), jnp.stack(sv_l, 1))
```

```python
import functools

import jax
import jax.numpy as jnp
import numpy as np
from jax import lax
from jax.experimental import pallas as pl
from jax.experimental.pallas import tpu as pltpu

F32 = jnp.float32
BF16 = jnp.bfloat16

D_MODEL = 1024
BATCH = 32
SEQ = 256
DEPTH = 4
DEC_BATCH = 4
DEC_SEQ = 1024
PAST_LEN = 512
GRID_W = 64
N_MOD = 9
D_FF = 2816
EPS = 1e-6
ROPE_BASE = 10000.0
NEG = -1e30
MLA_HEADS = 8
MLA_NOPE = 64
MLA_ROPE = 32
MLA_QK = MLA_NOPE + MLA_ROPE
MLA_V = 64
MLA_Q_LORA = 256
MLA_KV_LORA = 128
M_HEADS = 4
M_DK = 64
M_DV = 128
M_CHUNK = 128
S_HEADS = 16
S_KV_HEADS = 4
S_GROUP = S_HEADS // S_KV_HEADS
S_HEAD_DIM = 64
S_WINDOW = 128
S_BLOCK = 128

N_CTX = BATCH * SEQ
N_LAT = DEC_BATCH * DEC_SEQ
N_TOK = N_CTX + N_LAT
N_GROUPS = 1 + DEC_BATCH
COND_ROWS = 8

LANES = 128
FF_CHUNK = 256
N_FF_CHUNKS = D_FF // FF_CHUNK
assert N_FF_CHUNKS * FF_CHUNK == D_FF

TM = 512
ADA_TN = 1536
VMEM_LIMIT = 52 * 1024 * 1024

E_QA, E_KVA, E_KR, E_MQ, E_MK, E_MV, E_MO, E_MG, E_COLS = 0, 256, 384, 512, 768, 1024, 1536, 2048, 2176
KR_LANE = MLA_NOPE


def _cparams(*sem):
    return pltpu.CompilerParams(dimension_semantics=sem, vmem_limit_bytes=VMEM_LIMIT)


def _group_of_block(i, tm):
    n_ctx = N_CTX // tm
    per_lat = DEC_SEQ // tm
    return jnp.where(i < n_ctx, 0, 1 + (i - n_ctx) // per_lat)


def _rope_block(i, tm):
    n_ctx = N_CTX // tm
    per_lat = DEC_SEQ // tm
    return jnp.where(i < n_ctx, 0, 1 + (i - n_ctx) % per_lat)


def _rms(x, g):
    return x * lax.rsqrt(jnp.mean(x * x, axis=-1, keepdims=True) + EPS) * g


def _modulated(x, g_pre, mod_ref, sub):
    shift = mod_ref[0, 3 * sub:3 * sub + 1, :]
    scale = mod_ref[0, 3 * sub + 1:3 * sub + 2, :]
    return _rms(x, g_pre) * (1.0 + scale) + shift


def _silu(x):
    return x * jax.nn.sigmoid(x)


def _swap_halves(x, half):
    lane = lax.broadcasted_iota(jnp.int32, x.shape, 1)
    up = pltpu.roll(x, LANES - half, 1)
    down = pltpu.roll(x, half, 1)
    return jnp.where(lane % (2 * half) < half, up, down)


def _rope_tile(x, cos, sin, half):
    return x * cos + _swap_halves(x, half) * sin


def _adaln_kernel(cond_ref, w_ref, b_ref, o_ref):
    s = _silu(cond_ref[...]).astype(BF16)
    o_ref[0] = jnp.dot(s, w_ref[0].astype(BF16), preferred_element_type=F32) + b_ref[0]


def _adaln(cond, ada_w, ada_b):
    n_out = N_MOD * D_MODEL
    out = pl.pallas_call(
        _adaln_kernel,
        grid=(DEPTH, n_out // ADA_TN),
        in_specs=[
            pl.BlockSpec((COND_ROWS, D_MODEL), lambda l, j: (0, 0)),
            pl.BlockSpec((1, D_MODEL, ADA_TN), lambda l, j: (l, 0, j)),
            pl.BlockSpec((1, 1, ADA_TN), lambda l, j: (l, 0, j)),
        ],
        out_specs=pl.BlockSpec((1, COND_ROWS, ADA_TN), lambda l, j: (l, 0, j)),
        out_shape=jax.ShapeDtypeStruct((DEPTH, COND_ROWS, n_out), F32),
        compiler_params=_cparams("arbitrary", "arbitrary"),
        name="adaln",
    )(cond, ada_w, ada_b.reshape(DEPTH, 1, n_out))
    return out.reshape(DEPTH, COND_ROWS, N_MOD, D_MODEL)


def _ffn_kernel(sub, dual_in, dual_out, *refs):
    n_x = 2 if dual_in else 1
    n_o = 2 if dual_out else 1
    x_refs = refs[:n_x]
    mod_ref, g_ref, wg_ref, wu_ref, wd_ref = refs[n_x:n_x + 5]
    o_refs = refs[n_x + 5:n_x + 5 + n_o]
    a_ref = refs[-1]
    is_ctx = pl.program_id(0) < N_CTX // TM

    def load_x():
        if dual_in:
            return jnp.where(is_ctx, x_refs[0][...], x_refs[1][...])
        return x_refs[0][...]

    h = _modulated(load_x(), g_ref[0:1, :], mod_ref, sub).astype(BF16)
    for c in range(N_FF_CHUNKS):
        sl = slice(c * FF_CHUNK, (c + 1) * FF_CHUNK)
        gate = jnp.dot(h, wg_ref[:, sl], preferred_element_type=F32)
        up = jnp.dot(h, wu_ref[:, sl], preferred_element_type=F32)
        a_ref[:, sl] = (_silu(gate) * up).astype(BF16)
    y = jnp.dot(a_ref[...], wd_ref[...], preferred_element_type=F32)
    gate_row = mod_ref[0, 3 * sub + 2:3 * sub + 3, :]

    def result():
        return load_x() + 0.5 * gate_row * _rms(y, g_ref[1:2, :])

    if dual_out:
        @pl.when(is_ctx)
        def _ctx():
            o_refs[0][...] = result()

        @pl.when(jnp.logical_not(is_ctx))
        def _lat():
            o_refs[1][...] = result()
    else:
        o_refs[0][...] = result()


def _ctx_block(i):
    return jnp.minimum(i, N_CTX // TM - 1)


def _lat_block(i):
    return jnp.maximum(i - N_CTX // TM, 0)


def _dual_specs(width):
    return [pl.BlockSpec((TM, width), lambda i: (_ctx_block(i), 0)),
            pl.BlockSpec((TM, width), lambda i: (_lat_block(i), 0))]


def _ffn(xs, mod_l, g2, wg, wu, wd, sub, dual_out=False):
    dual_in = len(xs) == 2
    const = lambda i: (0, 0)
    x_specs = _dual_specs(D_MODEL) if dual_in else [pl.BlockSpec((TM, D_MODEL), lambda i: (i, 0))]
    if dual_out:
        out_specs = _dual_specs(D_MODEL)
        out_shape = [jax.ShapeDtypeStruct((N_CTX, D_MODEL), F32), jax.ShapeDtypeStruct((N_LAT, D_MODEL), F32)]
    else:
        out_specs = pl.BlockSpec((TM, D_MODEL), lambda i: (i, 0))
        out_shape = jax.ShapeDtypeStruct((N_TOK, D_MODEL), F32)
    return pl.pallas_call(
        functools.partial(_ffn_kernel, sub, dual_in, dual_out),
        grid=(N_TOK // TM,),
        in_specs=x_specs + [
            pl.BlockSpec((1, N_MOD, D_MODEL), lambda i: (_group_of_block(i, TM), 0, 0)),
            pl.BlockSpec((2, D_MODEL), const),
            pl.BlockSpec((D_MODEL, D_FF), const, pipeline_mode=pl.Buffered(1)),
            pl.BlockSpec((D_MODEL, D_FF), const, pipeline_mode=pl.Buffered(1)),
            pl.BlockSpec((D_FF, D_MODEL), const, pipeline_mode=pl.Buffered(1)),
        ],
        out_specs=out_specs,
        out_shape=out_shape,
        scratch_shapes=[pltpu.VMEM((TM, D_FF), BF16)],
        compiler_params=_cparams("arbitrary"),
        name="ffn",
    )(*xs, mod_l, g2, wg, wu, wd)


def _proj_res_kernel(n_in, *refs):
    a_refs = refs[:2 * n_in]
    w_refs = refs[2 * n_in:3 * n_in]
    x_ref, mod_ref, g_ref, o_ref = refs[3 * n_in:]
    is_ctx = pl.program_id(0) < N_CTX // TM
    y = None
    for k in range(n_in):
        a = jnp.where(is_ctx, a_refs[2 * k][...], a_refs[2 * k + 1][...]).astype(BF16)
        d = jnp.dot(a, w_refs[k][...], preferred_element_type=F32)
        y = d if y is None else y + d
    gate_row = mod_ref[0, 5:6, :]
    o_ref[...] = x_ref[...] + gate_row * _rms(y, g_ref[...])


def _proj_res(acts, weights, x, mod_l, g_post):
    n_in = len(acts)
    in_specs = []
    for a_c, _ in acts:
        in_specs += _dual_specs(a_c.shape[1])
    in_specs += [pl.BlockSpec(w.shape, lambda i: (0, 0)) for w in weights]
    in_specs += [
        pl.BlockSpec((TM, D_MODEL), lambda i: (i, 0)),
        pl.BlockSpec((1, N_MOD, D_MODEL), lambda i: (_group_of_block(i, TM), 0, 0)),
        pl.BlockSpec((1, D_MODEL), lambda i: (0, 0)),
    ]
    flat = [a for pair in acts for a in pair]
    return pl.pallas_call(
        functools.partial(_proj_res_kernel, n_in),
        grid=(N_TOK // TM,),
        in_specs=in_specs,
        out_specs=pl.BlockSpec((TM, D_MODEL), lambda i: (i, 0)),
        out_shape=jax.ShapeDtypeStruct((N_TOK, D_MODEL), F32),
        compiler_params=_cparams("arbitrary"),
        name="proj_res",
    )(*flat, *weights, x, mod_l, g_post)


def _even_in_kernel(x_ref, mod_ref, g_ref, wp_ref, gqa_ref, gkva_ref, wq_ref, cos_ref, sin_ref,
                    q_ref, ckv_ref, kr_ref, mq_ref, mk_ref, mv_ref, mo_ref, mg_ref, ckv_ctx_ref, kr_ctx_ref):
    h = _modulated(x_ref[...], g_ref[...], mod_ref, 1).astype(BF16)
    p = jnp.dot(h, wp_ref[...], preferred_element_type=F32)
    cos = cos_ref[...]
    sin = sin_ref[...]
    half = MLA_ROPE // 4
    qn = _rms(p[:, E_QA:E_QA + MLA_Q_LORA], gqa_ref[...]).astype(BF16)
    q = jnp.dot(qn, wq_ref[...], preferred_element_type=F32)
    for hd in range(MLA_HEADS):
        sl = slice(hd * LANES, (hd + 1) * LANES)
        q_ref[:, sl] = _rope_tile(q[:, sl], cos, sin, half)
    ckv = _rms(p[:, E_KVA:E_KVA + MLA_KV_LORA], gkva_ref[...])
    ckv_ref[...] = ckv
    kr = _rope_tile(p[:, E_KR:E_KR + LANES], cos, sin, half)
    kr_ref[...] = kr

    @pl.when(pl.program_id(0) < N_CTX // TM)
    def _ctx_state():
        ckv_ctx_ref[...] = ckv
        kr_ctx_ref[...] = kr[:, KR_LANE:KR_LANE + MLA_ROPE]

    mq_ref[...] = p[:, E_MQ:E_MK]
    mk_ref[...] = p[:, E_MK:E_MV]
    mv_ref[...] = p[:, E_MV:E_MO]
    mo_ref[...] = p[:, E_MO:E_MG]
    mg_ref[...] = p[:, E_MG:E_COLS]


def _even_in(x, mod_l, g_pre, wp, g_qa, g_kva, wq, cos_t, sin_t):
    row = lambda i: (i, 0)
    const = lambda i: (0, 0)
    widths = (MLA_HEADS * LANES, LANES, LANES, M_HEADS * M_DK, M_HEADS * M_DK, M_HEADS * M_DV,
              M_HEADS * M_DV, LANES)
    return pl.pallas_call(
        _even_in_kernel,
        grid=(N_TOK // TM,),
        in_specs=[
            pl.BlockSpec((TM, D_MODEL), row),
            pl.BlockSpec((1, N_MOD, D_MODEL), lambda i: (_group_of_block(i, TM), 0, 0)),
            pl.BlockSpec((1, D_MODEL), const),
            pl.BlockSpec((D_MODEL, E_COLS), const),
            pl.BlockSpec((1, MLA_Q_LORA), const),
            pl.BlockSpec((1, MLA_KV_LORA), const),
            pl.BlockSpec((MLA_Q_LORA, MLA_HEADS * LANES), const),
            pl.BlockSpec((TM, LANES), lambda i: (_rope_block(i, TM), 0)),
            pl.BlockSpec((TM, LANES), lambda i: (_rope_block(i, TM), 0)),
        ],
        out_specs=[pl.BlockSpec((TM, w), row) for w in widths]
        + [pl.BlockSpec((TM, MLA_KV_LORA), lambda i: (_ctx_block(i), 0)),
           pl.BlockSpec((TM, MLA_ROPE), lambda i: (_ctx_block(i), 0))],
        out_shape=[jax.ShapeDtypeStruct((N_TOK, w), F32) for w in widths]
        + [jax.ShapeDtypeStruct((N_CTX, MLA_KV_LORA), F32), jax.ShapeDtypeStruct((N_CTX, MLA_ROPE), F32)],
        compiler_params=_cparams("arbitrary"),
        name="even_in",
    )(x, mod_l, g_pre, wp, g_qa, g_kva, wq, cos_t, sin_t)


def _mla_kernel(t_own, n_cache, tq, *refs):
    if n_cache:
        q_ref, ckv_ref, kr_ref, cckv_ref, ckr_ref, wk_ref, wv_ref, o_ref, ks_ref, vs_ref = refs
    else:
        q_ref, ckv_ref, kr_ref, wk_ref, wv_ref, o_ref, ks_ref, vs_ref = refs
    t_keys = n_cache + t_own

    @pl.when(pl.program_id(1) == 0)
    def _expand():
        if n_cache:
            ckv = jnp.concatenate([cckv_ref[0], ckv_ref[...]], axis=0)
            kr = jnp.concatenate([ckr_ref[0], kr_ref[...]], axis=0)
        else:
            ckv, kr = ckv_ref[...], kr_ref[...]
        ckv16 = ckv.astype(BF16)
        for hd in range(MLA_HEADS):
            ks_ref[hd] = (jnp.dot(ckv16, wk_ref[hd], preferred_element_type=F32) + kr).astype(BF16)
        for pr in range(MLA_HEADS // 2):
            vs_ref[pr] = jnp.dot(ckv16, wv_ref[pr], preferred_element_type=F32).astype(BF16)

    scale = MLA_QK ** -0.5
    lane = lax.broadcasted_iota(jnp.int32, (tq, LANES), 1)
    for pr in range(MLA_HEADS // 2):
        outs = []
        for j in range(2):
            hd = 2 * pr + j
            qh = q_ref[:, hd * LANES:(hd + 1) * LANES].astype(BF16)
            s = lax.dot_general(qh, ks_ref[hd], (((1,), (1,)), ((), ())), preferred_element_type=F32) * scale
            m = jnp.max(s, axis=-1, keepdims=True)
            e = jnp.exp(s - m)
            l = jnp.sum(e, axis=-1, keepdims=True)
            outs.append(jnp.dot(e.astype(BF16), vs_ref[pr], preferred_element_type=F32) / l)
        o_ref[:, pr * LANES:(pr + 1) * LANES] = jnp.where(lane < MLA_V, outs[0], outs[1])
    del t_keys


def _mla(q, ckv, kr, wk, wv, *, row0, n_batch, t_own, tq, cache=None):
    n_cache = 0 if cache is None else cache[0].shape[1]
    nq = t_own // tq
    qb0 = row0 // tq
    kb0 = row0 // t_own
    in_specs = [
        pl.BlockSpec((tq, MLA_HEADS * LANES), lambda b, j: (qb0 + b * nq + j, 0)),
        pl.BlockSpec((t_own, LANES), lambda b, j: (kb0 + b, 0)),
        pl.BlockSpec((t_own, LANES), lambda b, j: (kb0 + b, 0)),
    ]
    args = [q, ckv, kr]
    if n_cache:
        in_specs += [pl.BlockSpec((1, n_cache, LANES), lambda b, j: (b, 0, 0))] * 2
        args += list(cache)
    in_specs += [
        pl.BlockSpec((MLA_HEADS, LANES, LANES), lambda b, j: (0, 0, 0)),
        pl.BlockSpec((MLA_HEADS // 2, LANES, LANES), lambda b, j: (0, 0, 0)),
    ]
    args += [wk, wv]
    t_keys = n_cache + t_own
    return pl.pallas_call(
        functools.partial(_mla_kernel, t_own, n_cache, tq),
        grid=(n_batch, nq),
        in_specs=in_specs,
        out_specs=pl.BlockSpec((tq, MLA_HEADS * MLA_V), lambda b, j: (b * nq + j, 0)),
        out_shape=jax.ShapeDtypeStruct((n_batch * t_own, MLA_HEADS * MLA_V), F32),
        scratch_shapes=[pltpu.VMEM((MLA_HEADS, t_keys, LANES), BF16),
                        pltpu.VMEM((MLA_HEADS // 2, t_keys, LANES), BF16)],
        compiler_params=_cparams("arbitrary", "arbitrary"),
        name="mla_lat" if n_cache else "mla_ctx",
    )(*args)


def _log_sigmoid(x):
    return jnp.minimum(x, 0.0) - jnp.log1p(jnp.exp(-jnp.abs(x)))


def _mlstm_chunk(rev, r0, hd, allowed, gates, gates_t, cum, cum_t, mq_ref, mk_ref, mv_ref, ct_ref, n_ref, m_ref,
                 h_ref):
    L = M_CHUNK
    nt = (((1,), (1,)), ((), ()))
    st = (M_HEADS if rev else 0) + hd
    gi = (2 * M_HEADS if rev else 0) + hd
    gf = gi + M_HEADS
    edge = 0 if rev else L - 1
    rows = pl.ds(r0, L)
    q16 = mq_ref[rows, hd * M_DK:(hd + 1) * M_DK].astype(BF16)
    k16 = (mk_ref[rows, hd * M_DK:(hd + 1) * M_DK] * (M_DK ** -0.5)).astype(BF16)
    v_t = mv_ref[rows, hd * M_DV:(hd + 1) * M_DV].T
    ct_prev = ct_ref[st, :, 0:M_DK]
    n_prev = n_ref[0, st:st + 1, :]
    m_prev = m_ref[0, st:st + 1, 0:1]

    b_row = cum_t[gf:gf + 1, :]
    li_row = gates_t[gi:gi + 1, :]
    e_col = gates[:, gi:gi + 1] - cum[:, gf:gf + 1]
    dmat = jnp.where(allowed, b_row + e_col, NEG)
    inter = m_prev + b_row
    mt = jnp.maximum(inter, jnp.max(dmat, axis=0, keepdims=True))
    w_inter = jnp.exp(inter - mt)
    qk = lax.dot_general(k16, q16, nt, preferred_element_type=F32)
    a = qk * jnp.exp(dmat - mt)
    n8 = jnp.broadcast_to(n_prev, (8, M_DK)).astype(BF16)
    nq = lax.dot_general(n8, q16, nt, preferred_element_type=F32)[0:1]
    num = w_inter * lax.dot_general(ct_prev.astype(BF16), q16, nt, preferred_element_type=F32) \
        + jnp.dot(v_t.astype(BF16), a.astype(BF16), preferred_element_type=F32)
    den = w_inter * nq + jnp.sum(a, axis=0, keepdims=True)
    h_t = num * (1.0 / jnp.maximum(jnp.abs(den), jnp.exp(-mt)))
    h_ref[rows, hd * M_DV:(hd + 1) * M_DV] = h_t.T

    m_new = mt[:, edge:edge + 1]
    b_last = b_row[:, edge:edge + 1]
    w_s = jnp.exp(b_last - b_row + li_row - m_new)
    w_c = jnp.exp(m_prev + b_last - m_new)
    ct_ref[st, :, 0:M_DK] = w_c * ct_prev + jnp.dot((v_t * w_s).astype(BF16), k16, preferred_element_type=F32)
    w8 = jnp.broadcast_to(w_s, (8, L)).astype(BF16)
    n_ref[0, st:st + 1, :] = w_c * n_prev + jnp.dot(w8, k16, preferred_element_type=F32)[0:1]
    m_ref[0, st:st + 1, :] = jnp.broadcast_to(m_new, (1, LANES))


def _mlstm_kernel(t_len, has_state, *refs):
    n_st = 2 * M_HEADS
    if has_state:
        (mq_ref, mk_ref, mv_ref, mo_ref, mg_ref, gb_ref, gout_ref, c0_ref, n0_ref, m0_ref,
         out_ref, c_ref, n_ref, m_ref, hf_ref, hb_ref, ct_ref) = refs
        pad = jnp.zeros((LANES - M_DK, M_DV), F32)
        for st in range(n_st):
            ct_ref[st] = jnp.concatenate([c0_ref[0, st], pad], axis=0).T
        n_ref[...] = n0_ref[...]
        m_ref[...] = m0_ref[...]
    else:
        (mq_ref, mk_ref, mv_ref, mo_ref, mg_ref, gb_ref, gout_ref,
         out_ref, c_ref, n_ref, m_ref, hf_ref, hb_ref, ct_ref) = refs
        ct_ref[...] = jnp.zeros_like(ct_ref)
        n_ref[...] = jnp.zeros_like(n_ref)
        m_ref[...] = jnp.zeros_like(m_ref)
    L = M_CHUNK
    nc = t_len // L
    r_idx = lax.broadcasted_iota(jnp.int32, (L, L), 0)
    c_idx = lax.broadcasted_iota(jnp.int32, (L, L), 1)
    lower = c_idx <= r_idx
    upper = c_idx >= r_idx
    tril = lower.astype(F32)
    triu = upper.astype(F32)
    lane = lax.broadcasted_iota(jnp.int32, (L, LANES), 1)
    is_forget = (lane % (2 * M_HEADS)) >= M_HEADS
    hi = lax.Precision.HIGHEST

    def gate_sums(r0, rev):
        g = mg_ref[pl.ds(r0, L), :] + gb_ref[...]
        g = jnp.where(is_forget, _log_sigmoid(g), g)
        g_t = g.T
        if rev:
            cum = jnp.dot(triu, g, precision=hi, preferred_element_type=F32)
            cum_t = jnp.dot(g_t, tril, precision=hi, preferred_element_type=F32)
        else:
            cum = jnp.dot(tril, g, precision=hi, preferred_element_type=F32)
            cum_t = jnp.dot(g_t, triu, precision=hi, preferred_element_type=F32)
        return g, g_t, cum, cum_t

    def step(c, carry):
        for rev in (False, True):
            r0 = pl.multiple_of((nc - 1 - c if rev else c) * L, L)
            g, g_t, cum, cum_t = gate_sums(r0, rev)
            allowed = lower if rev else upper
            for hd in range(M_HEADS):
                _mlstm_chunk(rev, r0, hd, allowed, g, g_t, cum, cum_t, mq_ref, mk_ref, mv_ref,
                             ct_ref, n_ref, m_ref, hb_ref if rev else hf_ref)
        return carry

    lax.fori_loop(0, nc, step, 0)
    for st in range(n_st):
        c_ref[0, st] = ct_ref[st].T[0:M_DK, :]
    for hd in range(M_HEADS):
        sl = slice(hd * M_DV, (hd + 1) * M_DV)
        hm = hf_ref[:, sl] + hb_ref[:, sl]
        hm = hm * lax.rsqrt(jnp.mean(hm * hm, axis=-1, keepdims=True) + EPS) * gout_ref[:, sl]
        out_ref[:, sl] = jax.nn.sigmoid(mo_ref[:, sl]) * hm


def _mlstm(mq, mk, mv, mo, mg, gate_b, g_out, *, row0, n_batch, t_len, state=None):
    kb0 = row0 // t_len
    row = lambda b: (kb0 + b, 0)
    const = lambda b: (0, 0)
    st4 = lambda b: (b, 0, 0, 0)
    st3 = lambda b: (b, 0, 0)
    n_st = 2 * M_HEADS
    in_specs = [
        pl.BlockSpec((t_len, M_HEADS * M_DK), row),
        pl.BlockSpec((t_len, M_HEADS * M_DK), row),
        pl.BlockSpec((t_len, M_HEADS * M_DV), row),
        pl.BlockSpec((t_len, M_HEADS * M_DV), row),
        pl.BlockSpec((t_len, LANES), row),
        pl.BlockSpec((1, LANES), const),
        pl.BlockSpec((1, M_HEADS * M_DV), const),
    ]
    args = [mq, mk, mv, mo, mg, gate_b, g_out]
    if state is not None:
        in_specs += [pl.BlockSpec((1, n_st, M_DK, M_DV), st4), pl.BlockSpec((1, n_st, M_DK), st3),
                     pl.BlockSpec((1, n_st, LANES), st3)]
        args += list(state)
    return pl.pallas_call(
        functools.partial(_mlstm_kernel, t_len, state is not None),
        grid=(n_batch,),
        in_specs=in_specs,
        out_specs=[pl.BlockSpec((t_len, M_HEADS * M_DV), lambda b: (b, 0)),
                   pl.BlockSpec((1, n_st, M_DK, M_DV), st4), pl.BlockSpec((1, n_st, M_DK), st3),
                   pl.BlockSpec((1, n_st, LANES), st3)],
        out_shape=[jax.ShapeDtypeStruct((n_batch * t_len, M_HEADS * M_DV), F32),
                   jax.ShapeDtypeStruct((n_batch, n_st, M_DK, M_DV), F32),
                   jax.ShapeDtypeStruct((n_batch, n_st, M_DK), F32),
                   jax.ShapeDtypeStruct((n_batch, n_st, LANES), F32)],
        scratch_shapes=[pltpu.VMEM((t_len, M_HEADS * M_DV), F32), pltpu.VMEM((t_len, M_HEADS * M_DV), F32),
                        pltpu.VMEM((n_st, M_DV, LANES), F32)],
        compiler_params=_cparams("arbitrary"),
        name="mlstm_lat" if state is not None else "mlstm_ctx",
    )(*args)


def _odd_in_kernel(x_ref, mod_ref, g_ref, wp_ref, cos_ref, sin_ref, q_ref, k_ref, v_ref, k_ctx_ref, v_ctx_ref):
    h = _modulated(x_ref[...], g_ref[...], mod_ref, 1).astype(BF16)
    p = jnp.dot(h, wp_ref[...], preferred_element_type=F32)
    cos = cos_ref[...]
    sin = sin_ref[...]
    half = S_HEAD_DIM // 4
    n_q = S_HEADS * S_HEAD_DIM
    n_k = S_KV_HEADS * LANES
    for t in range(n_q // LANES):
        sl = slice(t * LANES, (t + 1) * LANES)
        q_ref[:, sl] = _rope_tile(p[:, sl], cos, sin, half)
    for t in range(n_k // LANES):
        k_ref[:, t * LANES:(t + 1) * LANES] = _rope_tile(p[:, n_q + t * LANES:n_q + (t + 1) * LANES], cos, sin, half)
    v_ref[...] = p[:, n_q + n_k:]

    @pl.when(pl.program_id(0) < N_CTX // TM)
    def _ctx_state():
        for bb in range(TM // SEQ):
            rows = slice(bb * SEQ, (bb + 1) * SEQ)
            for g in range(S_KV_HEADS):
                k_ctx_ref[bb, g] = k_ref[rows, g * LANES:g * LANES + S_HEAD_DIM]
                v_ctx_ref[bb, g] = v_ref[rows, g * LANES:g * LANES + S_HEAD_DIM]


def _odd_in(x, mod_l, g_pre, wp, cos_t, sin_t):
    row = lambda i: (i, 0)
    const = lambda i: (0, 0)
    widths = (S_HEADS * S_HEAD_DIM, S_KV_HEADS * LANES, S_KV_HEADS * LANES)
    return pl.pallas_call(
        _odd_in_kernel,
        grid=(N_TOK // TM,),
        in_specs=[
            pl.BlockSpec((TM, D_MODEL), row),
            pl.BlockSpec((1, N_MOD, D_MODEL), lambda i: (_group_of_block(i, TM), 0, 0)),
            pl.BlockSpec((1, D_MODEL), const),
            pl.BlockSpec((D_MODEL, sum(widths)), const),
            pl.BlockSpec((TM, LANES), lambda i: (_rope_block(i, TM), 0)),
            pl.BlockSpec((TM, LANES), lambda i: (_rope_block(i, TM), 0)),
        ],
        out_specs=[pl.BlockSpec((TM, w), row) for w in widths]
        + [pl.BlockSpec((TM // SEQ, S_KV_HEADS, SEQ, S_HEAD_DIM), lambda i: (_ctx_block(i), 0, 0, 0))] * 2,
        out_shape=[jax.ShapeDtypeStruct((N_TOK, w), F32) for w in widths]
        + [jax.ShapeDtypeStruct((BATCH, S_KV_HEADS, SEQ, S_HEAD_DIM), F32)] * 2,
        compiler_params=_cparams("arbitrary"),
        name="odd_in",
    )(x, mod_l, g_pre, wp, cos_t, sin_t)


def _gqa_heads(q_ref, keys, vals, sink_ref, mask, o_ref, tq):
    scale = S_HEAD_DIM ** -0.5
    lane = lax.broadcasted_iota(jnp.int32, (tq, LANES), 1)
    low = lane < S_HEAD_DIM
    for g in range(S_KV_HEADS):
        k2, v2 = keys(g), vals(g)
        col = g * S_GROUP * S_HEAD_DIM
        tiles = [q_ref[:, col:col + LANES] * scale, q_ref[:, col + LANES:col + 2 * LANES] * scale]
        q4 = jnp.concatenate([jnp.where(low, tiles[0], 0.0), jnp.where(low, 0.0, tiles[0]),
                              jnp.where(low, tiles[1], 0.0), jnp.where(low, 0.0, tiles[1])], axis=0).astype(BF16)
        s = lax.dot_general(k2, q4, (((1,), (1,)), ((), ())), preferred_element_type=F32)
        if mask is not None:
            s = jnp.where(mask, s, NEG)
        sk = jnp.concatenate([jnp.full((1, tq), sink_ref[g * S_GROUP + j], F32) for j in range(S_GROUP)], axis=1)
        m = jnp.maximum(jnp.max(s, axis=0, keepdims=True), sk)
        e = jnp.exp(s - m)
        l = jnp.sum(e, axis=0, keepdims=True) + jnp.exp(sk - m)
        o_t = lax.dot_general(v2, e.astype(BF16), (((0,), (0,)), ((), ())), preferred_element_type=F32)
        o4 = (o_t * (1.0 / l)).T
        o_ref[:, col:col + LANES] = jnp.where(low, o4[0:tq], o4[tq:2 * tq])
        o_ref[:, col + LANES:col + 2 * LANES] = jnp.where(low, o4[2 * tq:3 * tq], o4[3 * tq:4 * tq])


def _gqa_ctx_kernel(sink_ref, q_ref, k_ref, v_ref, o_ref):
    keys = lambda g: k_ref[:, g * LANES:(g + 1) * LANES].astype(BF16)
    vals = lambda g: v_ref[:, g * LANES:(g + 1) * LANES].astype(BF16)
    _gqa_heads(q_ref, keys, vals, sink_ref, None, o_ref, SEQ)


def _gqa_ctx(q, kd, vd, sink):
    row = lambda b: (b, 0)
    return pl.pallas_call(
        _gqa_ctx_kernel,
        grid=(BATCH,),
        in_specs=[
            pl.BlockSpec(memory_space=pltpu.SMEM),
            pl.BlockSpec((SEQ, S_HEADS * S_HEAD_DIM), row),
            pl.BlockSpec((SEQ, S_KV_HEADS * LANES), row),
            pl.BlockSpec((SEQ, S_KV_HEADS * LANES), row),
        ],
        out_specs=pl.BlockSpec((SEQ, S_HEADS * S_HEAD_DIM), row),
        out_shape=jax.ShapeDtypeStruct((N_CTX, S_HEADS * S_HEAD_DIM), F32),
        compiler_params=_cparams("arbitrary"),
        name="gqa_ctx",
    )(sink, q, kd, vd)


def _gqa_lat_kernel(sink_ref, q_ref, k_ref, v_ref, ck_ref, cv_ref, o_ref):
    n = pl.program_id(1)
    nb = DEC_SEQ // S_BLOCK
    B = S_BLOCK
    prev0 = pl.multiple_of(jnp.maximum(n - 1, 0) * B, B)
    cur0 = pl.multiple_of(n * B, B)
    next0 = pl.multiple_of(jnp.minimum(n + 1, nb - 1) * B, B)
    t_idx = lax.broadcasted_iota(jnp.int32, (PAST_LEN + 3 * B, S_GROUP * B), 1) % B
    c_idx = lax.broadcasted_iota(jnp.int32, (PAST_LEN + 3 * B, S_GROUP * B), 0)
    s_prev = c_idx - PAST_LEN
    s_next = c_idx - (PAST_LEN + 2 * B)
    in_prev = (c_idx >= PAST_LEN) & (c_idx < PAST_LEN + B)
    in_next = c_idx >= PAST_LEN + 2 * B
    far = jnp.int32(4 * B)
    bad_prev = in_prev & (s_prev < t_idx + jnp.where(n == 0, far, 0))
    bad_next = in_next & (s_next + jnp.where(n == nb - 1, far, 0) > t_idx)
    mask = jnp.logical_not(bad_prev | bad_next)

    def gather(ref, cache_ref, g):
        sl = slice(g * LANES, (g + 1) * LANES)
        return jnp.concatenate([cache_ref[0, g], ref[pl.ds(prev0, B), sl], ref[pl.ds(cur0, B), sl],
                                ref[pl.ds(next0, B), sl]], axis=0).astype(BF16)

    keys = lambda g: gather(k_ref, ck_ref, g)
    vals = lambda g: gather(v_ref, cv_ref, g)
    _gqa_heads(q_ref, keys, vals, sink_ref, mask, o_ref, B)


def _gqa_lat(q, kd, vd, cache_k2, cache_v2, sink):
    nb = DEC_SEQ // S_BLOCK
    qb0 = N_CTX // S_BLOCK
    kb0 = N_CTX // DEC_SEQ
    return pl.pallas_call(
        _gqa_lat_kernel,
        grid=(DEC_BATCH, nb),
        in_specs=[
            pl.BlockSpec(memory_space=pltpu.SMEM),
            pl.BlockSpec((S_BLOCK, S_HEADS * S_HEAD_DIM), lambda b, n: (qb0 + b * nb + n, 0)),
            pl.BlockSpec((DEC_SEQ, S_KV_HEADS * LANES), lambda b, n: (kb0 + b, 0)),
            pl.BlockSpec((DEC_SEQ, S_KV_HEADS * LANES), lambda b, n: (kb0 + b, 0)),
            pl.BlockSpec((1, S_KV_HEADS, PAST_LEN, LANES), lambda b, n: (b, 0, 0, 0)),
            pl.BlockSpec((1, S_KV_HEADS, PAST_LEN, LANES), lambda b, n: (b, 0, 0, 0)),
        ],
        out_specs=pl.BlockSpec((S_BLOCK, S_HEADS * S_HEAD_DIM), lambda b, n: (b * nb + n, 0)),
        out_shape=jax.ShapeDtypeStruct((N_LAT, S_HEADS * S_HEAD_DIM), F32),
        compiler_params=_cparams("arbitrary", "arbitrary"),
        name="gqa_lat",
    )(sink, q, kd, vd, cache_k2, cache_v2)


def _rope_tables(rot_dim, lane_off, reps):
    nf = rot_dim // 4
    inv = ROPE_BASE ** (-jnp.arange(nf, dtype=F32) / nf)
    rows = DEC_SEQ // GRID_W
    row = jnp.repeat(jnp.arange(rows, dtype=F32), GRID_W)
    col = jnp.tile(jnp.arange(GRID_W, dtype=F32), rows)
    ang_r, ang_c = row[:, None] * inv, col[:, None] * inv
    cos_g = jnp.concatenate([jnp.cos(ang_r), jnp.cos(ang_r), jnp.cos(ang_c), jnp.cos(ang_c)], axis=1)
    sin_g = jnp.concatenate([-jnp.sin(ang_r), jnp.sin(ang_r), -jnp.sin(ang_c), jnp.sin(ang_c)], axis=1)
    cos_t = jnp.ones((DEC_SEQ, LANES), F32)
    sin_t = jnp.zeros((DEC_SEQ, LANES), F32)
    for r in range(reps):
        lo = lane_off + r * rot_dim
        cos_t = cos_t.at[:, lo:lo + rot_dim].set(cos_g)
        sin_t = sin_t.at[:, lo:lo + rot_dim].set(sin_g)
    cos_t = jnp.concatenate([jnp.ones((TM, LANES), F32), cos_t], axis=0)
    sin_t = jnp.concatenate([jnp.zeros((TM, LANES), F32), sin_t], axis=0)
    return cos_t, sin_t


def _even_weights(w_in, w_qb, w_kvb, gate_b):
    z = lambda n: jnp.zeros((D_MODEL, n), F32)
    idx = np.cumsum([MLA_Q_LORA, MLA_KV_LORA, MLA_ROPE, M_HEADS * M_DK, M_HEADS * M_DK, M_HEADS * M_DV,
                     M_HEADS * M_DV])
    q_a, kv_a, k_rope, mq, mk, mv, mo, mg = jnp.split(w_in, idx, axis=1)
    wp = jnp.concatenate([q_a, kv_a, z(KR_LANE), k_rope, z(LANES - KR_LANE - MLA_ROPE), mq, mk, mv, mo, mg,
                          z(LANES - 4 * M_HEADS)], axis=1).astype(BF16)
    wq = jnp.pad(w_qb.reshape(MLA_Q_LORA, MLA_HEADS, MLA_QK), ((0, 0), (0, 0), (0, LANES - MLA_QK)))
    wq = wq.reshape(MLA_Q_LORA, MLA_HEADS * LANES).astype(BF16)
    kvb = w_kvb.reshape(MLA_KV_LORA, MLA_HEADS, MLA_NOPE + MLA_V)
    wk = jnp.pad(kvb[:, :, :MLA_NOPE], ((0, 0), (0, 0), (0, LANES - MLA_NOPE))).transpose(1, 0, 2).astype(BF16)
    wv = kvb[:, :, MLA_NOPE:].reshape(MLA_KV_LORA, MLA_HEADS // 2, 2 * MLA_V).transpose(1, 0, 2).astype(BF16)
    gb = jnp.pad(gate_b, (0, LANES - 4 * M_HEADS)).reshape(1, LANES)
    return wp, wq, wk, wv, gb


def _dup_heads(w):
    w3 = w.reshape(D_MODEL, S_KV_HEADS, S_HEAD_DIM)
    return jnp.concatenate([w3, w3], axis=-1).reshape(D_MODEL, S_KV_HEADS * LANES)


def _odd_weights(w_in):
    n_q = S_HEADS * S_HEAD_DIM
    n_kv = S_KV_HEADS * S_HEAD_DIM
    return jnp.concatenate([w_in[:, :n_q], _dup_heads(w_in[:, n_q:n_q + n_kv]), _dup_heads(w_in[:, n_q + n_kv:])],
                           axis=1).astype(BF16)


def kernel(x_prompt, x_sample, cache_mla_ckv, cache_mla_krope, state_mlstm_C, state_mlstm_n, state_mlstm_m,
           cache_swa_k, cache_swa_v, c, c_ctx, ada_w, ada_b, norm_g, ffn_w_gate, ffn_w_up, ffn_w_down,
           even_w_in, mla_g_qa, mla_g_kva, mla_w_qb, mla_w_kvb, mlstm_gate_b, mlstm_g_out, even_w_out,
           odd_w_in, swa_sink, odd_w_out):
    xs = (x_prompt.reshape(N_CTX, D_MODEL), x_sample.reshape(N_LAT, D_MODEL))
    cond = jnp.concatenate([c_ctx[None], c, jnp.zeros((COND_ROWS - N_GROUPS, D_MODEL), F32)], axis=0)
    mod = _adaln(cond, ada_w, ada_b)
    cos_e, sin_e = _rope_tables(MLA_ROPE, KR_LANE, 1)
    cos_o, sin_o = _rope_tables(S_HEAD_DIM, 0, LANES // S_HEAD_DIM)
    wg16, wu16, wd16 = ffn_w_gate.astype(BF16), ffn_w_up.astype(BF16), ffn_w_down.astype(BF16)

    ckv_l, kr_l, mc_l, mn_l, mm_l, sk_l, sv_l = [], [], [], [], [], [], []
    for l in range(DEPTH):
        mod_l = mod[l]
        g = norm_g[l]
        i = l // 2
        x = _ffn(xs, mod_l, g[0:2], wg16[l, 0], wu16[l, 0], wd16[l, 0], 0)
        if l % 2 == 0:
            wp, wq, wk, wv, gb = _even_weights(even_w_in[i], mla_w_qb[i], mla_w_kvb[i], mlstm_gate_b[i])
            q, ckv, kr, mq, mk, mv, mo, mg, ckv_ctx, kr_ctx = _even_in(
                x, mod_l, g[2:3], wp, mla_g_qa[i].reshape(1, -1), mla_g_kva[i].reshape(1, -1), wq, cos_e, sin_e)
            cache_kr = jnp.pad(cache_mla_krope[:, i], ((0, 0), (0, 0), (KR_LANE, LANES - KR_LANE - MLA_ROPE)))
            att_c = _mla(q, ckv, kr, wk, wv, row0=0, n_batch=BATCH, t_own=SEQ, tq=SEQ)
            att_l = _mla(q, ckv, kr, wk, wv, row0=N_CTX, n_batch=DEC_BATCH, t_own=DEC_SEQ, tq=256,
                         cache=(cache_mla_ckv[:, i], cache_kr))
            g_out = mlstm_g_out[i].reshape(1, -1)
            mo_c, s_c, s_n, s_m = _mlstm(mq, mk, mv, mo, mg, gb, g_out, row0=0, n_batch=BATCH, t_len=SEQ)
            n_st = 2 * M_HEADS
            state = (state_mlstm_C[:, i].reshape(DEC_BATCH, n_st, M_DK, M_DV),
                     state_mlstm_n[:, i].reshape(DEC_BATCH, n_st, M_DK),
                     jnp.broadcast_to(state_mlstm_m[:, i].reshape(DEC_BATCH, n_st, 1), (DEC_BATCH, n_st, LANES)))
            mo_l, _, _, _ = _mlstm(mq, mk, mv, mo, mg, gb, g_out, row0=N_CTX, n_batch=DEC_BATCH, t_len=DEC_SEQ,
                                   state=state)
            w_out = even_w_out[i].astype(BF16)
            n_att = MLA_HEADS * MLA_V
            x = _proj_res([(att_c, att_l), (mo_c, mo_l)], [w_out[:n_att], w_out[n_att:]], x, mod_l, g[3:4])
            ckv_l.append(ckv_ctx.reshape(BATCH, SEQ, MLA_KV_LORA))
            kr_l.append(kr_ctx.reshape(BATCH, SEQ, MLA_ROPE))
            mc_l.append(s_c.reshape(BATCH, 2, M_HEADS, M_DK, M_DV))
            mn_l.append(s_n.reshape(BATCH, 2, M_HEADS, M_DK))
            mm_l.append(s_m[:, :, 0].reshape(BATCH, 2, M_HEADS))
        else:
            q, kd, vd, k_ctx, v_ctx = _odd_in(x, mod_l, g[2:3], _odd_weights(odd_w_in[i]), cos_o, sin_o)
            dup = lambda a: jnp.concatenate([a, a], axis=-1)
            o_c = _gqa_ctx(q, kd, vd, swa_sink[i])
            o_l = _gqa_lat(q, kd, vd, dup(cache_swa_k[:, i]), dup(cache_swa_v[:, i]), swa_sink[i])
            x = _proj_res([(o_c, o_l)], [odd_w_out[i].astype(BF16)], x, mod_l, g[3:4])
            sk_l.append(k_ctx)
            sv_l.append(v_ctx)
        last = l == DEPTH - 1
        xs = _ffn((x,), mod_l, g[4:6], wg16[l, 1], wu16[l, 1], wd16[l, 1], 2, dual_out=last)
        if not last:
            xs = (xs,)

    return (xs[0].reshape(BATCH, SEQ, D_MODEL), xs[1].reshape(DEC_BATCH, DEC_SEQ, D_MODEL),
            jnp.stack(ckv_l, 1), jnp.stack(kr_l, 1), jnp.stack(mc_l, 1), jnp.stack(mn_l, 1), jnp.stack(mm_l, 1),
            jnp.stack(sk_l, 1), jnp.stack(sv_l, 1))
```

```python
import functools

import jax
import jax.numpy as jnp
import numpy as np
from jax import lax
from jax.experimental import pallas as pl
from jax.experimental.pallas import tpu as pltpu

F32 = jnp.float32
BF16 = jnp.bfloat16

D_MODEL = 1024
BATCH = 32
SEQ = 256
DEPTH = 4
DEC_BATCH = 4
DEC_SEQ = 1024
PAST_LEN = 512
GRID_W = 64
N_MOD = 9
D_FF = 2816
EPS = 1e-6
ROPE_BASE = 10000.0
NEG = -1e30
MLA_HEADS = 8
MLA_NOPE = 64
MLA_ROPE = 32
MLA_QK = MLA_NOPE + MLA_ROPE
MLA_V = 64
MLA_Q_LORA = 256
MLA_KV_LORA = 128
M_HEADS = 4
M_DK = 64
M_DV = 128
M_CHUNK = 128
S_HEADS = 16
S_KV_HEADS = 4
S_GROUP = S_HEADS // S_KV_HEADS
S_HEAD_DIM = 64
S_WINDOW = 128
S_BLOCK = 128

N_CTX = BATCH * SEQ
N_LAT = DEC_BATCH * DEC_SEQ
N_TOK = N_CTX + N_LAT
N_GROUPS = 1 + DEC_BATCH
COND_ROWS = 8

LANES = 128
FF_CHUNK = 256
N_FF_CHUNKS = D_FF // FF_CHUNK
assert N_FF_CHUNKS * FF_CHUNK == D_FF

TM = 512
ADA_TN = 1536
VMEM_LIMIT = 52 * 1024 * 1024

E_QA, E_KVA, E_KR, E_MQ, E_MK, E_MV, E_MO, E_MG, E_COLS = 0, 256, 384, 512, 768, 1024, 1536, 2048, 2176
KR_LANE = MLA_NOPE


def _cparams(*sem):
    return pltpu.CompilerParams(dimension_semantics=sem, vmem_limit_bytes=VMEM_LIMIT)


def _group_of_block(i, tm):
    n_ctx = N_CTX // tm
    per_lat = DEC_SEQ // tm
    return jnp.where(i < n_ctx, 0, 1 + (i - n_ctx) // per_lat)


def _rope_block(i, tm):
    n_ctx = N_CTX // tm
    per_lat = DEC_SEQ // tm
    return jnp.where(i < n_ctx, 0, 1 + (i - n_ctx) % per_lat)


def _rms(x, g):
    return x * lax.rsqrt(jnp.mean(x * x, axis=-1, keepdims=True) + EPS) * g


def _modulated(x, g_pre, mod_ref, sub):
    shift = mod_ref[0, 3 * sub:3 * sub + 1, :]
    scale = mod_ref[0, 3 * sub + 1:3 * sub + 2, :]
    return _rms(x, g_pre) * (1.0 + scale) + shift


def _silu(x):
    return x * jax.nn.sigmoid(x)


def _swap_halves(x, half):
    lane = lax.broadcasted_iota(jnp.int32, x.shape, 1)
    up = pltpu.roll(x, LANES - half, 1)
    down = pltpu.roll(x, half, 1)
    return jnp.where(lane % (2 * half) < half, up, down)


def _rope_tile(x, cos, sin, half):
    return x * cos + _swap_halves(x, half) * sin


def _adaln_kernel(cond_ref, w_ref, b_ref, o_ref):
    s = _silu(cond_ref[...]).astype(BF16)
    o_ref[0] = jnp.dot(s, w_ref[0].astype(BF16), preferred_element_type=F32) + b_ref[0]


def _adaln(cond, ada_w, ada_b):
    n_out = N_MOD * D_MODEL
    out = pl.pallas_call(
        _adaln_kernel,
        grid=(DEPTH, n_out // ADA_TN),
        in_specs=[
            pl.BlockSpec((COND_ROWS, D_MODEL), lambda l, j: (0, 0)),
            pl.BlockSpec((1, D_MODEL, ADA_TN), lambda l, j: (l, 0, j)),
            pl.BlockSpec((1, 1, ADA_TN), lambda l, j: (l, 0, j)),
        ],
        out_specs=pl.BlockSpec((1, COND_ROWS, ADA_TN), lambda l, j: (l, 0, j)),
        out_shape=jax.ShapeDtypeStruct((DEPTH, COND_ROWS, n_out), F32),
        compiler_params=_cparams("arbitrary", "arbitrary"),
        name="adaln",
    )(cond, ada_w, ada_b.reshape(DEPTH, 1, n_out))
    return out.reshape(DEPTH, COND_ROWS, N_MOD, D_MODEL)


def _ffn_kernel(sub, dual_in, dual_out, *refs):
    n_x = 2 if dual_in else 1
    n_o = 2 if dual_out else 1
    x_refs = refs[:n_x]
    mod_ref, g_ref, wg_ref, wu_ref, wd_ref = refs[n_x:n_x + 5]
    o_refs = refs[n_x + 5:n_x + 5 + n_o]
    a_ref = refs[-1]
    is_ctx = pl.program_id(0) < N_CTX // TM

    def load_x():
        if dual_in:
            return jnp.where(is_ctx, x_refs[0][...], x_refs[1][...])
        return x_refs[0][...]

    h = _modulated(load_x(), g_ref[0:1, :], mod_ref, sub).astype(BF16)
    for c in range(N_FF_CHUNKS):
        sl = slice(c * FF_CHUNK, (c + 1) * FF_CHUNK)
        gate = jnp.dot(h, wg_ref[:, sl], preferred_element_type=F32)
        up = jnp.dot(h, wu_ref[:, sl], preferred_element_type=F32)
        a_ref[:, sl] = (_silu(gate) * up).astype(BF16)
    y = jnp.dot(a_ref[...], wd_ref[...], preferred_element_type=F32)
    gate_row = mod_ref[0, 3 * sub + 2:3 * sub + 3, :]

    def result():
        return load_x() + 0.5 * gate_row * _rms(y, g_ref[1:2, :])

    if dual_out:
        @pl.when(is_ctx)
        def _ctx():
            o_refs[0][...] = result()

        @pl.when(jnp.logical_not(is_ctx))
        def _lat():
            o_refs[1][...] = result()
    else:
        o_refs[0][...] = result()


def _ctx_block(i):
    return jnp.minimum(i, N_CTX // TM - 1)


def _lat_block(i):
    return jnp.maximum(i - N_CTX // TM, 0)


def _dual_specs(width):
    return [pl.BlockSpec((TM, width), lambda i: (_ctx_block(i), 0)),
            pl.BlockSpec((TM, width), lambda i: (_lat_block(i), 0))]


def _ffn(xs, mod_l, g2, wg, wu, wd, layer, half, dual_out=False):
    dual_in = len(xs) == 2
    sub = 2 * half
    const = lambda i: (0, 0)
    w_idx = lambda i: (layer, half, 0, 0)
    x_specs = _dual_specs(D_MODEL) if dual_in else [pl.BlockSpec((TM, D_MODEL), lambda i: (i, 0))]
    if dual_out:
        out_specs = _dual_specs(D_MODEL)
        out_shape = [jax.ShapeDtypeStruct((N_CTX, D_MODEL), F32), jax.ShapeDtypeStruct((N_LAT, D_MODEL), F32)]
    else:
        out_specs = pl.BlockSpec((TM, D_MODEL), lambda i: (i, 0))
        out_shape = jax.ShapeDtypeStruct((N_TOK, D_MODEL), F32)
    return pl.pallas_call(
        functools.partial(_ffn_kernel, sub, dual_in, dual_out),
        grid=(N_TOK // TM,),
        in_specs=x_specs + [
            pl.BlockSpec((1, N_MOD, D_MODEL), lambda i: (_group_of_block(i, TM), 0, 0)),
            pl.BlockSpec((2, D_MODEL), const),
            pl.BlockSpec((None, None, D_MODEL, D_FF), w_idx, pipeline_mode=pl.Buffered(1)),
            pl.BlockSpec((None, None, D_MODEL, D_FF), w_idx, pipeline_mode=pl.Buffered(1)),
            pl.BlockSpec((None, None, D_FF, D_MODEL), w_idx, pipeline_mode=pl.Buffered(1)),
        ],
        out_specs=out_specs,
        out_shape=out_shape,
        scratch_shapes=[pltpu.VMEM((TM, D_FF), BF16)],
        compiler_params=_cparams("arbitrary"),
        name="ffn",
    )(*xs, mod_l, g2, wg, wu, wd)


def _proj_res_kernel(n_in, *refs):
    a_refs = refs[:2 * n_in]
    w_refs = refs[2 * n_in:3 * n_in]
    x_ref, mod_ref, g_ref, o_ref = refs[3 * n_in:]
    is_ctx = pl.program_id(0) < N_CTX // TM
    y = None
    for k in range(n_in):
        a = jnp.where(is_ctx, a_refs[2 * k][...], a_refs[2 * k + 1][...])
        d = jnp.dot(a, w_refs[k][...], preferred_element_type=F32)
        y = d if y is None else y + d
    gate_row = mod_ref[0, 5:6, :]
    o_ref[...] = x_ref[...] + gate_row * _rms(y, g_ref[...])


def _proj_res(acts, weights, x, mod_l, g_post):
    n_in = len(acts)
    in_specs = []
    for a_c, _ in acts:
        in_specs += _dual_specs(a_c.shape[1])
    in_specs += [pl.BlockSpec(w.shape, lambda i: (0, 0)) for w in weights]
    in_specs += [
        pl.BlockSpec((TM, D_MODEL), lambda i: (i, 0)),
        pl.BlockSpec((1, N_MOD, D_MODEL), lambda i: (_group_of_block(i, TM), 0, 0)),
        pl.BlockSpec((1, D_MODEL), lambda i: (0, 0)),
    ]
    flat = [a for pair in acts for a in pair]
    return pl.pallas_call(
        functools.partial(_proj_res_kernel, n_in),
        grid=(N_TOK // TM,),
        in_specs=in_specs,
        out_specs=pl.BlockSpec((TM, D_MODEL), lambda i: (i, 0)),
        out_shape=jax.ShapeDtypeStruct((N_TOK, D_MODEL), F32),
        compiler_params=_cparams("arbitrary"),
        name="proj_res",
    )(*flat, *weights, x, mod_l, g_post)


def _even_in_kernel(x_ref, mod_ref, g_ref, wp_ref, gqa_ref, gkva_ref, wq_ref, cos_ref, sin_ref,
                    q_ref, ckv_ref, kr_ref, mq_ref, mk_ref, mv_ref, mo_ref, mg_ref, ckv_ctx_ref, kr_ctx_ref):
    h = _modulated(x_ref[...], g_ref[...], mod_ref, 1).astype(BF16)
    p = jnp.dot(h, wp_ref[...], preferred_element_type=F32)
    cos = cos_ref[...]
    sin = sin_ref[...]
    half = MLA_ROPE // 4
    qn = _rms(p[:, E_QA:E_QA + MLA_Q_LORA], gqa_ref[...]).astype(BF16)
    q = jnp.dot(qn, wq_ref[...], preferred_element_type=F32)
    for hd in range(MLA_HEADS):
        sl = slice(hd * LANES, (hd + 1) * LANES)
        q_ref[:, sl] = _rope_tile(q[:, sl], cos, sin, half).astype(BF16)
    ckv = _rms(p[:, E_KVA:E_KVA + MLA_KV_LORA], gkva_ref[...])
    ckv_ref[...] = ckv.astype(BF16)
    kr = _rope_tile(p[:, E_KR:E_KR + LANES], cos, sin, half)
    kr_ref[...] = kr.astype(BF16)

    @pl.when(pl.program_id(0) < N_CTX // TM)
    def _ctx_state():
        ckv_ctx_ref[...] = ckv
        kr_ctx_ref[...] = kr[:, KR_LANE:KR_LANE + MLA_ROPE]

    mq_ref[...] = p[:, E_MQ:E_MK].astype(BF16)
    mk_ref[...] = p[:, E_MK:E_MV].astype(BF16)
    mv_ref[...] = p[:, E_MV:E_MO].astype(BF16)
    mo_ref[...] = p[:, E_MO:E_MG]
    mg_ref[...] = p[:, E_MG:E_COLS]


def _even_in(x, mod_l, g_pre, wp, g_qa, g_kva, wq, cos_t, sin_t):
    row = lambda i: (i, 0)
    const = lambda i: (0, 0)
    widths = (MLA_HEADS * LANES, LANES, LANES, M_HEADS * M_DK, M_HEADS * M_DK, M_HEADS * M_DV,
              M_HEADS * M_DV, LANES)
    dtypes = (BF16, BF16, BF16, BF16, BF16, BF16, F32, F32)
    return pl.pallas_call(
        _even_in_kernel,
        grid=(N_TOK // TM,),
        in_specs=[
            pl.BlockSpec((TM, D_MODEL), row),
            pl.BlockSpec((1, N_MOD, D_MODEL), lambda i: (_group_of_block(i, TM), 0, 0)),
            pl.BlockSpec((1, D_MODEL), const),
            pl.BlockSpec((D_MODEL, E_COLS), const),
            pl.BlockSpec((1, MLA_Q_LORA), const),
            pl.BlockSpec((1, MLA_KV_LORA), const),
            pl.BlockSpec((MLA_Q_LORA, MLA_HEADS * LANES), const),
            pl.BlockSpec((TM, LANES), lambda i: (_rope_block(i, TM), 0)),
            pl.BlockSpec((TM, LANES), lambda i: (_rope_block(i, TM), 0)),
        ],
        out_specs=[pl.BlockSpec((TM, w), row) for w in widths]
        + [pl.BlockSpec((TM, MLA_KV_LORA), lambda i: (_ctx_block(i), 0)),
           pl.BlockSpec((TM, MLA_ROPE), lambda i: (_ctx_block(i), 0))],
        out_shape=[jax.ShapeDtypeStruct((N_TOK, w), dt) for w, dt in zip(widths, dtypes)]
        + [jax.ShapeDtypeStruct((N_CTX, MLA_KV_LORA), F32), jax.ShapeDtypeStruct((N_CTX, MLA_ROPE), F32)],
        compiler_params=_cparams("arbitrary"),
        name="even_in",
    )(x, mod_l, g_pre, wp, g_qa, g_kva, wq, cos_t, sin_t)


def _mla_kernel(t_own, n_cache, tq, *refs):
    if n_cache:
        q_ref, ckv_ref, kr_ref, cckv_ref, ckr_ref, wk_ref, wv_ref, o_ref, ks_ref, vs_ref = refs
    else:
        q_ref, ckv_ref, kr_ref, wk_ref, wv_ref, o_ref, ks_ref, vs_ref = refs
    t_keys = n_cache + t_own

    @pl.when(pl.program_id(1) == 0)
    def _expand():
        if n_cache:
            ckv16 = jnp.concatenate([cckv_ref[0].astype(BF16), ckv_ref[...]], axis=0)
            kr = jnp.concatenate([ckr_ref[0].astype(BF16), kr_ref[...]], axis=0)
        else:
            ckv16, kr = ckv_ref[...], kr_ref[...]
        for hd in range(MLA_HEADS):
            ks_ref[hd] = (jnp.dot(ckv16, wk_ref[hd], preferred_element_type=F32) + kr).astype(BF16)
        for pr in range(MLA_HEADS // 2):
            vs_ref[pr] = jnp.dot(ckv16, wv_ref[pr], preferred_element_type=F32).astype(BF16)

    scale = MLA_QK ** -0.5
    lane = lax.broadcasted_iota(jnp.int32, (tq, LANES), 1)
    for pr in range(MLA_HEADS // 2):
        outs = []
        for j in range(2):
            hd = 2 * pr + j
            qh = q_ref[:, hd * LANES:(hd + 1) * LANES]
            s = lax.dot_general(qh, ks_ref[hd], (((1,), (1,)), ((), ())), preferred_element_type=F32) * scale
            m = jnp.max(s, axis=-1, keepdims=True)
            e = jnp.exp(s - m)
            l = jnp.sum(e, axis=-1, keepdims=True)
            outs.append(jnp.dot(e.astype(BF16), vs_ref[pr], preferred_element_type=F32) / l)
        o_ref[:, pr * LANES:(pr + 1) * LANES] = jnp.where(lane < MLA_V, outs[0], outs[1]).astype(BF16)
    del t_keys


def _mla(q, ckv, kr, wk, wv, *, row0, n_batch, t_own, tq, cache=None):
    n_cache = 0 if cache is None else cache[0].shape[1]
    nq = t_own // tq
    qb0 = row0 // tq
    kb0 = row0 // t_own
    in_specs = [
        pl.BlockSpec((tq, MLA_HEADS * LANES), lambda b, j: (qb0 + b * nq + j, 0)),
        pl.BlockSpec((t_own, LANES), lambda b, j: (kb0 + b, 0)),
        pl.BlockSpec((t_own, LANES), lambda b, j: (kb0 + b, 0)),
    ]
    args = [q, ckv, kr]
    if n_cache:
        in_specs += [pl.BlockSpec((1, n_cache, LANES), lambda b, j: (b, 0, 0))] * 2
        args += list(cache)
    in_specs += [
        pl.BlockSpec((MLA_HEADS, LANES, LANES), lambda b, j: (0, 0, 0)),
        pl.BlockSpec((MLA_HEADS // 2, LANES, LANES), lambda b, j: (0, 0, 0)),
    ]
    args += [wk, wv]
    t_keys = n_cache + t_own
    return pl.pallas_call(
        functools.partial(_mla_kernel, t_own, n_cache, tq),
        grid=(n_batch, nq),
        in_specs=in_specs,
        out_specs=pl.BlockSpec((tq, MLA_HEADS * MLA_V), lambda b, j: (b * nq + j, 0)),
        out_shape=jax.ShapeDtypeStruct((n_batch * t_own, MLA_HEADS * MLA_V), BF16),
        scratch_shapes=[pltpu.VMEM((MLA_HEADS, t_keys, LANES), BF16),
                        pltpu.VMEM((MLA_HEADS // 2, t_keys, LANES), BF16)],
        compiler_params=_cparams("arbitrary", "arbitrary"),
        name="mla_lat" if n_cache else "mla_ctx",
    )(*args)


def _log_sigmoid(x):
    return jnp.minimum(x, 0.0) - jnp.log1p(jnp.exp(-jnp.abs(x)))


def _mlstm_chunk(rev, r0, hd, allowed, gates, gates_t, cum, cum_t, mq_ref, mk_ref, mv_ref, ct_ref, n_ref, m_ref,
                 h_ref):
    L = M_CHUNK
    nt = (((1,), (1,)), ((), ()))
    st = (M_HEADS if rev else 0) + hd
    gi = (2 * M_HEADS if rev else 0) + hd
    gf = gi + M_HEADS
    edge = 0 if rev else L - 1
    rows = pl.ds(r0, L)
    q16 = mq_ref[rows, hd * M_DK:(hd + 1) * M_DK]
    k16 = mk_ref[rows, hd * M_DK:(hd + 1) * M_DK] * (M_DK ** -0.5)
    v_t = mv_ref[rows, hd * M_DV:(hd + 1) * M_DV].astype(F32).T
    ct_prev = ct_ref[st, :, 0:M_DK]
    n_prev = n_ref[0, st:st + 1, :]
    m_prev = m_ref[0, st:st + 1, 0:1]

    b_row = cum_t[gf:gf + 1, :]
    li_row = gates_t[gi:gi + 1, :]
    e_col = gates[:, gi:gi + 1] - cum[:, gf:gf + 1]
    dmat = jnp.where(allowed, b_row + e_col, NEG)
    inter = m_prev + b_row
    mt = jnp.maximum(inter, jnp.max(dmat, axis=0, keepdims=True))
    w_inter = jnp.exp(inter - mt)
    qk = lax.dot_general(k16, q16, nt, preferred_element_type=F32)
    a = qk * jnp.exp(dmat - mt)
    n8 = jnp.broadcast_to(n_prev, (8, M_DK)).astype(BF16)
    nq = lax.dot_general(n8, q16, nt, preferred_element_type=F32)[0:1]
    num = w_inter * lax.dot_general(ct_prev.astype(BF16), q16, nt, preferred_element_type=F32) \
        + jnp.dot(v_t.astype(BF16), a.astype(BF16), preferred_element_type=F32)
    den = w_inter * nq + jnp.sum(a, axis=0, keepdims=True)
    h_t = num * (1.0 / jnp.maximum(jnp.abs(den), jnp.exp(-mt)))
    h_ref[rows, hd * M_DV:(hd + 1) * M_DV] = h_t.T

    m_new = mt[:, edge:edge + 1]
    b_last = b_row[:, edge:edge + 1]
    w_s = jnp.exp(b_last - b_row + li_row - m_new)
    w_c = jnp.exp(m_prev + b_last - m_new)
    ct_ref[st, :, 0:M_DK] = w_c * ct_prev + jnp.dot((v_t * w_s).astype(BF16), k16, preferred_element_type=F32)
    w8 = jnp.broadcast_to(w_s, (8, L)).astype(BF16)
    n_ref[0, st:st + 1, :] = w_c * n_prev + jnp.dot(w8, k16, preferred_element_type=F32)[0:1]
    m_ref[0, st:st + 1, :] = jnp.broadcast_to(m_new, (1, LANES))


def _mlstm_kernel(t_len, has_state, *refs):
    n_st = 2 * M_HEADS
    if has_state:
        (mq_ref, mk_ref, mv_ref, mo_ref, mg_ref, gb_ref, gout_ref, c0_ref, n0_ref, m0_ref,
         out_ref, c_ref, n_ref, m_ref, hf_ref, hb_ref, ct_ref) = refs
        pad = jnp.zeros((LANES - M_DK, M_DV), F32)
        for st in range(n_st):
            ct_ref[st] = jnp.concatenate([c0_ref[0, st], pad], axis=0).T
        n_ref[...] = n0_ref[...]
        m_ref[...] = m0_ref[...]
    else:
        (mq_ref, mk_ref, mv_ref, mo_ref, mg_ref, gb_ref, gout_ref,
         out_ref, c_ref, n_ref, m_ref, hf_ref, hb_ref, ct_ref) = refs
        ct_ref[...] = jnp.zeros_like(ct_ref)
        n_ref[...] = jnp.zeros_like(n_ref)
        m_ref[...] = jnp.zeros_like(m_ref)
    L = M_CHUNK
    nc = t_len // L
    r_idx = lax.broadcasted_iota(jnp.int32, (L, L), 0)
    c_idx = lax.broadcasted_iota(jnp.int32, (L, L), 1)
    lower = c_idx <= r_idx
    upper = c_idx >= r_idx
    tril = lower.astype(F32)
    triu = upper.astype(F32)
    lane = lax.broadcasted_iota(jnp.int32, (L, LANES), 1)
    is_forget = (lane % (2 * M_HEADS)) >= M_HEADS
    hi = lax.Precision.HIGHEST

    def gate_sums(r0, rev):
        g = mg_ref[pl.ds(r0, L), :] + gb_ref[...]
        g = jnp.where(is_forget, _log_sigmoid(g), g)
        g_t = g.T
        if rev:
            cum = jnp.dot(triu, g, precision=hi, preferred_element_type=F32)
            cum_t = jnp.dot(g_t, tril, precision=hi, preferred_element_type=F32)
        else:
            cum = jnp.dot(tril, g, precision=hi, preferred_element_type=F32)
            cum_t = jnp.dot(g_t, triu, precision=hi, preferred_element_type=F32)
        return g, g_t, cum, cum_t

    def step(c, carry):
        for rev in (False, True):
            r0 = pl.multiple_of((nc - 1 - c if rev else c) * L, L)
            g, g_t, cum, cum_t = gate_sums(r0, rev)
            allowed = lower if rev else upper
            for hd in range(M_HEADS):
                _mlstm_chunk(rev, r0, hd, allowed, g, g_t, cum, cum_t, mq_ref, mk_ref, mv_ref,
                             ct_ref, n_ref, m_ref, hb_ref if rev else hf_ref)
        return carry

    lax.fori_loop(0, nc, step, 0)
    for st in range(n_st):
        c_ref[0, st] = ct_ref[st].T[0:M_DK, :]
    for hd in range(M_HEADS):
        sl = slice(hd * M_DV, (hd + 1) * M_DV)
        hm = hf_ref[:, sl] + hb_ref[:, sl]
        hm = hm * lax.rsqrt(jnp.mean(hm * hm, axis=-1, keepdims=True) + EPS) * gout_ref[:, sl]
        out_ref[:, sl] = (jax.nn.sigmoid(mo_ref[:, sl]) * hm).astype(BF16)


def _mlstm(mq, mk, mv, mo, mg, gate_b, g_out, *, row0, n_batch, t_len, state=None):
    kb0 = row0 // t_len
    row = lambda b: (kb0 + b, 0)
    const = lambda b: (0, 0)
    st4 = lambda b: (b, 0, 0, 0)
    st3 = lambda b: (b, 0, 0)
    n_st = 2 * M_HEADS
    in_specs = [
        pl.BlockSpec((t_len, M_HEADS * M_DK), row),
        pl.BlockSpec((t_len, M_HEADS * M_DK), row),
        pl.BlockSpec((t_len, M_HEADS * M_DV), row),
        pl.BlockSpec((t_len, M_HEADS * M_DV), row),
        pl.BlockSpec((t_len, LANES), row),
        pl.BlockSpec((1, LANES), const),
        pl.BlockSpec((1, M_HEADS * M_DV), const),
    ]
    args = [mq, mk, mv, mo, mg, gate_b, g_out]
    if state is not None:
        in_specs += [pl.BlockSpec((1, n_st, M_DK, M_DV), st4), pl.BlockSpec((1, n_st, M_DK), st3),
                     pl.BlockSpec((1, n_st, LANES), st3)]
        args += list(state)
    return pl.pallas_call(
        functools.partial(_mlstm_kernel, t_len, state is not None),
        grid=(n_batch,),
        in_specs=in_specs,
        out_specs=[pl.BlockSpec((t_len, M_HEADS * M_DV), lambda b: (b, 0)),
                   pl.BlockSpec((1, n_st, M_DK, M_DV), st4), pl.BlockSpec((1, n_st, M_DK), st3),
                   pl.BlockSpec((1, n_st, LANES), st3)],
        out_shape=[jax.ShapeDtypeStruct((n_batch * t_len, M_HEADS * M_DV), BF16),
                   jax.ShapeDtypeStruct((n_batch, n_st, M_DK, M_DV), F32),
                   jax.ShapeDtypeStruct((n_batch, n_st, M_DK), F32),
                   jax.ShapeDtypeStruct((n_batch, n_st, LANES), F32)],
        scratch_shapes=[pltpu.VMEM((t_len, M_HEADS * M_DV), F32), pltpu.VMEM((t_len, M_HEADS * M_DV), F32),
                        pltpu.VMEM((n_st, M_DV, LANES), F32)],
        compiler_params=_cparams("arbitrary"),
        name="mlstm_lat" if state is not None else "mlstm_ctx",
    )(*args)


def _odd_in_kernel(x_ref, mod_ref, g_ref, wp_ref, cos_ref, sin_ref, q_ref, k_ref, v_ref, k_ctx_ref, v_ctx_ref):
    h = _modulated(x_ref[...], g_ref[...], mod_ref, 1).astype(BF16)
    p = jnp.dot(h, wp_ref[...], preferred_element_type=F32)
    cos = cos_ref[...]
    sin = sin_ref[...]
    half = S_HEAD_DIM // 4
    n_q = S_HEADS * S_HEAD_DIM
    n_k = S_KV_HEADS * LANES
    for t in range(n_q // LANES):
        sl = slice(t * LANES, (t + 1) * LANES)
        q_ref[:, sl] = _rope_tile(p[:, sl], cos, sin, half).astype(BF16)
    k_tiles = [_rope_tile(p[:, n_q + g * LANES:n_q + (g + 1) * LANES], cos, sin, half) for g in range(S_KV_HEADS)]
    v_tiles = [p[:, n_q + n_k + g * LANES:n_q + n_k + (g + 1) * LANES] for g in range(S_KV_HEADS)]
    for g in range(S_KV_HEADS):
        k_ref[:, g * LANES:(g + 1) * LANES] = k_tiles[g].astype(BF16)
        v_ref[:, g * LANES:(g + 1) * LANES] = v_tiles[g].astype(BF16)

    @pl.when(pl.program_id(0) < N_CTX // TM)
    def _ctx_state():
        for bb in range(TM // SEQ):
            rows = slice(bb * SEQ, (bb + 1) * SEQ)
            for g in range(S_KV_HEADS):
                k_ctx_ref[bb, g] = k_tiles[g][rows, 0:S_HEAD_DIM]
                v_ctx_ref[bb, g] = v_tiles[g][rows, 0:S_HEAD_DIM]


def _odd_in(x, mod_l, g_pre, wp, cos_t, sin_t):
    row = lambda i: (i, 0)
    const = lambda i: (0, 0)
    widths = (S_HEADS * S_HEAD_DIM, S_KV_HEADS * LANES, S_KV_HEADS * LANES)
    return pl.pallas_call(
        _odd_in_kernel,
        grid=(N_TOK // TM,),
        in_specs=[
            pl.BlockSpec((TM, D_MODEL), row),
            pl.BlockSpec((1, N_MOD, D_MODEL), lambda i: (_group_of_block(i, TM), 0, 0)),
            pl.BlockSpec((1, D_MODEL), const),
            pl.BlockSpec((D_MODEL, sum(widths)), const),
            pl.BlockSpec((TM, LANES), lambda i: (_rope_block(i, TM), 0)),
            pl.BlockSpec((TM, LANES), lambda i: (_rope_block(i, TM), 0)),
        ],
        out_specs=[pl.BlockSpec((TM, w), row) for w in widths]
        + [pl.BlockSpec((TM // SEQ, S_KV_HEADS, SEQ, S_HEAD_DIM), lambda i: (_ctx_block(i), 0, 0, 0))] * 2,
        out_shape=[jax.ShapeDtypeStruct((N_TOK, w), BF16) for w in widths]
        + [jax.ShapeDtypeStruct((BATCH, S_KV_HEADS, SEQ, S_HEAD_DIM), F32)] * 2,
        compiler_params=_cparams("arbitrary"),
        name="odd_in",
    )(x, mod_l, g_pre, wp, cos_t, sin_t)


def _gqa_heads(q_ref, keys, vals, sink_ref, mask, o_ref, tq):
    scale = S_HEAD_DIM ** -0.5
    lane = lax.broadcasted_iota(jnp.int32, (tq, LANES), 1)
    low = lane < S_HEAD_DIM
    for g in range(S_KV_HEADS):
        k2, v2 = keys(g), vals(g)
        col = g * S_GROUP * S_HEAD_DIM
        tiles = [q_ref[:, col:col + LANES].astype(F32) * scale, q_ref[:, col + LANES:col + 2 * LANES].astype(F32) * scale]
        q4 = jnp.concatenate([jnp.where(low, tiles[0], 0.0), jnp.where(low, 0.0, tiles[0]),
                              jnp.where(low, tiles[1], 0.0), jnp.where(low, 0.0, tiles[1])], axis=0).astype(BF16)
        s = lax.dot_general(k2, q4, (((1,), (1,)), ((), ())), preferred_element_type=F32)
        if mask is not None:
            s = jnp.where(mask, s, NEG)
        sk = jnp.concatenate([jnp.full((1, tq), sink_ref[g * S_GROUP + j], F32) for j in range(S_GROUP)], axis=1)
        m = jnp.maximum(jnp.max(s, axis=0, keepdims=True), sk)
        e = jnp.exp(s - m)
        l = jnp.sum(e, axis=0, keepdims=True) + jnp.exp(sk - m)
        o_t = lax.dot_general(v2, e.astype(BF16), (((0,), (0,)), ((), ())), preferred_element_type=F32)
        o4 = (o_t * (1.0 / l)).T
        o_ref[:, col:col + LANES] = jnp.where(low, o4[0:tq], o4[tq:2 * tq]).astype(BF16)
        o_ref[:, col + LANES:col + 2 * LANES] = jnp.where(low, o4[2 * tq:3 * tq], o4[3 * tq:4 * tq]).astype(BF16)


def _gqa_ctx_kernel(sink_ref, q_ref, k_ref, v_ref, o_ref):
    keys = lambda g: k_ref[:, g * LANES:(g + 1) * LANES]
    vals = lambda g: v_ref[:, g * LANES:(g + 1) * LANES]
    _gqa_heads(q_ref, keys, vals, sink_ref, None, o_ref, SEQ)


def _gqa_ctx(q, kd, vd, sink):
    row = lambda b: (b, 0)
    return pl.pallas_call(
        _gqa_ctx_kernel,
        grid=(BATCH,),
        in_specs=[
            pl.BlockSpec(memory_space=pltpu.SMEM),
            pl.BlockSpec((SEQ, S_HEADS * S_HEAD_DIM), row),
            pl.BlockSpec((SEQ, S_KV_HEADS * LANES), row),
            pl.BlockSpec((SEQ, S_KV_HEADS * LANES), row),
        ],
        out_specs=pl.BlockSpec((SEQ, S_HEADS * S_HEAD_DIM), row),
        out_shape=jax.ShapeDtypeStruct((N_CTX, S_HEADS * S_HEAD_DIM), BF16),
        compiler_params=_cparams("arbitrary"),
        name="gqa_ctx",
    )(sink, q, kd, vd)


def _gqa_lat_kernel(sink_ref, q_ref, k_ref, v_ref, ck_ref, cv_ref, o_ref):
    n = pl.program_id(1)
    nb = DEC_SEQ // S_BLOCK
    B = S_BLOCK
    prev0 = pl.multiple_of(jnp.maximum(n - 1, 0) * B, B)
    cur0 = pl.multiple_of(n * B, B)
    next0 = pl.multiple_of(jnp.minimum(n + 1, nb - 1) * B, B)
    t_idx = lax.broadcasted_iota(jnp.int32, (PAST_LEN + 3 * B, S_GROUP * B), 1) % B
    c_idx = lax.broadcasted_iota(jnp.int32, (PAST_LEN + 3 * B, S_GROUP * B), 0)
    s_prev = c_idx - PAST_LEN
    s_next = c_idx - (PAST_LEN + 2 * B)
    in_prev = (c_idx >= PAST_LEN) & (c_idx < PAST_LEN + B)
    in_next = c_idx >= PAST_LEN + 2 * B
    far = jnp.int32(4 * B)
    bad_prev = in_prev & (s_prev < t_idx + jnp.where(n == 0, far, 0))
    bad_next = in_next & (s_next + jnp.where(n == nb - 1, far, 0) > t_idx)
    mask = jnp.logical_not(bad_prev | bad_next)

    def gather(ref, cache_ref, g):
        sl = slice(g * LANES, (g + 1) * LANES)
        return jnp.concatenate([cache_ref[0, g], ref[pl.ds(prev0, B), sl], ref[pl.ds(cur0, B), sl],
                                ref[pl.ds(next0, B), sl]], axis=0)

    keys = lambda g: gather(k_ref, ck_ref, g)
    vals = lambda g: gather(v_ref, cv_ref, g)
    _gqa_heads(q_ref, keys, vals, sink_ref, mask, o_ref, B)


def _gqa_lat(q, kd, vd, cache_k2, cache_v2, sink):
    nb = DEC_SEQ // S_BLOCK
    qb0 = N_CTX // S_BLOCK
    kb0 = N_CTX // DEC_SEQ
    return pl.pallas_call(
        _gqa_lat_kernel,
        grid=(DEC_BATCH, nb),
        in_specs=[
            pl.BlockSpec(memory_space=pltpu.SMEM),
            pl.BlockSpec((S_BLOCK, S_HEADS * S_HEAD_DIM), lambda b, n: (qb0 + b * nb + n, 0)),
            pl.BlockSpec((DEC_SEQ, S_KV_HEADS * LANES), lambda b, n: (kb0 + b, 0)),
            pl.BlockSpec((DEC_SEQ, S_KV_HEADS * LANES), lambda b, n: (kb0 + b, 0)),
            pl.BlockSpec((1, S_KV_HEADS, PAST_LEN, LANES), lambda b, n: (b, 0, 0, 0)),
            pl.BlockSpec((1, S_KV_HEADS, PAST_LEN, LANES), lambda b, n: (b, 0, 0, 0)),
        ],
        out_specs=pl.BlockSpec((S_BLOCK, S_HEADS * S_HEAD_DIM), lambda b, n: (b * nb + n, 0)),
        out_shape=jax.ShapeDtypeStruct((N_LAT, S_HEADS * S_HEAD_DIM), BF16),
        compiler_params=_cparams("arbitrary", "arbitrary"),
        name="gqa_lat",
    )(sink, q, kd, vd, cache_k2, cache_v2)


def _rope_tables(rot_dim, lane_off, reps):
    nf = rot_dim // 4
    inv = np.float32(ROPE_BASE) ** (-np.arange(nf, dtype=np.float32) / np.float32(nf))
    pos = np.arange(DEC_SEQ)
    ang_r = (pos // GRID_W).astype(np.float32)[:, None] * inv
    ang_c = (pos % GRID_W).astype(np.float32)[:, None] * inv
    cos_g = np.concatenate([np.cos(ang_r), np.cos(ang_r), np.cos(ang_c), np.cos(ang_c)], axis=1)
    sin_g = np.concatenate([-np.sin(ang_r), np.sin(ang_r), -np.sin(ang_c), np.sin(ang_c)], axis=1)
    cos_t = np.ones((TM + DEC_SEQ, LANES), np.float32)
    sin_t = np.zeros((TM + DEC_SEQ, LANES), np.float32)
    for r in range(reps):
        lo = lane_off + r * rot_dim
        cos_t[TM:, lo:lo + rot_dim] = cos_g
        sin_t[TM:, lo:lo + rot_dim] = sin_g
    return jnp.asarray(cos_t, F32), jnp.asarray(sin_t, F32)


def _even_weights(w_in, w_qb, w_kvb, gate_b):
    z = lambda n: jnp.zeros((D_MODEL, n), F32)
    idx = np.cumsum([MLA_Q_LORA, MLA_KV_LORA, MLA_ROPE, M_HEADS * M_DK, M_HEADS * M_DK, M_HEADS * M_DV,
                     M_HEADS * M_DV])
    q_a, kv_a, k_rope, mq, mk, mv, mo, mg = jnp.split(w_in, idx, axis=1)
    wp = jnp.concatenate([q_a, kv_a, z(KR_LANE), k_rope, z(LANES - KR_LANE - MLA_ROPE), mq, mk, mv, mo, mg,
                          z(LANES - 4 * M_HEADS)], axis=1).astype(BF16)
    wq = jnp.pad(w_qb.reshape(MLA_Q_LORA, MLA_HEADS, MLA_QK), ((0, 0), (0, 0), (0, LANES - MLA_QK)))
    wq = wq.reshape(MLA_Q_LORA, MLA_HEADS * LANES).astype(BF16)
    kvb = w_kvb.reshape(MLA_KV_LORA, MLA_HEADS, MLA_NOPE + MLA_V)
    wk = jnp.pad(kvb[:, :, :MLA_NOPE], ((0, 0), (0, 0), (0, LANES - MLA_NOPE))).transpose(1, 0, 2).astype(BF16)
    wv = kvb[:, :, MLA_NOPE:].reshape(MLA_KV_LORA, MLA_HEADS // 2, 2 * MLA_V).transpose(1, 0, 2).astype(BF16)
    gb = jnp.pad(gate_b, (0, LANES - 4 * M_HEADS)).reshape(1, LANES)
    return wp, wq, wk, wv, gb


def _dup_heads(w):
    w3 = w.reshape(D_MODEL, S_KV_HEADS, S_HEAD_DIM)
    return jnp.concatenate([w3, w3], axis=-1).reshape(D_MODEL, S_KV_HEADS * LANES)


def _odd_weights(w_in):
    n_q = S_HEADS * S_HEAD_DIM
    n_kv = S_KV_HEADS * S_HEAD_DIM
    return jnp.concatenate([w_in[:, :n_q], _dup_heads(w_in[:, n_q:n_q + n_kv]), _dup_heads(w_in[:, n_q + n_kv:])],
                           axis=1).astype(BF16)


def kernel(x_prompt, x_sample, cache_mla_ckv, cache_mla_krope, state_mlstm_C, state_mlstm_n, state_mlstm_m,
           cache_swa_k, cache_swa_v, c, c_ctx, ada_w, ada_b, norm_g, ffn_w_gate, ffn_w_up, ffn_w_down,
           even_w_in, mla_g_qa, mla_g_kva, mla_w_qb, mla_w_kvb, mlstm_gate_b, mlstm_g_out, even_w_out,
           odd_w_in, swa_sink, odd_w_out):
    xs = (x_prompt.reshape(N_CTX, D_MODEL), x_sample.reshape(N_LAT, D_MODEL))
    cond = jnp.concatenate([c_ctx[None], c, jnp.zeros((COND_ROWS - N_GROUPS, D_MODEL), F32)], axis=0)
    mod = _adaln(cond, ada_w, ada_b)
    cos_e, sin_e = _rope_tables(MLA_ROPE, KR_LANE, 1)
    cos_o, sin_o = _rope_tables(S_HEAD_DIM, 0, LANES // S_HEAD_DIM)
    wg16, wu16, wd16 = ffn_w_gate.astype(BF16), ffn_w_up.astype(BF16), ffn_w_down.astype(BF16)

    ckv_l, kr_l, mc_l, mn_l, mm_l, sk_l, sv_l = [], [], [], [], [], [], []
    for l in range(DEPTH):
        mod_l = mod[l]
        g = norm_g[l]
        i = l // 2
        x = _ffn(xs, mod_l, g[0:2], wg16, wu16, wd16, l, 0)
        if l % 2 == 0:
            wp, wq, wk, wv, gb = _even_weights(even_w_in[i], mla_w_qb[i], mla_w_kvb[i], mlstm_gate_b[i])
            q, ckv, kr, mq, mk, mv, mo, mg, ckv_ctx, kr_ctx = _even_in(
                x, mod_l, g[2:3], wp, mla_g_qa[i].reshape(1, -1), mla_g_kva[i].reshape(1, -1), wq, cos_e, sin_e)
            cache_kr = jnp.pad(cache_mla_krope[:, i], ((0, 0), (0, 0), (KR_LANE, LANES - KR_LANE - MLA_ROPE)))
            att_c = _mla(q, ckv, kr, wk, wv, row0=0, n_batch=BATCH, t_own=SEQ, tq=SEQ)
            att_l = _mla(q, ckv, kr, wk, wv, row0=N_CTX, n_batch=DEC_BATCH, t_own=DEC_SEQ, tq=512,
                         cache=(cache_mla_ckv[:, i], cache_kr))
            g_out = mlstm_g_out[i].reshape(1, -1)
            mo_c, s_c, s_n, s_m = _mlstm(mq, mk, mv, mo, mg, gb, g_out, row0=0, n_batch=BATCH, t_len=SEQ)
            n_st = 2 * M_HEADS
            state = (state_mlstm_C[:, i].reshape(DEC_BATCH, n_st, M_DK, M_DV),
                     state_mlstm_n[:, i].reshape(DEC_BATCH, n_st, M_DK),
                     jnp.broadcast_to(state_mlstm_m[:, i].reshape(DEC_BATCH, n_st, 1), (DEC_BATCH, n_st, LANES)))
            mo_l, _, _, _ = _mlstm(mq, mk, mv, mo, mg, gb, g_out, row0=N_CTX, n_batch=DEC_BATCH, t_len=DEC_SEQ,
                                   state=state)
            w_out = even_w_out[i].astype(BF16)
            n_att = MLA_HEADS * MLA_V
            x = _proj_res([(att_c, att_l), (mo_c, mo_l)], [w_out[:n_att], w_out[n_att:]], x, mod_l, g[3:4])
            ckv_l.append(ckv_ctx.reshape(BATCH, SEQ, MLA_KV_LORA))
            kr_l.append(kr_ctx.reshape(BATCH, SEQ, MLA_ROPE))
            mc_l.append(s_c.reshape(BATCH, 2, M_HEADS, M_DK, M_DV))
            mn_l.append(s_n.reshape(BATCH, 2, M_HEADS, M_DK))
            mm_l.append(s_m[:, :, 0].reshape(BATCH, 2, M_HEADS))
        else:
            q, kd, vd, k_ctx, v_ctx = _odd_in(x, mod_l, g[2:3], _odd_weights(odd_w_in[i]), cos_o, sin_o)
            dup = lambda a: jnp.concatenate([a, a], axis=-1).astype(BF16)
            o_c = _gqa_ctx(q, kd, vd, swa_sink[i])
            o_l = _gqa_lat(q, kd, vd, dup(cache_swa_k[:, i]), dup(cache_swa_v[:, i]), swa_sink[i])
            x = _proj_res([(o_c, o_l)], [odd_w_out[i].astype(BF16)], x, mod_l, g[3:4])
            sk_l.append(k_ctx)
            sv_l.append(v_ctx)
        last = l == DEPTH - 1
        xs = _ffn((x,), mod_l, g[4:6], wg16, wu16, wd16, l, 1, dual_out=last)
        if not last:
            xs = (xs,)

    return (xs[0].reshape(BATCH, SEQ, D_MODEL), xs[1].reshape(DEC_BATCH, DEC_SEQ, D_MODEL),
            jnp.stack(ckv_l, 1), jnp.stack(kr_l, 1), jnp.stack(mc_l, 1), jnp.stack(mn_l, 1), jnp.stack(mm_l, 1),
            jnp.stack(sk_l, 1), jnp.stack(sv_l, 1))
```

```python
import functools

import jax
import jax.numpy as jnp
import numpy as np
from jax import lax
from jax.experimental import pallas as pl
from jax.experimental.pallas import tpu as pltpu

F32 = jnp.float32
BF16 = jnp.bfloat16

D_MODEL = 1024
BATCH = 32
SEQ = 256
DEPTH = 4
DEC_BATCH = 4
DEC_SEQ = 1024
PAST_LEN = 512
GRID_W = 64
N_MOD = 9
D_FF = 2816
EPS = 1e-6
ROPE_BASE = 10000.0
NEG = -1e30
MLA_HEADS = 8
MLA_NOPE = 64
MLA_ROPE = 32
MLA_QK = MLA_NOPE + MLA_ROPE
MLA_V = 64
MLA_Q_LORA = 256
MLA_KV_LORA = 128
M_HEADS = 4
M_DK = 64
M_DV = 128
M_CHUNK = 128
S_HEADS = 16
S_KV_HEADS = 4
S_GROUP = S_HEADS // S_KV_HEADS
S_HEAD_DIM = 64
S_WINDOW = 128
S_BLOCK = 128

N_CTX = BATCH * SEQ
N_LAT = DEC_BATCH * DEC_SEQ
N_TOK = N_CTX + N_LAT
N_GROUPS = 1 + DEC_BATCH
COND_ROWS = 8

LANES = 128
FF_CHUNK = 256
N_FF_CHUNKS = D_FF // FF_CHUNK
assert N_FF_CHUNKS * FF_CHUNK == D_FF

TM = 512
ADA_TN = 1536
VMEM_LIMIT = 52 * 1024 * 1024

E_QA, E_KVA, E_KR, E_MQ, E_MK, E_MV, E_MO, E_MG, E_COLS = 0, 256, 384, 512, 768, 1024, 1536, 2048, 2176
KR_LANE = MLA_NOPE


def _cparams(*sem):
    return pltpu.CompilerParams(dimension_semantics=sem, vmem_limit_bytes=VMEM_LIMIT)


def _group_of_block(i, tm):
    n_ctx = N_CTX // tm
    per_lat = DEC_SEQ // tm
    return jnp.where(i < n_ctx, 0, 1 + (i - n_ctx) // per_lat)


def _rope_block(i, tm):
    n_ctx = N_CTX // tm
    per_lat = DEC_SEQ // tm
    return jnp.where(i < n_ctx, 0, 1 + (i - n_ctx) % per_lat)


def _rms(x, g):
    return x * lax.rsqrt(jnp.mean(x * x, axis=-1, keepdims=True) + EPS) * g


def _modulated(x, g_pre, mod_ref, sub):
    shift = mod_ref[0, 3 * sub:3 * sub + 1, :]
    scale = mod_ref[0, 3 * sub + 1:3 * sub + 2, :]
    return _rms(x, g_pre) * (1.0 + scale) + shift


def _silu(x):
    return x * jax.nn.sigmoid(x)


def _swap_halves(x, half):
    lane = lax.broadcasted_iota(jnp.int32, x.shape, 1)
    up = pltpu.roll(x, LANES - half, 1)
    down = pltpu.roll(x, half, 1)
    return jnp.where(lane % (2 * half) < half, up, down)


def _rope_tile(x, cos, sin, half):
    return x * cos + _swap_halves(x, half) * sin


def _adaln_kernel(cond_ref, w_ref, b_ref, o_ref):
    s = _silu(cond_ref[...]).astype(BF16)
    o_ref[0] = jnp.dot(s, w_ref[0].astype(BF16), preferred_element_type=F32) + b_ref[0]


def _adaln(cond, ada_w, ada_b):
    n_out = N_MOD * D_MODEL
    out = pl.pallas_call(
        _adaln_kernel,
        grid=(DEPTH, n_out // ADA_TN),
        in_specs=[
            pl.BlockSpec((COND_ROWS, D_MODEL), lambda l, j: (0, 0)),
            pl.BlockSpec((1, D_MODEL, ADA_TN), lambda l, j: (l, 0, j)),
            pl.BlockSpec((1, 1, ADA_TN), lambda l, j: (l, 0, j)),
        ],
        out_specs=pl.BlockSpec((1, COND_ROWS, ADA_TN), lambda l, j: (l, 0, j)),
        out_shape=jax.ShapeDtypeStruct((DEPTH, COND_ROWS, n_out), F32),
        compiler_params=_cparams("arbitrary", "arbitrary"),
        name="adaln",
    )(cond, ada_w, ada_b.reshape(DEPTH, 1, n_out))
    return out.reshape(DEPTH, COND_ROWS, N_MOD, D_MODEL)


def _ctx_block(i):
    return jnp.minimum(i, N_CTX // TM - 1)


def _lat_block(i):
    return jnp.maximum(i - N_CTX // TM, 0)


def _dual_specs(width):
    return [pl.BlockSpec((TM, width), lambda i: (_ctx_block(i), 0)),
            pl.BlockSpec((TM, width), lambda i: (_lat_block(i), 0))]


def _ffn_kernel(sub, dual_in, dual_out, n_act, convert, *refs):
    it = iter(refs)
    take = lambda n: [next(it) for _ in range(n)]
    a_refs = take(2 * n_act)
    w_refs = take(n_act)
    x_refs = take(2 if dual_in else 1)
    mod_ref, g_ref, wg_ref, wu_ref, wd_ref = take(5)
    nxt_refs = take(3) if convert else []
    o_refs = take(2 if dual_out else 1)
    nxt_out_refs = take(3) if convert else []
    h_ref, xs_ref = take(2)
    step = pl.program_id(0)
    is_ctx = step < N_CTX // TM
    g0 = 1 if n_act else 0

    x = jnp.where(is_ctx, x_refs[0][...], x_refs[1][...]) if dual_in else x_refs[0][...]
    if n_act:
        y = None
        for k in range(n_act):
            a = jnp.where(is_ctx, a_refs[2 * k][...], a_refs[2 * k + 1][...])
            d = jnp.dot(a, w_refs[k][...], preferred_element_type=F32)
            y = d if y is None else y + d
        x = x + mod_ref[0, 5:6, :] * _rms(y, g_ref[0:1, :])
    xs_ref[...] = x

    h = _modulated(x, g_ref[g0:g0 + 1, :], mod_ref, sub).astype(BF16)
    for c in range(N_FF_CHUNKS):
        sl = slice(c * FF_CHUNK, (c + 1) * FF_CHUNK)
        gate = jnp.dot(h, wg_ref[:, sl], preferred_element_type=F32)
        up = jnp.dot(h, wu_ref[:, sl], preferred_element_type=F32)
        h_ref[:, sl] = (_silu(gate) * up).astype(BF16)
    y = jnp.dot(h_ref[...], wd_ref[...], preferred_element_type=F32)
    gate_row = mod_ref[0, 3 * sub + 2:3 * sub + 3, :]

    def result():
        return xs_ref[...] + 0.5 * gate_row * _rms(y, g_ref[g0 + 1:g0 + 2, :])

    if dual_out:
        @pl.when(is_ctx)
        def _ctx():
            o_refs[0][...] = result()

        @pl.when(jnp.logical_not(is_ctx))
        def _lat():
            o_refs[1][...] = result()
    else:
        o_refs[0][...] = result()

    if convert:
        @pl.when(step < N_FF_CHUNKS)
        def _next_weights():
            for src_ref, dst_ref in zip(nxt_refs, nxt_out_refs):
                dst_ref[...] = src_ref[...].astype(BF16)


def _ffn(xs, mod_l, g_rows, w16, half, *, acts=(), act_w=(), dual_out=False, next_w=None):
    dual_in = len(xs) == 2
    n_act = len(acts)
    convert = next_w is not None
    sub = 2 * half
    const = lambda i: (0, 0)
    once = dict(pipeline_mode=pl.Buffered(1))
    in_specs, args = [], []
    for a_c, a_l in acts:
        in_specs += _dual_specs(a_c.shape[1])
        args += [a_c, a_l]
    in_specs += [pl.BlockSpec(w.shape, const, **once) for w in act_w]
    args += list(act_w)
    in_specs += _dual_specs(D_MODEL) if dual_in else [pl.BlockSpec((TM, D_MODEL), lambda i: (i, 0))]
    args += list(xs)
    in_specs += [
        pl.BlockSpec((1, N_MOD, D_MODEL), lambda i: (_group_of_block(i, TM), 0, 0)),
        pl.BlockSpec(g_rows.shape, const),
        pl.BlockSpec((D_MODEL, D_FF), const, **once),
        pl.BlockSpec((D_MODEL, D_FF), const, **once),
        pl.BlockSpec((D_FF, D_MODEL), const, **once),
    ]
    args += [mod_l, g_rows, *w16]
    if dual_out:
        out_specs = _dual_specs(D_MODEL)
        out_shape = [jax.ShapeDtypeStruct((N_CTX, D_MODEL), F32), jax.ShapeDtypeStruct((N_LAT, D_MODEL), F32)]
    else:
        out_specs = [pl.BlockSpec((TM, D_MODEL), lambda i: (i, 0))]
        out_shape = [jax.ShapeDtypeStruct((N_TOK, D_MODEL), F32)]
    if convert:
        (wg32, wu32, wd32), ln, hn = next_w
        chunk = lambda i: jnp.minimum(i, N_FF_CHUNKS - 1)
        in_specs += [
            pl.BlockSpec((None, None, D_MODEL, FF_CHUNK), lambda i: (ln, hn, 0, chunk(i))),
            pl.BlockSpec((None, None, D_MODEL, FF_CHUNK), lambda i: (ln, hn, 0, chunk(i))),
            pl.BlockSpec((None, None, FF_CHUNK, D_MODEL), lambda i: (ln, hn, chunk(i), 0)),
        ]
        args += [wg32, wu32, wd32]
        out_specs += [
            pl.BlockSpec((D_MODEL, FF_CHUNK), lambda i: (0, chunk(i))),
            pl.BlockSpec((D_MODEL, FF_CHUNK), lambda i: (0, chunk(i))),
            pl.BlockSpec((FF_CHUNK, D_MODEL), lambda i: (chunk(i), 0)),
        ]
        out_shape += [jax.ShapeDtypeStruct((D_MODEL, D_FF), BF16), jax.ShapeDtypeStruct((D_MODEL, D_FF), BF16),
                      jax.ShapeDtypeStruct((D_FF, D_MODEL), BF16)]
    outs = pl.pallas_call(
        functools.partial(_ffn_kernel, sub, dual_in, dual_out, n_act, convert),
        grid=(N_TOK // TM,),
        in_specs=in_specs,
        out_specs=out_specs,
        out_shape=out_shape,
        scratch_shapes=[pltpu.VMEM((TM, D_FF), BF16), pltpu.VMEM((TM, D_MODEL), F32)],
        compiler_params=_cparams("arbitrary"),
        name="mix_ffn" if n_act else "ffn",
    )(*args)
    n_stream = 2 if dual_out else 1
    stream = tuple(outs[:n_stream])
    return stream, (tuple(outs[n_stream:]) if convert else None)


def _even_in_kernel(x_ref, mod_ref, g_ref, wp_ref, gqa_ref, gkva_ref, wq_ref, cos_ref, sin_ref,
                    q_ref, ckv_ref, kr_ref, mq_ref, mk_ref, mv_ref, mo_ref, mg_ref, ckv_ctx_ref, kr_ctx_ref):
    h = _modulated(x_ref[...], g_ref[...], mod_ref, 1).astype(BF16)
    p = jnp.dot(h, wp_ref[...], preferred_element_type=F32)
    cos = cos_ref[...]
    sin = sin_ref[...]
    half = MLA_ROPE // 4
    qn = _rms(p[:, E_QA:E_QA + MLA_Q_LORA], gqa_ref[...]).astype(BF16)
    q = jnp.dot(qn, wq_ref[...], preferred_element_type=F32)
    for hd in range(MLA_HEADS):
        sl = slice(hd * LANES, (hd + 1) * LANES)
        q_ref[:, sl] = _rope_tile(q[:, sl], cos, sin, half).astype(BF16)
    ckv = _rms(p[:, E_KVA:E_KVA + MLA_KV_LORA], gkva_ref[...])
    ckv_ref[...] = ckv.astype(BF16)
    kr = _rope_tile(p[:, E_KR:E_KR + LANES], cos, sin, half)
    kr_ref[...] = kr.astype(BF16)

    @pl.when(pl.program_id(0) < N_CTX // TM)
    def _ctx_state():
        ckv_ctx_ref[...] = ckv
        kr_ctx_ref[...] = kr[:, KR_LANE:KR_LANE + MLA_ROPE]

    mq_ref[...] = p[:, E_MQ:E_MK].astype(BF16)
    mk_ref[...] = p[:, E_MK:E_MV].astype(BF16)
    mv_ref[...] = p[:, E_MV:E_MO].astype(BF16)
    mo_ref[...] = p[:, E_MO:E_MG]
    mg_ref[...] = p[:, E_MG:E_COLS]


def _even_in(x, mod_l, g_pre, wp, g_qa, g_kva, wq, cos_t, sin_t):
    row = lambda i: (i, 0)
    const = lambda i: (0, 0)
    widths = (MLA_HEADS * LANES, LANES, LANES, M_HEADS * M_DK, M_HEADS * M_DK, M_HEADS * M_DV,
              M_HEADS * M_DV, LANES)
    dtypes = (BF16, BF16, BF16, BF16, BF16, BF16, F32, F32)
    return pl.pallas_call(
        _even_in_kernel,
        grid=(N_TOK // TM,),
        in_specs=[
            pl.BlockSpec((TM, D_MODEL), row),
            pl.BlockSpec((1, N_MOD, D_MODEL), lambda i: (_group_of_block(i, TM), 0, 0)),
            pl.BlockSpec((1, D_MODEL), const),
            pl.BlockSpec((D_MODEL, E_COLS), const),
            pl.BlockSpec((1, MLA_Q_LORA), const),
            pl.BlockSpec((1, MLA_KV_LORA), const),
            pl.BlockSpec((MLA_Q_LORA, MLA_HEADS * LANES), const),
            pl.BlockSpec((TM, LANES), lambda i: (_rope_block(i, TM), 0)),
            pl.BlockSpec((TM, LANES), lambda i: (_rope_block(i, TM), 0)),
        ],
        out_specs=[pl.BlockSpec((TM, w), row) for w in widths]
        + [pl.BlockSpec((TM, MLA_KV_LORA), lambda i: (_ctx_block(i), 0)),
           pl.BlockSpec((TM, MLA_ROPE), lambda i: (_ctx_block(i), 0))],
        out_shape=[jax.ShapeDtypeStruct((N_TOK, w), dt) for w, dt in zip(widths, dtypes)]
        + [jax.ShapeDtypeStruct((N_CTX, MLA_KV_LORA), F32), jax.ShapeDtypeStruct((N_CTX, MLA_ROPE), F32)],
        compiler_params=_cparams("arbitrary"),
        name="even_in",
    )(x, mod_l, g_pre, wp, g_qa, g_kva, wq, cos_t, sin_t)


def _mla_kernel(t_own, n_cache, tq, *refs):
    if n_cache:
        q_ref, ckv_ref, kr_ref, cckv_ref, ckr_ref, wk_ref, wv_ref, o_ref, ks_ref, vs_ref = refs
    else:
        q_ref, ckv_ref, kr_ref, wk_ref, wv_ref, o_ref, ks_ref, vs_ref = refs
    t_keys = n_cache + t_own

    @pl.when(pl.program_id(1) == 0)
    def _expand():
        if n_cache:
            ckv16 = jnp.concatenate([cckv_ref[0].astype(BF16), ckv_ref[...]], axis=0)
            kr = jnp.concatenate([ckr_ref[0].astype(BF16), kr_ref[...]], axis=0)
        else:
            ckv16, kr = ckv_ref[...], kr_ref[...]
        for hd in range(MLA_HEADS):
            ks_ref[hd] = (jnp.dot(ckv16, wk_ref[hd], preferred_element_type=F32) + kr).astype(BF16)
        for pr in range(MLA_HEADS // 2):
            vs_ref[pr] = jnp.dot(ckv16, wv_ref[pr], preferred_element_type=F32).astype(BF16)

    scale = MLA_QK ** -0.5
    lane = lax.broadcasted_iota(jnp.int32, (tq, LANES), 1)
    for pr in range(MLA_HEADS // 2):
        outs = []
        for j in range(2):
            hd = 2 * pr + j
            qh = q_ref[:, hd * LANES:(hd + 1) * LANES]
            s = lax.dot_general(qh, ks_ref[hd], (((1,), (1,)), ((), ())), preferred_element_type=F32) * scale
            m = jnp.max(s, axis=-1, keepdims=True)
            e = jnp.exp(s - m)
            l = jnp.sum(e, axis=-1, keepdims=True)
            outs.append(jnp.dot(e.astype(BF16), vs_ref[pr], preferred_element_type=F32) / l)
        o_ref[:, pr * LANES:(pr + 1) * LANES] = jnp.where(lane < MLA_V, outs[0], outs[1]).astype(BF16)
    del t_keys


def _mla(q, ckv, kr, wk, wv, *, row0, n_batch, t_own, tq, cache=None):
    n_cache = 0 if cache is None else cache[0].shape[1]
    nq = t_own // tq
    qb0 = row0 // tq
    kb0 = row0 // t_own
    in_specs = [
        pl.BlockSpec((tq, MLA_HEADS * LANES), lambda b, j: (qb0 + b * nq + j, 0)),
        pl.BlockSpec((t_own, LANES), lambda b, j: (kb0 + b, 0)),
        pl.BlockSpec((t_own, LANES), lambda b, j: (kb0 + b, 0)),
    ]
    args = [q, ckv, kr]
    if n_cache:
        in_specs += [pl.BlockSpec((1, n_cache, LANES), lambda b, j: (b, 0, 0))] * 2
        args += list(cache)
    in_specs += [
        pl.BlockSpec((MLA_HEADS, LANES, LANES), lambda b, j: (0, 0, 0)),
        pl.BlockSpec((MLA_HEADS // 2, LANES, LANES), lambda b, j: (0, 0, 0)),
    ]
    args += [wk, wv]
    t_keys = n_cache + t_own
    return pl.pallas_call(
        functools.partial(_mla_kernel, t_own, n_cache, tq),
        grid=(n_batch, nq),
        in_specs=in_specs,
        out_specs=pl.BlockSpec((tq, MLA_HEADS * MLA_V), lambda b, j: (b * nq + j, 0)),
        out_shape=jax.ShapeDtypeStruct((n_batch * t_own, MLA_HEADS * MLA_V), BF16),
        scratch_shapes=[pltpu.VMEM((MLA_HEADS, t_keys, LANES), BF16),
                        pltpu.VMEM((MLA_HEADS // 2, t_keys, LANES), BF16)],
        compiler_params=_cparams("arbitrary", "arbitrary"),
        name="mla_lat" if n_cache else "mla_ctx",
    )(*args)


def _log_sigmoid(x):
    return jnp.minimum(x, 0.0) - jnp.log1p(jnp.exp(-jnp.abs(x)))


def _mlstm_chunk(rev, r0, hd, allowed, gates, gates_t, cum, cum_t, mq_ref, mk_ref, mv_ref, ct_ref, n_ref, m_ref,
                 h_ref):
    L = M_CHUNK
    nt = (((1,), (1,)), ((), ()))
    st = (M_HEADS if rev else 0) + hd
    gi = (2 * M_HEADS if rev else 0) + hd
    gf = gi + M_HEADS
    edge = 0 if rev else L - 1
    rows = pl.ds(r0, L)
    q16 = mq_ref[rows, hd * M_DK:(hd + 1) * M_DK]
    k16 = mk_ref[rows, hd * M_DK:(hd + 1) * M_DK] * (M_DK ** -0.5)
    v_t = mv_ref[rows, hd * M_DV:(hd + 1) * M_DV].astype(F32).T
    ct_prev = ct_ref[st, :, 0:M_DK]
    n_prev = n_ref[0, st:st + 1, :]
    m_prev = m_ref[0, st:st + 1, 0:1]

    b_row = cum_t[gf:gf + 1, :]
    li_row = gates_t[gi:gi + 1, :]
    e_col = gates[:, gi:gi + 1] - cum[:, gf:gf + 1]
    dmat = jnp.where(allowed, b_row + e_col, NEG)
    inter = m_prev + b_row
    mt = jnp.maximum(inter, jnp.max(dmat, axis=0, keepdims=True))
    w_inter = jnp.exp(inter - mt)
    qk = lax.dot_general(k16, q16, nt, preferred_element_type=F32)
    a = qk * jnp.exp(dmat - mt)
    n8 = jnp.broadcast_to(n_prev, (8, M_DK)).astype(BF16)
    nq = lax.dot_general(n8, q16, nt, preferred_element_type=F32)[0:1]
    num = w_inter * lax.dot_general(ct_prev.astype(BF16), q16, nt, preferred_element_type=F32) \
        + jnp.dot(v_t.astype(BF16), a.astype(BF16), preferred_element_type=F32)
    den = w_inter * nq + jnp.sum(a, axis=0, keepdims=True)
    h_t = num * (1.0 / jnp.maximum(jnp.abs(den), jnp.exp(-mt)))
    h_ref[rows, hd * M_DV:(hd + 1) * M_DV] = h_t.T

    m_new = mt[:, edge:edge + 1]
    b_last = b_row[:, edge:edge + 1]
    w_s = jnp.exp(b_last - b_row + li_row - m_new)
    w_c = jnp.exp(m_prev + b_last - m_new)
    ct_ref[st, :, 0:M_DK] = w_c * ct_prev + jnp.dot((v_t * w_s).astype(BF16), k16, preferred_element_type=F32)
    w8 = jnp.broadcast_to(w_s, (8, L)).astype(BF16)
    n_ref[0, st:st + 1, :] = w_c * n_prev + jnp.dot(w8, k16, preferred_element_type=F32)[0:1]
    m_ref[0, st:st + 1, :] = jnp.broadcast_to(m_new, (1, LANES))


def _mlstm_kernel(t_len, has_state, *refs):
    n_st = 2 * M_HEADS
    if has_state:
        (mq_ref, mk_ref, mv_ref, mo_ref, mg_ref, gb_ref, gout_ref, c0_ref, n0_ref, m0_ref,
         out_ref, c_ref, n_ref, m_ref, hf_ref, hb_ref, ct_ref) = refs
        pad = jnp.zeros((LANES - M_DK, M_DV), F32)
        for st in range(n_st):
            ct_ref[st] = jnp.concatenate([c0_ref[0, st], pad], axis=0).T
        n_ref[...] = n0_ref[...]
        m_ref[...] = m0_ref[...]
    else:
        (mq_ref, mk_ref, mv_ref, mo_ref, mg_ref, gb_ref, gout_ref,
         out_ref, c_ref, n_ref, m_ref, hf_ref, hb_ref, ct_ref) = refs
        ct_ref[...] = jnp.zeros_like(ct_ref)
        n_ref[...] = jnp.zeros_like(n_ref)
        m_ref[...] = jnp.zeros_like(m_ref)
    L = M_CHUNK
    nc = t_len // L
    r_idx = lax.broadcasted_iota(jnp.int32, (L, L), 0)
    c_idx = lax.broadcasted_iota(jnp.int32, (L, L), 1)
    lower = c_idx <= r_idx
    upper = c_idx >= r_idx
    tril = lower.astype(F32)
    triu = upper.astype(F32)
    lane = lax.broadcasted_iota(jnp.int32, (L, LANES), 1)
    is_forget = (lane % (2 * M_HEADS)) >= M_HEADS
    hi = lax.Precision.HIGHEST

    def gate_sums(r0, rev):
        g = mg_ref[pl.ds(r0, L), :] + gb_ref[...]
        g = jnp.where(is_forget, _log_sigmoid(g), g)
        g_t = g.T
        if rev:
            cum = jnp.dot(triu, g, precision=hi, preferred_element_type=F32)
            cum_t = jnp.dot(g_t, tril, precision=hi, preferred_element_type=F32)
        else:
            cum = jnp.dot(tril, g, precision=hi, preferred_element_type=F32)
            cum_t = jnp.dot(g_t, triu, precision=hi, preferred_element_type=F32)
        return g, g_t, cum, cum_t

    def step(c, carry):
        for rev in (False, True):
            r0 = pl.multiple_of((nc - 1 - c if rev else c) * L, L)
            g, g_t, cum, cum_t = gate_sums(r0, rev)
            allowed = lower if rev else upper
            for hd in range(M_HEADS):
                _mlstm_chunk(rev, r0, hd, allowed, g, g_t, cum, cum_t, mq_ref, mk_ref, mv_ref,
                             ct_ref, n_ref, m_ref, hb_ref if rev else hf_ref)
        return carry

    lax.fori_loop(0, nc, step, 0)
    for st in range(n_st):
        c_ref[0, st] = ct_ref[st].T[0:M_DK, :]
    for hd in range(M_HEADS):
        sl = slice(hd * M_DV, (hd + 1) * M_DV)
        hm = hf_ref[:, sl] + hb_ref[:, sl]
        hm = hm * lax.rsqrt(jnp.mean(hm * hm, axis=-1, keepdims=True) + EPS) * gout_ref[:, sl]
        out_ref[:, sl] = (jax.nn.sigmoid(mo_ref[:, sl]) * hm).astype(BF16)


def _mlstm(mq, mk, mv, mo, mg, gate_b, g_out, *, row0, n_batch, t_len, state=None):
    kb0 = row0 // t_len
    row = lambda b: (kb0 + b, 0)
    const = lambda b: (0, 0)
    st4 = lambda b: (b, 0, 0, 0)
    st3 = lambda b: (b, 0, 0)
    n_st = 2 * M_HEADS
    in_specs = [
        pl.BlockSpec((t_len, M_HEADS * M_DK), row),
        pl.BlockSpec((t_len, M_HEADS * M_DK), row),
        pl.BlockSpec((t_len, M_HEADS * M_DV), row),
        pl.BlockSpec((t_len, M_HEADS * M_DV), row),
        pl.BlockSpec((t_len, LANES), row),
        pl.BlockSpec((1, LANES), const),
        pl.BlockSpec((1, M_HEADS * M_DV), const),
    ]
    args = [mq, mk, mv, mo, mg, gate_b, g_out]
    if state is not None:
        in_specs += [pl.BlockSpec((1, n_st, M_DK, M_DV), st4), pl.BlockSpec((1, n_st, M_DK), st3),
                     pl.BlockSpec((1, n_st, LANES), st3)]
        args += list(state)
    return pl.pallas_call(
        functools.partial(_mlstm_kernel, t_len, state is not None),
        grid=(n_batch,),
        in_specs=in_specs,
        out_specs=[pl.BlockSpec((t_len, M_HEADS * M_DV), lambda b: (b, 0)),
                   pl.BlockSpec((1, n_st, M_DK, M_DV), st4), pl.BlockSpec((1, n_st, M_DK), st3),
                   pl.BlockSpec((1, n_st, LANES), st3)],
        out_shape=[jax.ShapeDtypeStruct((n_batch * t_len, M_HEADS * M_DV), BF16),
                   jax.ShapeDtypeStruct((n_batch, n_st, M_DK, M_DV), F32),
                   jax.ShapeDtypeStruct((n_batch, n_st, M_DK), F32),
                   jax.ShapeDtypeStruct((n_batch, n_st, LANES), F32)],
        scratch_shapes=[pltpu.VMEM((t_len, M_HEADS * M_DV), F32), pltpu.VMEM((t_len, M_HEADS * M_DV), F32),
                        pltpu.VMEM((n_st, M_DV, LANES), F32)],
        compiler_params=_cparams("arbitrary"),
        name="mlstm_lat" if state is not None else "mlstm_ctx",
    )(*args)


def _odd_in_kernel(x_ref, mod_ref, g_ref, wp_ref, cos_ref, sin_ref, q_ref, k_ref, v_ref, k_ctx_ref, v_ctx_ref):
    h = _modulated(x_ref[...], g_ref[...], mod_ref, 1).astype(BF16)
    p = jnp.dot(h, wp_ref[...], preferred_element_type=F32)
    cos = cos_ref[...]
    sin = sin_ref[...]
    half = S_HEAD_DIM // 4
    n_q = S_HEADS * S_HEAD_DIM
    n_k = S_KV_HEADS * LANES
    for t in range(n_q // LANES):
        sl = slice(t * LANES, (t + 1) * LANES)
        q_ref[:, sl] = _rope_tile(p[:, sl], cos, sin, half).astype(BF16)
    k_tiles = [_rope_tile(p[:, n_q + g * LANES:n_q + (g + 1) * LANES], cos, sin, half) for g in range(S_KV_HEADS)]
    v_tiles = [p[:, n_q + n_k + g * LANES:n_q + n_k + (g + 1) * LANES] for g in range(S_KV_HEADS)]
    for g in range(S_KV_HEADS):
        k_ref[:, g * LANES:(g + 1) * LANES] = k_tiles[g].astype(BF16)
        v_ref[:, g * LANES:(g + 1) * LANES] = v_tiles[g].astype(BF16)

    @pl.when(pl.program_id(0) < N_CTX // TM)
    def _ctx_state():
        for bb in range(TM // SEQ):
            rows = slice(bb * SEQ, (bb + 1) * SEQ)
            for g in range(S_KV_HEADS):
                k_ctx_ref[bb, g] = k_tiles[g][rows, 0:S_HEAD_DIM]
                v_ctx_ref[bb, g] = v_tiles[g][rows, 0:S_HEAD_DIM]


def _odd_in(x, mod_l, g_pre, wp, cos_t, sin_t):
    row = lambda i: (i, 0)
    const = lambda i: (0, 0)
    widths = (S_HEADS * S_HEAD_DIM, S_KV_HEADS * LANES, S_KV_HEADS * LANES)
    return pl.pallas_call(
        _odd_in_kernel,
        grid=(N_TOK // TM,),
        in_specs=[
            pl.BlockSpec((TM, D_MODEL), row),
            pl.BlockSpec((1, N_MOD, D_MODEL), lambda i: (_group_of_block(i, TM), 0, 0)),
            pl.BlockSpec((1, D_MODEL), const),
            pl.BlockSpec((D_MODEL, sum(widths)), const),
            pl.BlockSpec((TM, LANES), lambda i: (_rope_block(i, TM), 0)),
            pl.BlockSpec((TM, LANES), lambda i: (_rope_block(i, TM), 0)),
        ],
        out_specs=[pl.BlockSpec((TM, w), row) for w in widths]
        + [pl.BlockSpec((TM // SEQ, S_KV_HEADS, SEQ, S_HEAD_DIM), lambda i: (_ctx_block(i), 0, 0, 0))] * 2,
        out_shape=[jax.ShapeDtypeStruct((N_TOK, w), BF16) for w in widths]
        + [jax.ShapeDtypeStruct((BATCH, S_KV_HEADS, SEQ, S_HEAD_DIM), F32)] * 2,
        compiler_params=_cparams("arbitrary"),
        name="odd_in",
    )(x, mod_l, g_pre, wp, cos_t, sin_t)


def _gqa_heads(q_ref, keys, vals, sink_ref, mask, o_ref, tq):
    scale = S_HEAD_DIM ** -0.5
    lane = lax.broadcasted_iota(jnp.int32, (tq, LANES), 1)
    low = lane < S_HEAD_DIM
    for g in range(S_KV_HEADS):
        k2, v2 = keys(g), vals(g)
        col = g * S_GROUP * S_HEAD_DIM
        tiles = [q_ref[:, col:col + LANES].astype(F32) * scale, q_ref[:, col + LANES:col + 2 * LANES].astype(F32) * scale]
        q4 = jnp.concatenate([jnp.where(low, tiles[0], 0.0), jnp.where(low, 0.0, tiles[0]),
                              jnp.where(low, tiles[1], 0.0), jnp.where(low, 0.0, tiles[1])], axis=0).astype(BF16)
        s = lax.dot_general(k2, q4, (((1,), (1,)), ((), ())), preferred_element_type=F32)
        if mask is not None:
            s = jnp.where(mask, s, NEG)
        sk = jnp.concatenate([jnp.full((1, tq), sink_ref[g * S_GROUP + j], F32) for j in range(S_GROUP)], axis=1)
        m = jnp.maximum(jnp.max(s, axis=0, keepdims=True), sk)
        e = jnp.exp(s - m)
        l = jnp.sum(e, axis=0, keepdims=True) + jnp.exp(sk - m)
        o_t = lax.dot_general(v2, e.astype(BF16), (((0,), (0,)), ((), ())), preferred_element_type=F32)
        o4 = (o_t * (1.0 / l)).T
        o_ref[:, col:col + LANES] = jnp.where(low, o4[0:tq], o4[tq:2 * tq]).astype(BF16)
        o_ref[:, col + LANES:col + 2 * LANES] = jnp.where(low, o4[2 * tq:3 * tq], o4[3 * tq:4 * tq]).astype(BF16)


def _gqa_ctx_kernel(sink_ref, q_ref, k_ref, v_ref, o_ref):
    keys = lambda g: k_ref[:, g * LANES:(g + 1) * LANES]
    vals = lambda g: v_ref[:, g * LANES:(g + 1) * LANES]
    _gqa_heads(q_ref, keys, vals, sink_ref, None, o_ref, SEQ)


def _gqa_ctx(q, kd, vd, sink):
    row = lambda b: (b, 0)
    return pl.pallas_call(
        _gqa_ctx_kernel,
        grid=(BATCH,),
        in_specs=[
            pl.BlockSpec(memory_space=pltpu.SMEM),
            pl.BlockSpec((SEQ, S_HEADS * S_HEAD_DIM), row),
            pl.BlockSpec((SEQ, S_KV_HEADS * LANES), row),
            pl.BlockSpec((SEQ, S_KV_HEADS * LANES), row),
        ],
        out_specs=pl.BlockSpec((SEQ, S_HEADS * S_HEAD_DIM), row),
        out_shape=jax.ShapeDtypeStruct((N_CTX, S_HEADS * S_HEAD_DIM), BF16),
        compiler_params=_cparams("arbitrary"),
        name="gqa_ctx",
    )(sink, q, kd, vd)


def _gqa_lat_kernel(sink_ref, q_ref, k_ref, v_ref, ck_ref, cv_ref, o_ref):
    n = pl.program_id(1)
    nb = DEC_SEQ // S_BLOCK
    B = S_BLOCK
    prev0 = pl.multiple_of(jnp.maximum(n - 1, 0) * B, B)
    cur0 = pl.multiple_of(n * B, B)
    next0 = pl.multiple_of(jnp.minimum(n + 1, nb - 1) * B, B)
    t_idx = lax.broadcasted_iota(jnp.int32, (PAST_LEN + 3 * B, S_GROUP * B), 1) % B
    c_idx = lax.broadcasted_iota(jnp.int32, (PAST_LEN + 3 * B, S_GROUP * B), 0)
    s_prev = c_idx - PAST_LEN
    s_next = c_idx - (PAST_LEN + 2 * B)
    in_prev = (c_idx >= PAST_LEN) & (c_idx < PAST_LEN + B)
    in_next = c_idx >= PAST_LEN + 2 * B
    far = jnp.int32(4 * B)
    bad_prev = in_prev & (s_prev < t_idx + jnp.where(n == 0, far, 0))
    bad_next = in_next & (s_next + jnp.where(n == nb - 1, far, 0) > t_idx)
    mask = jnp.logical_not(bad_prev | bad_next)

    def gather(ref, cache_ref, g):
        sl = slice(g * LANES, (g + 1) * LANES)
        return jnp.concatenate([cache_ref[0, g], ref[pl.ds(prev0, B), sl], ref[pl.ds(cur0, B), sl],
                                ref[pl.ds(next0, B), sl]], axis=0)

    keys = lambda g: gather(k_ref, ck_ref, g)
    vals = lambda g: gather(v_ref, cv_ref, g)
    _gqa_heads(q_ref, keys, vals, sink_ref, mask, o_ref, B)


def _gqa_lat(q, kd, vd, cache_k2, cache_v2, sink):
    nb = DEC_SEQ // S_BLOCK
    qb0 = N_CTX // S_BLOCK
    kb0 = N_CTX // DEC_SEQ
    return pl.pallas_call(
        _gqa_lat_kernel,
        grid=(DEC_BATCH, nb),
        in_specs=[
            pl.BlockSpec(memory_space=pltpu.SMEM),
            pl.BlockSpec((S_BLOCK, S_HEADS * S_HEAD_DIM), lambda b, n: (qb0 + b * nb + n, 0)),
            pl.BlockSpec((DEC_SEQ, S_KV_HEADS * LANES), lambda b, n: (kb0 + b, 0)),
            pl.BlockSpec((DEC_SEQ, S_KV_HEADS * LANES), lambda b, n: (kb0 + b, 0)),
            pl.BlockSpec((1, S_KV_HEADS, PAST_LEN, LANES), lambda b, n: (b, 0, 0, 0)),
            pl.BlockSpec((1, S_KV_HEADS, PAST_LEN, LANES), lambda b, n: (b, 0, 0, 0)),
        ],
        out_specs=pl.BlockSpec((S_BLOCK, S_HEADS * S_HEAD_DIM), lambda b, n: (b * nb + n, 0)),
        out_shape=jax.ShapeDtypeStruct((N_LAT, S_HEADS * S_HEAD_DIM), BF16),
        compiler_params=_cparams("arbitrary", "arbitrary"),
        name="gqa_lat",
    )(sink, q, kd, vd, cache_k2, cache_v2)


def _rope_tables(rot_dim, lane_off, reps):
    nf = rot_dim // 4
    inv = np.float32(ROPE_BASE) ** (-np.arange(nf, dtype=np.float32) / np.float32(nf))
    pos = np.arange(DEC_SEQ)
    ang_r = (pos // GRID_W).astype(np.float32)[:, None] * inv
    ang_c = (pos % GRID_W).astype(np.float32)[:, None] * inv
    cos_g = np.concatenate([np.cos(ang_r), np.cos(ang_r), np.cos(ang_c), np.cos(ang_c)], axis=1)
    sin_g = np.concatenate([-np.sin(ang_r), np.sin(ang_r), -np.sin(ang_c), np.sin(ang_c)], axis=1)
    cos_t = np.ones((TM + DEC_SEQ, LANES), np.float32)
    sin_t = np.zeros((TM + DEC_SEQ, LANES), np.float32)
    for r in range(reps):
        lo = lane_off + r * rot_dim
        cos_t[TM:, lo:lo + rot_dim] = cos_g
        sin_t[TM:, lo:lo + rot_dim] = sin_g
    return jnp.asarray(cos_t, F32), jnp.asarray(sin_t, F32)


def _even_weights(w_in, w_qb, w_kvb, gate_b):
    z = lambda n: jnp.zeros((D_MODEL, n), F32)
    idx = np.cumsum([MLA_Q_LORA, MLA_KV_LORA, MLA_ROPE, M_HEADS * M_DK, M_HEADS * M_DK, M_HEADS * M_DV,
                     M_HEADS * M_DV])
    q_a, kv_a, k_rope, mq, mk, mv, mo, mg = jnp.split(w_in, idx, axis=1)
    wp = jnp.concatenate([q_a, kv_a, z(KR_LANE), k_rope, z(LANES - KR_LANE - MLA_ROPE), mq, mk, mv, mo, mg,
                          z(LANES - 4 * M_HEADS)], axis=1).astype(BF16)
    wq = jnp.pad(w_qb.reshape(MLA_Q_LORA, MLA_HEADS, MLA_QK), ((0, 0), (0, 0), (0, LANES - MLA_QK)))
    wq = wq.reshape(MLA_Q_LORA, MLA_HEADS * LANES).astype(BF16)
    kvb = w_kvb.reshape(MLA_KV_LORA, MLA_HEADS, MLA_NOPE + MLA_V)
    wk = jnp.pad(kvb[:, :, :MLA_NOPE], ((0, 0), (0, 0), (0, LANES - MLA_NOPE))).transpose(1, 0, 2).astype(BF16)
    wv = kvb[:, :, MLA_NOPE:].reshape(MLA_KV_LORA, MLA_HEADS // 2, 2 * MLA_V).transpose(1, 0, 2).astype(BF16)
    gb = jnp.pad(gate_b, (0, LANES - 4 * M_HEADS)).reshape(1, LANES)
    return wp, wq, wk, wv, gb


def _dup_heads(w):
    w3 = w.reshape(D_MODEL, S_KV_HEADS, S_HEAD_DIM)
    return jnp.concatenate([w3, w3], axis=-1).reshape(D_MODEL, S_KV_HEADS * LANES)


def _odd_weights(w_in):
    n_q = S_HEADS * S_HEAD_DIM
    n_kv = S_KV_HEADS * S_HEAD_DIM
    return jnp.concatenate([w_in[:, :n_q], _dup_heads(w_in[:, n_q:n_q + n_kv]), _dup_heads(w_in[:, n_q + n_kv:])],
                           axis=1).astype(BF16)


def kernel(x_prompt, x_sample, cache_mla_ckv, cache_mla_krope, state_mlstm_C, state_mlstm_n, state_mlstm_m,
           cache_swa_k, cache_swa_v, c, c_ctx, ada_w, ada_b, norm_g, ffn_w_gate, ffn_w_up, ffn_w_down,
           even_w_in, mla_g_qa, mla_g_kva, mla_w_qb, mla_w_kvb, mlstm_gate_b, mlstm_g_out, even_w_out,
           odd_w_in, swa_sink, odd_w_out):
    xs = (x_prompt.reshape(N_CTX, D_MODEL), x_sample.reshape(N_LAT, D_MODEL))
    cond = jnp.concatenate([c_ctx[None], c, jnp.zeros((COND_ROWS - N_GROUPS, D_MODEL), F32)], axis=0)
    mod = _adaln(cond, ada_w, ada_b)
    cos_e, sin_e = _rope_tables(MLA_ROPE, KR_LANE, 1)
    cos_o, sin_o = _rope_tables(S_HEAD_DIM, 0, LANES // S_HEAD_DIM)
    w32 = (ffn_w_gate, ffn_w_up, ffn_w_down)
    w16 = tuple(w[0, 0].astype(BF16) for w in w32)

    ckv_l, kr_l, mc_l, mn_l, mm_l, sk_l, sv_l = [], [], [], [], [], [], []
    for l in range(DEPTH):
        mod_l = mod[l]
        g = norm_g[l]
        i = l // 2
        (x,), w16 = _ffn(xs, mod_l, g[0:2], w16, 0, next_w=(w32, l, 1))
        if l % 2 == 0:
            wp, wq, wk, wv, gb = _even_weights(even_w_in[i], mla_w_qb[i], mla_w_kvb[i], mlstm_gate_b[i])
            q, ckv, kr, mq, mk, mv, mo, mg, ckv_ctx, kr_ctx = _even_in(
                x, mod_l, g[2:3], wp, mla_g_qa[i].reshape(1, -1), mla_g_kva[i].reshape(1, -1), wq, cos_e, sin_e)
            cache_kr = jnp.pad(cache_mla_krope[:, i], ((0, 0), (0, 0), (KR_LANE, LANES - KR_LANE - MLA_ROPE)))
            att_c = _mla(q, ckv, kr, wk, wv, row0=0, n_batch=BATCH, t_own=SEQ, tq=SEQ)
            att_l = _mla(q, ckv, kr, wk, wv, row0=N_CTX, n_batch=DEC_BATCH, t_own=DEC_SEQ, tq=512,
                         cache=(cache_mla_ckv[:, i], cache_kr))
            g_out = mlstm_g_out[i].reshape(1, -1)
            mo_c, s_c, s_n, s_m = _mlstm(mq, mk, mv, mo, mg, gb, g_out, row0=0, n_batch=BATCH, t_len=SEQ)
            n_st = 2 * M_HEADS
            state = (state_mlstm_C[:, i].reshape(DEC_BATCH, n_st, M_DK, M_DV),
                     state_mlstm_n[:, i].reshape(DEC_BATCH, n_st, M_DK),
                     jnp.broadcast_to(state_mlstm_m[:, i].reshape(DEC_BATCH, n_st, 1), (DEC_BATCH, n_st, LANES)))
            mo_l, _, _, _ = _mlstm(mq, mk, mv, mo, mg, gb, g_out, row0=N_CTX, n_batch=DEC_BATCH, t_len=DEC_SEQ,
                                   state=state)
            w_out = even_w_out[i].astype(BF16)
            n_att = MLA_HEADS * MLA_V
            acts, act_w = [(att_c, att_l), (mo_c, mo_l)], [w_out[:n_att], w_out[n_att:]]
            ckv_l.append(ckv_ctx.reshape(BATCH, SEQ, MLA_KV_LORA))
            kr_l.append(kr_ctx.reshape(BATCH, SEQ, MLA_ROPE))
            mc_l.append(s_c.reshape(BATCH, 2, M_HEADS, M_DK, M_DV))
            mn_l.append(s_n.reshape(BATCH, 2, M_HEADS, M_DK))
            mm_l.append(s_m[:, :, 0].reshape(BATCH, 2, M_HEADS))
        else:
            q, kd, vd, k_ctx, v_ctx = _odd_in(x, mod_l, g[2:3], _odd_weights(odd_w_in[i]), cos_o, sin_o)
            dup = lambda a: jnp.concatenate([a, a], axis=-1).astype(BF16)
            o_c = _gqa_ctx(q, kd, vd, swa_sink[i])
            o_l = _gqa_lat(q, kd, vd, dup(cache_swa_k[:, i]), dup(cache_swa_v[:, i]), swa_sink[i])
            acts, act_w = [(o_c, o_l)], [odd_w_out[i].astype(BF16)]
            sk_l.append(k_ctx)
            sv_l.append(v_ctx)
        last = l == DEPTH - 1
        xs, w16 = _ffn((x,), mod_l, g[3:6], w16, 1, acts=acts, act_w=act_w, dual_out=last,
                       next_w=None if last else (w32, l + 1, 0))

    return (xs[0].reshape(BATCH, SEQ, D_MODEL), xs[1].reshape(DEC_BATCH, DEC_SEQ, D_MODEL),
            jnp.stack(ckv_l, 1), jnp.stack(kr_l, 1), jnp.stack(mc_l, 1), jnp.stack(mn_l, 1), jnp.stack(mm_l, 1),
            jnp.stack(sk_l, 1), jnp.stack(sv_l, 1))
```

```python
import functools

import jax
import jax.numpy as jnp
import numpy as np
from jax import lax
from jax.experimental import pallas as pl
from jax.experimental.pallas import tpu as pltpu

F32 = jnp.float32
BF16 = jnp.bfloat16

D_MODEL = 1024
BATCH = 32
SEQ = 256
DEPTH = 4
DEC_BATCH = 4
DEC_SEQ = 1024
PAST_LEN = 512
GRID_W = 64
N_MOD = 9
D_FF = 2816
EPS = 1e-6
ROPE_BASE = 10000.0
NEG = -1e30
MLA_HEADS = 8
MLA_NOPE = 64
MLA_ROPE = 32
MLA_QK = MLA_NOPE + MLA_ROPE
MLA_V = 64
MLA_Q_LORA = 256
MLA_KV_LORA = 128
M_HEADS = 4
M_DK = 64
M_DV = 128
M_CHUNK = 128
S_HEADS = 16
S_KV_HEADS = 4
S_GROUP = S_HEADS // S_KV_HEADS
S_HEAD_DIM = 64
S_WINDOW = 128
S_BLOCK = 128

N_CTX = BATCH * SEQ
N_LAT = DEC_BATCH * DEC_SEQ
N_TOK = N_CTX + N_LAT
N_GROUPS = 1 + DEC_BATCH
COND_ROWS = 8

LANES = 128
FF_CHUNK = 256
N_FF_CHUNKS = D_FF // FF_CHUNK
assert N_FF_CHUNKS * FF_CHUNK == D_FF

TM = 512
ADA_TN = 1536
M_SEQ_PER_STEP = 2
VMEM_LIMIT = 52 * 1024 * 1024

E_QA, E_KVA, E_KR, E_MQ, E_MK, E_MV, E_MO, E_MG, E_COLS = 0, 256, 384, 512, 768, 1024, 1536, 2048, 2176
KR_LANE = MLA_NOPE


def _cparams(*sem):
    return pltpu.CompilerParams(dimension_semantics=sem, vmem_limit_bytes=VMEM_LIMIT)


def _group_of_block(i, tm):
    n_ctx = N_CTX // tm
    per_lat = DEC_SEQ // tm
    return jnp.where(i < n_ctx, 0, 1 + (i - n_ctx) // per_lat)


def _rope_block(i, tm):
    n_ctx = N_CTX // tm
    per_lat = DEC_SEQ // tm
    return jnp.where(i < n_ctx, 0, 1 + (i - n_ctx) % per_lat)


def _rms(x, g):
    return x * lax.rsqrt(jnp.mean(x * x, axis=-1, keepdims=True) + EPS) * g


def _modulated(x, g_pre, mod_ref, sub):
    shift = mod_ref[0, 3 * sub:3 * sub + 1, :]
    scale = mod_ref[0, 3 * sub + 1:3 * sub + 2, :]
    return _rms(x, g_pre) * (1.0 + scale) + shift


def _silu(x):
    return x * jax.nn.sigmoid(x)


def _swap_halves(x, half):
    lane = lax.broadcasted_iota(jnp.int32, x.shape, 1)
    up = pltpu.roll(x, LANES - half, 1)
    down = pltpu.roll(x, half, 1)
    return jnp.where(lane % (2 * half) < half, up, down)


def _rope_tile(x, cos, sin, half):
    return x * cos + _swap_halves(x, half) * sin


def _run_staged(stages):
    stages = list(stages)
    while stages:
        stages = [s for s in stages if next(s, True) is None]


def _adaln_kernel(cond_ref, w_ref, b_ref, o_ref):
    s = _silu(cond_ref[...]).astype(BF16)
    o_ref[0] = jnp.dot(s, w_ref[0].astype(BF16), preferred_element_type=F32) + b_ref[0]


def _adaln(cond, ada_w, ada_b):
    n_out = N_MOD * D_MODEL
    out = pl.pallas_call(
        _adaln_kernel,
        grid=(DEPTH, n_out // ADA_TN),
        in_specs=[
            pl.BlockSpec((COND_ROWS, D_MODEL), lambda l, j: (0, 0)),
            pl.BlockSpec((1, D_MODEL, ADA_TN), lambda l, j: (l, 0, j)),
            pl.BlockSpec((1, 1, ADA_TN), lambda l, j: (l, 0, j)),
        ],
        out_specs=pl.BlockSpec((1, COND_ROWS, ADA_TN), lambda l, j: (l, 0, j)),
        out_shape=jax.ShapeDtypeStruct((DEPTH, COND_ROWS, n_out), F32),
        compiler_params=_cparams("arbitrary", "arbitrary"),
        name="adaln",
    )(cond, ada_w, ada_b.reshape(DEPTH, 1, n_out))
    return out.reshape(DEPTH, COND_ROWS, N_MOD, D_MODEL)


def _ctx_block(i):
    return jnp.minimum(i, N_CTX // TM - 1)


def _lat_block(i):
    return jnp.maximum(i - N_CTX // TM, 0)


def _dual_specs(width):
    return [pl.BlockSpec((TM, width), lambda i: (_ctx_block(i), 0)),
            pl.BlockSpec((TM, width), lambda i: (_lat_block(i), 0))]


def _ffn_kernel(sub, dual_in, dual_out, n_act, convert, *refs):
    it = iter(refs)
    take = lambda n: [next(it) for _ in range(n)]
    a_refs = take(2 * n_act)
    w_refs = take(n_act)
    x_refs = take(2 if dual_in else 1)
    mod_ref, g_ref, wg_ref, wu_ref, wd_ref = take(5)
    nxt_refs = take(3) if convert else []
    o_refs = take(2 if dual_out else 1)
    nxt_out_refs = take(3) if convert else []
    h_ref, xs_ref = take(2)
    step = pl.program_id(0)
    is_ctx = step < N_CTX // TM
    g0 = 1 if n_act else 0

    x = jnp.where(is_ctx, x_refs[0][...], x_refs[1][...]) if dual_in else x_refs[0][...]
    if n_act:
        y = None
        for k in range(n_act):
            a = jnp.where(is_ctx, a_refs[2 * k][...], a_refs[2 * k + 1][...])
            d = jnp.dot(a, w_refs[k][...], preferred_element_type=F32)
            y = d if y is None else y + d
        x = x + mod_ref[0, 5:6, :] * _rms(y, g_ref[0:1, :])
    xs_ref[...] = x

    h = _modulated(x, g_ref[g0:g0 + 1, :], mod_ref, sub).astype(BF16)
    for c in range(N_FF_CHUNKS):
        sl = slice(c * FF_CHUNK, (c + 1) * FF_CHUNK)
        gate = jnp.dot(h, wg_ref[:, sl], preferred_element_type=F32)
        up = jnp.dot(h, wu_ref[:, sl], preferred_element_type=F32)
        h_ref[:, sl] = (_silu(gate) * up).astype(BF16)
    y = jnp.dot(h_ref[...], wd_ref[...], preferred_element_type=F32)
    gate_row = mod_ref[0, 3 * sub + 2:3 * sub + 3, :]

    def result():
        return xs_ref[...] + 0.5 * gate_row * _rms(y, g_ref[g0 + 1:g0 + 2, :])

    if dual_out:
        @pl.when(is_ctx)
        def _ctx():
            o_refs[0][...] = result()

        @pl.when(jnp.logical_not(is_ctx))
        def _lat():
            o_refs[1][...] = result()
    else:
        o_refs[0][...] = result()

    if convert:
        @pl.when(step < N_FF_CHUNKS)
        def _next_weights():
            for src_ref, dst_ref in zip(nxt_refs, nxt_out_refs):
                dst_ref[...] = src_ref[...].astype(BF16)


def _ffn(xs, mod_l, g_rows, w16, half, *, acts=(), act_w=(), dual_out=False, next_w=None):
    dual_in = len(xs) == 2
    n_act = len(acts)
    convert = next_w is not None
    sub = 2 * half
    const = lambda i: (0, 0)
    once = dict(pipeline_mode=pl.Buffered(1))
    in_specs, args = [], []
    for a_c, a_l in acts:
        in_specs += _dual_specs(a_c.shape[1])
        args += [a_c, a_l]
    in_specs += [pl.BlockSpec(w.shape, const, **once) for w in act_w]
    args += list(act_w)
    in_specs += _dual_specs(D_MODEL) if dual_in else [pl.BlockSpec((TM, D_MODEL), lambda i: (i, 0))]
    args += list(xs)
    in_specs += [
        pl.BlockSpec((1, N_MOD, D_MODEL), lambda i: (_group_of_block(i, TM), 0, 0)),
        pl.BlockSpec(g_rows.shape, const),
        pl.BlockSpec((D_MODEL, D_FF), const, **once),
        pl.BlockSpec((D_MODEL, D_FF), const, **once),
        pl.BlockSpec((D_FF, D_MODEL), const, **once),
    ]
    args += [mod_l, g_rows, *w16]
    if dual_out:
        out_specs = _dual_specs(D_MODEL)
        out_shape = [jax.ShapeDtypeStruct((N_CTX, D_MODEL), F32), jax.ShapeDtypeStruct((N_LAT, D_MODEL), F32)]
    else:
        out_specs = [pl.BlockSpec((TM, D_MODEL), lambda i: (i, 0))]
        out_shape = [jax.ShapeDtypeStruct((N_TOK, D_MODEL), F32)]
    if convert:
        (wg32, wu32, wd32), ln, hn = next_w
        chunk = lambda i: jnp.minimum(i, N_FF_CHUNKS - 1)
        in_specs += [
            pl.BlockSpec((None, None, D_MODEL, FF_CHUNK), lambda i: (ln, hn, 0, chunk(i))),
            pl.BlockSpec((None, None, D_MODEL, FF_CHUNK), lambda i: (ln, hn, 0, chunk(i))),
            pl.BlockSpec((None, None, FF_CHUNK, D_MODEL), lambda i: (ln, hn, chunk(i), 0)),
        ]
        args += [wg32, wu32, wd32]
        out_specs += [
            pl.BlockSpec((D_MODEL, FF_CHUNK), lambda i: (0, chunk(i))),
            pl.BlockSpec((D_MODEL, FF_CHUNK), lambda i: (0, chunk(i))),
            pl.BlockSpec((FF_CHUNK, D_MODEL), lambda i: (chunk(i), 0)),
        ]
        out_shape += [jax.ShapeDtypeStruct((D_MODEL, D_FF), BF16), jax.ShapeDtypeStruct((D_MODEL, D_FF), BF16),
                      jax.ShapeDtypeStruct((D_FF, D_MODEL), BF16)]
    outs = pl.pallas_call(
        functools.partial(_ffn_kernel, sub, dual_in, dual_out, n_act, convert),
        grid=(N_TOK // TM,),
        in_specs=in_specs,
        out_specs=out_specs,
        out_shape=out_shape,
        scratch_shapes=[pltpu.VMEM((TM, D_FF), BF16), pltpu.VMEM((TM, D_MODEL), F32)],
        compiler_params=_cparams("arbitrary"),
        name="mix_ffn" if n_act else "ffn",
    )(*args)
    n_stream = 2 if dual_out else 1
    stream = tuple(outs[:n_stream])
    return stream, (tuple(outs[n_stream:]) if convert else None)


def _even_in_kernel(x_ref, mod_ref, g_ref, wp_ref, gqa_ref, gkva_ref, wq_ref, cos_ref, sin_ref,
                    q_ref, ckv_ref, kr_ref, mq_ref, mk_ref, mv_ref, mo_ref, mg_ref, ckv_ctx_ref, kr_ctx_ref):
    h = _modulated(x_ref[...], g_ref[...], mod_ref, 1).astype(BF16)
    p = jnp.dot(h, wp_ref[...], preferred_element_type=F32)
    cos = cos_ref[...]
    sin = sin_ref[...]
    half = MLA_ROPE // 4
    qn = _rms(p[:, E_QA:E_QA + MLA_Q_LORA], gqa_ref[...]).astype(BF16)
    q = jnp.dot(qn, wq_ref[...], preferred_element_type=F32)
    for hd in range(MLA_HEADS):
        sl = slice(hd * LANES, (hd + 1) * LANES)
        q_ref[:, sl] = _rope_tile(q[:, sl], cos, sin, half).astype(BF16)
    ckv = _rms(p[:, E_KVA:E_KVA + MLA_KV_LORA], gkva_ref[...])
    ckv_ref[...] = ckv.astype(BF16)
    kr = _rope_tile(p[:, E_KR:E_KR + LANES], cos, sin, half)
    kr_ref[...] = kr.astype(BF16)

    @pl.when(pl.program_id(0) < N_CTX // TM)
    def _ctx_state():
        ckv_ctx_ref[...] = ckv
        kr_ctx_ref[...] = kr[:, KR_LANE:KR_LANE + MLA_ROPE]

    mq_ref[...] = p[:, E_MQ:E_MK].astype(BF16)
    mk_ref[...] = p[:, E_MK:E_MV].astype(BF16)
    mv_ref[...] = p[:, E_MV:E_MO].astype(BF16)
    mo_ref[...] = p[:, E_MO:E_MG]
    mg_ref[...] = p[:, E_MG:E_COLS]


def _even_in(x, mod_l, g_pre, wp, g_qa, g_kva, wq, cos_t, sin_t):
    row = lambda i: (i, 0)
    const = lambda i: (0, 0)
    widths = (MLA_HEADS * LANES, LANES, LANES, M_HEADS * M_DK, M_HEADS * M_DK, M_HEADS * M_DV,
              M_HEADS * M_DV, LANES)
    dtypes = (BF16, BF16, BF16, BF16, BF16, BF16, F32, F32)
    return pl.pallas_call(
        _even_in_kernel,
        grid=(N_TOK // TM,),
        in_specs=[
            pl.BlockSpec((TM, D_MODEL), row),
            pl.BlockSpec((1, N_MOD, D_MODEL), lambda i: (_group_of_block(i, TM), 0, 0)),
            pl.BlockSpec((1, D_MODEL), const),
            pl.BlockSpec((D_MODEL, E_COLS), const),
            pl.BlockSpec((1, MLA_Q_LORA), const),
            pl.BlockSpec((1, MLA_KV_LORA), const),
            pl.BlockSpec((MLA_Q_LORA, MLA_HEADS * LANES), const),
            pl.BlockSpec((TM, LANES), lambda i: (_rope_block(i, TM), 0)),
            pl.BlockSpec((TM, LANES), lambda i: (_rope_block(i, TM), 0)),
        ],
        out_specs=[pl.BlockSpec((TM, w), row) for w in widths]
        + [pl.BlockSpec((TM, MLA_KV_LORA), lambda i: (_ctx_block(i), 0)),
           pl.BlockSpec((TM, MLA_ROPE), lambda i: (_ctx_block(i), 0))],
        out_shape=[jax.ShapeDtypeStruct((N_TOK, w), dt) for w, dt in zip(widths, dtypes)]
        + [jax.ShapeDtypeStruct((N_CTX, MLA_KV_LORA), F32), jax.ShapeDtypeStruct((N_CTX, MLA_ROPE), F32)],
        compiler_params=_cparams("arbitrary"),
        name="even_in",
    )(x, mod_l, g_pre, wp, g_qa, g_kva, wq, cos_t, sin_t)


def _mla_kernel(t_own, n_cache, tq, *refs):
    if n_cache:
        q_ref, ckv_ref, kr_ref, cckv_ref, ckr_ref, wk_ref, wv_ref, o_ref, ks_ref, vs_ref = refs
    else:
        q_ref, ckv_ref, kr_ref, wk_ref, wv_ref, o_ref, ks_ref, vs_ref = refs
    t_keys = n_cache + t_own

    @pl.when(pl.program_id(1) == 0)
    def _expand():
        if n_cache:
            ckv16 = jnp.concatenate([cckv_ref[0].astype(BF16), ckv_ref[...]], axis=0)
            kr = jnp.concatenate([ckr_ref[0].astype(BF16), kr_ref[...]], axis=0)
        else:
            ckv16, kr = ckv_ref[...], kr_ref[...]
        for hd in range(MLA_HEADS):
            ks_ref[hd] = (jnp.dot(ckv16, wk_ref[hd], preferred_element_type=F32) + kr).astype(BF16)
        for pr in range(MLA_HEADS // 2):
            vs_ref[pr] = jnp.dot(ckv16, wv_ref[pr], preferred_element_type=F32).astype(BF16)

    scale = MLA_QK ** -0.5
    lane = lax.broadcasted_iota(jnp.int32, (tq, LANES), 1)
    for pr in range(MLA_HEADS // 2):
        outs = []
        for j in range(2):
            hd = 2 * pr + j
            qh = q_ref[:, hd * LANES:(hd + 1) * LANES]
            s = lax.dot_general(qh, ks_ref[hd], (((1,), (1,)), ((), ())), preferred_element_type=F32) * scale
            m = jnp.max(s, axis=-1, keepdims=True)
            e = jnp.exp(s - m)
            l = jnp.sum(e, axis=-1, keepdims=True)
            outs.append(jnp.dot(e.astype(BF16), vs_ref[pr], preferred_element_type=F32) / l)
        o_ref[:, pr * LANES:(pr + 1) * LANES] = jnp.where(lane < MLA_V, outs[0], outs[1]).astype(BF16)
    del t_keys


def _mla(q, ckv, kr, wk, wv, *, row0, n_batch, t_own, tq, cache=None):
    n_cache = 0 if cache is None else cache[0].shape[1]
    nq = t_own // tq
    qb0 = row0 // tq
    kb0 = row0 // t_own
    in_specs = [
        pl.BlockSpec((tq, MLA_HEADS * LANES), lambda b, j: (qb0 + b * nq + j, 0)),
        pl.BlockSpec((t_own, LANES), lambda b, j: (kb0 + b, 0)),
        pl.BlockSpec((t_own, LANES), lambda b, j: (kb0 + b, 0)),
    ]
    args = [q, ckv, kr]
    if n_cache:
        in_specs += [pl.BlockSpec((1, n_cache, LANES), lambda b, j: (b, 0, 0))] * 2
        args += list(cache)
    in_specs += [
        pl.BlockSpec((MLA_HEADS, LANES, LANES), lambda b, j: (0, 0, 0)),
        pl.BlockSpec((MLA_HEADS // 2, LANES, LANES), lambda b, j: (0, 0, 0)),
    ]
    args += [wk, wv]
    t_keys = n_cache + t_own
    return pl.pallas_call(
        functools.partial(_mla_kernel, t_own, n_cache, tq),
        grid=(n_batch, nq),
        in_specs=in_specs,
        out_specs=pl.BlockSpec((tq, MLA_HEADS * MLA_V), lambda b, j: (b * nq + j, 0)),
        out_shape=jax.ShapeDtypeStruct((n_batch * t_own, MLA_HEADS * MLA_V), BF16),
        scratch_shapes=[pltpu.VMEM((MLA_HEADS, t_keys, LANES), BF16),
                        pltpu.VMEM((MLA_HEADS // 2, t_keys, LANES), BF16)],
        compiler_params=_cparams("arbitrary", "arbitrary"),
        name="mla_lat" if n_cache else "mla_ctx",
    )(*args)


def _log_sigmoid(x):
    return jnp.minimum(x, 0.0) - jnp.log1p(jnp.exp(-jnp.abs(x)))


def _mlstm_chunk(rev, r0, st, hd, allowed, cum, e_rows, e_cols, mq_ref, mk_ref, mv_ref, ct_ref, ns_ref, ms_ref,
                 h_ref):
    L = M_CHUNK
    nt = (((1,), (1,)), ((), ()))
    gi = (2 * M_HEADS if rev else 0) + hd
    gf = gi + M_HEADS
    edge = 0 if rev else L - 1
    rows = pl.ds(r0, L)
    q16 = mq_ref[rows, hd * M_DK:(hd + 1) * M_DK]
    k16 = mk_ref[rows, hd * M_DK:(hd + 1) * M_DK] * (M_DK ** -0.5)
    v_t = mv_ref[rows, hd * M_DV:(hd + 1) * M_DV].astype(F32).T
    ct_prev = ct_ref[st, :, 0:M_DK]
    n_prev = ns_ref[st, 0:1, 0:M_DK]
    m_prev = ms_ref[st, 0:1, 0:1]
    yield

    b_row = cum[gf:gf + 1, :]
    e_row = e_rows[gi:gi + 1, :]
    e_col = e_cols[:, gi:gi + 1]
    dmat = jnp.where(allowed, b_row + e_col, NEG)
    inter = m_prev + b_row
    mt = jnp.maximum(inter, jnp.max(dmat, axis=0, keepdims=True))
    w_inter = jnp.exp(inter - mt)
    qk = lax.dot_general(k16, q16, nt, preferred_element_type=F32)
    a = qk * jnp.exp(dmat - mt)
    yield
    n8 = jnp.broadcast_to(n_prev, (8, M_DK)).astype(BF16)
    nq = lax.dot_general(n8, q16, nt, preferred_element_type=F32)[0:1]
    num = w_inter * lax.dot_general(ct_prev.astype(BF16), q16, nt, preferred_element_type=F32) \
        + jnp.dot(v_t.astype(BF16), a.astype(BF16), preferred_element_type=F32)
    den = w_inter * nq + jnp.sum(a, axis=0, keepdims=True)
    h_t = num * (1.0 / jnp.maximum(jnp.abs(den), jnp.exp(-mt)))
    h_ref[rows, hd * M_DV:(hd + 1) * M_DV] = h_t.T
    yield

    m_new = mt[:, edge:edge + 1]
    b_last = b_row[:, edge:edge + 1]
    w_s = jnp.exp(b_last + e_row - m_new)
    w_c = jnp.exp(m_prev + b_last - m_new)
    ct_ref[st, :, 0:M_DK] = w_c * ct_prev + jnp.dot((v_t * w_s).astype(BF16), k16, preferred_element_type=F32)
    w8 = jnp.broadcast_to(w_s, (8, L)).astype(BF16)
    ns_ref[st, 0:1, 0:M_DK] = w_c * n_prev + jnp.dot(w8, k16, preferred_element_type=F32)[0:1]
    ms_ref[st, 0:1, :] = jnp.broadcast_to(m_new, (1, LANES))


def _mlstm_kernel(t_len, has_state, nb, *refs):
    n_st = 2 * M_HEADS
    if has_state:
        (mq_ref, mk_ref, mv_ref, mo_ref, mg_ref, gb_ref, gout_ref, c0_ref, n0_ref, m0_ref,
         out_ref, c_ref, n_ref, m_ref, hf_ref, hb_ref, ct_ref, ns_ref, ms_ref) = refs
        pad = jnp.zeros((LANES - M_DK, M_DV), F32)
        for bb in range(nb):
            for st in range(n_st):
                ct_ref[bb * n_st + st] = jnp.concatenate([c0_ref[bb, st], pad], axis=0).T
                ns_ref[bb * n_st + st, 0:1, 0:M_DK] = n0_ref[bb, st:st + 1, :]
                ms_ref[bb * n_st + st, 0:1, :] = m0_ref[bb, st:st + 1, :]
    else:
        (mq_ref, mk_ref, mv_ref, mo_ref, mg_ref, gb_ref, gout_ref,
         out_ref, c_ref, n_ref, m_ref, hf_ref, hb_ref, ct_ref, ns_ref, ms_ref) = refs
        ct_ref[...] = jnp.zeros_like(ct_ref)
        ns_ref[...] = jnp.zeros_like(ns_ref)
        ms_ref[...] = jnp.zeros_like(ms_ref)
    L = M_CHUNK
    nc = t_len // L
    r_idx = lax.broadcasted_iota(jnp.int32, (L, L), 0)
    c_idx = lax.broadcasted_iota(jnp.int32, (L, L), 1)
    lower = c_idx <= r_idx
    upper = c_idx >= r_idx
    lane = lax.broadcasted_iota(jnp.int32, (L, LANES), 1)
    is_forget = (lane % (2 * M_HEADS)) >= M_HEADS
    n_gate = 4 * M_HEADS
    zeros_below = jnp.zeros((L - n_gate, L), F32)

    def gate_sums(r0, rev):
        g = mg_ref[pl.ds(r0, L), :] + gb_ref[...]
        g = jnp.where(is_forget, _log_sigmoid(g), g)
        rows = g.T[0:n_gate, :]
        tri = (lower if rev else upper).astype(BF16)
        hi = rows.astype(BF16)
        rest = rows - hi.astype(F32)
        mid = rest.astype(BF16)
        lo = (rest - mid.astype(F32)).astype(BF16)
        cum = (jnp.dot(hi, tri, preferred_element_type=F32) + jnp.dot(mid, tri, preferred_element_type=F32)
               + jnp.dot(lo, tri, preferred_element_type=F32))
        e_rows = rows - pltpu.roll(cum, n_gate - M_HEADS, 0)
        e_cols = jnp.concatenate([e_rows, zeros_below], axis=0).T
        return cum, e_rows, e_cols

    def step(c, carry):
        chunks = []
        for bb in range(nb):
            for rev in (False, True):
                r0 = pl.multiple_of(bb * t_len + (nc - 1 - c if rev else c) * L, L)
                cum, e_rows, e_cols = gate_sums(r0, rev)
                allowed = lower if rev else upper
                for hd in range(M_HEADS):
                    st = bb * n_st + (M_HEADS if rev else 0) + hd
                    chunks.append(_mlstm_chunk(rev, r0, st, hd, allowed, cum, e_rows, e_cols, mq_ref, mk_ref, mv_ref,
                                               ct_ref, ns_ref, ms_ref, hb_ref if rev else hf_ref))
        _run_staged(chunks)
        return carry

    lax.fori_loop(0, nc, step, 0)
    for bb in range(nb):
        for st in range(n_st):
            c_ref[bb, st] = ct_ref[bb * n_st + st].T[0:M_DK, :]
            n_ref[bb, st:st + 1, :] = ns_ref[bb * n_st + st, 0:1, 0:M_DK]
            m_ref[bb, st:st + 1, :] = ms_ref[bb * n_st + st, 0:1, :]
    for hd in range(M_HEADS):
        sl = slice(hd * M_DV, (hd + 1) * M_DV)
        hm = hf_ref[:, sl] + hb_ref[:, sl]
        hm = hm * lax.rsqrt(jnp.mean(hm * hm, axis=-1, keepdims=True) + EPS) * gout_ref[:, sl]
        out_ref[:, sl] = (jax.nn.sigmoid(mo_ref[:, sl]) * hm).astype(BF16)


def _mlstm(mq, mk, mv, mo, mg, gate_b, g_out, *, row0, n_batch, t_len, state=None):
    nb = M_SEQ_PER_STEP
    rows = nb * t_len
    kb0 = row0 // rows
    row = lambda b: (kb0 + b, 0)
    const = lambda b: (0, 0)
    st4 = lambda b: (b, 0, 0, 0)
    st3 = lambda b: (b, 0, 0)
    n_st = 2 * M_HEADS
    in_specs = [
        pl.BlockSpec((rows, M_HEADS * M_DK), row),
        pl.BlockSpec((rows, M_HEADS * M_DK), row),
        pl.BlockSpec((rows, M_HEADS * M_DV), row),
        pl.BlockSpec((rows, M_HEADS * M_DV), row),
        pl.BlockSpec((rows, LANES), row),
        pl.BlockSpec((1, LANES), const),
        pl.BlockSpec((1, M_HEADS * M_DV), const),
    ]
    args = [mq, mk, mv, mo, mg, gate_b, g_out]
    state_specs = [pl.BlockSpec((nb, n_st, M_DK, M_DV), st4), pl.BlockSpec((nb, n_st, M_DK), st3),
                   pl.BlockSpec((nb, n_st, LANES), st3)]
    if state is not None:
        in_specs += state_specs
        args += list(state)
    return pl.pallas_call(
        functools.partial(_mlstm_kernel, t_len, state is not None, nb),
        grid=(n_batch // nb,),
        in_specs=in_specs,
        out_specs=[pl.BlockSpec((rows, M_HEADS * M_DV), lambda b: (b, 0))] + state_specs,
        out_shape=[jax.ShapeDtypeStruct((n_batch * t_len, M_HEADS * M_DV), BF16),
                   jax.ShapeDtypeStruct((n_batch, n_st, M_DK, M_DV), F32),
                   jax.ShapeDtypeStruct((n_batch, n_st, M_DK), F32),
                   jax.ShapeDtypeStruct((n_batch, n_st, LANES), F32)],
        scratch_shapes=[pltpu.VMEM((rows, M_HEADS * M_DV), F32), pltpu.VMEM((rows, M_HEADS * M_DV), F32),
                        pltpu.VMEM((nb * n_st, M_DV, LANES), F32),
                        pltpu.VMEM((nb * n_st, 8, LANES), F32), pltpu.VMEM((nb * n_st, 8, LANES), F32)],
        compiler_params=_cparams("arbitrary"),
        name="mlstm_lat" if state is not None else "mlstm_ctx",
    )(*args)


def _odd_in_kernel(x_ref, mod_ref, g_ref, wp_ref, cos_ref, sin_ref, q_ref, k_ref, v_ref, k_ctx_ref, v_ctx_ref):
    h = _modulated(x_ref[...], g_ref[...], mod_ref, 1).astype(BF16)
    p = jnp.dot(h, wp_ref[...], preferred_element_type=F32)
    cos = cos_ref[...]
    sin = sin_ref[...]
    half = S_HEAD_DIM // 4
    n_q = S_HEADS * S_HEAD_DIM
    n_k = S_KV_HEADS * LANES
    for t in range(n_q // LANES):
        sl = slice(t * LANES, (t + 1) * LANES)
        q_ref[:, sl] = _rope_tile(p[:, sl], cos, sin, half).astype(BF16)
    k_tiles = [_rope_tile(p[:, n_q + g * LANES:n_q + (g + 1) * LANES], cos, sin, half) for g in range(S_KV_HEADS)]
    v_tiles = [p[:, n_q + n_k + g * LANES:n_q + n_k + (g + 1) * LANES] for g in range(S_KV_HEADS)]
    for g in range(S_KV_HEADS):
        k_ref[:, g * LANES:(g + 1) * LANES] = k_tiles[g].astype(BF16)
        v_ref[:, g * LANES:(g + 1) * LANES] = v_tiles[g].astype(BF16)

    @pl.when(pl.program_id(0) < N_CTX // TM)
    def _ctx_state():
        for bb in range(TM // SEQ):
            rows = slice(bb * SEQ, (bb + 1) * SEQ)
            for g in range(S_KV_HEADS):
                k_ctx_ref[bb, g] = k_tiles[g][rows, 0:S_HEAD_DIM]
                v_ctx_ref[bb, g] = v_tiles[g][rows, 0:S_HEAD_DIM]


def _odd_in(x, mod_l, g_pre, wp, cos_t, sin_t):
    row = lambda i: (i, 0)
    const = lambda i: (0, 0)
    widths = (S_HEADS * S_HEAD_DIM, S_KV_HEADS * LANES, S_KV_HEADS * LANES)
    return pl.pallas_call(
        _odd_in_kernel,
        grid=(N_TOK // TM,),
        in_specs=[
            pl.BlockSpec((TM, D_MODEL), row),
            pl.BlockSpec((1, N_MOD, D_MODEL), lambda i: (_group_of_block(i, TM), 0, 0)),
            pl.BlockSpec((1, D_MODEL), const),
            pl.BlockSpec((D_MODEL, sum(widths)), const),
            pl.BlockSpec((TM, LANES), lambda i: (_rope_block(i, TM), 0)),
            pl.BlockSpec((TM, LANES), lambda i: (_rope_block(i, TM), 0)),
        ],
        out_specs=[pl.BlockSpec((TM, w), row) for w in widths]
        + [pl.BlockSpec((TM // SEQ, S_KV_HEADS, SEQ, S_HEAD_DIM), lambda i: (_ctx_block(i), 0, 0, 0))] * 2,
        out_shape=[jax.ShapeDtypeStruct((N_TOK, w), BF16) for w in widths]
        + [jax.ShapeDtypeStruct((BATCH, S_KV_HEADS, SEQ, S_HEAD_DIM), F32)] * 2,
        compiler_params=_cparams("arbitrary"),
        name="odd_in",
    )(x, mod_l, g_pre, wp, cos_t, sin_t)


def _gqa_heads(q_ref, keys, vals, sink_ref, mask, o_ref, tq):
    scale = S_HEAD_DIM ** -0.5
    lane = lax.broadcasted_iota(jnp.int32, (tq, LANES), 1)
    low = lane < S_HEAD_DIM

    def kv_group(g):
        k2, v2 = keys(g), vals(g)
        col = g * S_GROUP * S_HEAD_DIM
        tiles = [q_ref[:, col:col + LANES].astype(F32) * scale, q_ref[:, col + LANES:col + 2 * LANES].astype(F32) * scale]
        q4 = jnp.concatenate([jnp.where(low, tiles[0], 0.0), jnp.where(low, 0.0, tiles[0]),
                              jnp.where(low, tiles[1], 0.0), jnp.where(low, 0.0, tiles[1])], axis=0).astype(BF16)
        yield
        s = lax.dot_general(k2, q4, (((1,), (1,)), ((), ())), preferred_element_type=F32)
        if mask is not None:
            s = jnp.where(mask, s, NEG)
        yield
        sk = jnp.concatenate([jnp.full((1, tq), sink_ref[g * S_GROUP + j], F32) for j in range(S_GROUP)], axis=1)
        m = jnp.maximum(jnp.max(s, axis=0, keepdims=True), sk)
        e = jnp.exp(s - m)
        yield
        l = jnp.sum(e, axis=0, keepdims=True) + jnp.exp(sk - m)
        o_t = lax.dot_general(v2, e.astype(BF16), (((0,), (0,)), ((), ())), preferred_element_type=F32)
        yield
        o4 = (o_t * (1.0 / l)).T
        o_ref[:, col:col + LANES] = jnp.where(low, o4[0:tq], o4[tq:2 * tq]).astype(BF16)
        o_ref[:, col + LANES:col + 2 * LANES] = jnp.where(low, o4[2 * tq:3 * tq], o4[3 * tq:4 * tq]).astype(BF16)

    _run_staged(kv_group(g) for g in range(S_KV_HEADS))


def _gqa_ctx_kernel(sink_ref, q_ref, k_ref, v_ref, o_ref):
    keys = lambda g: k_ref[:, g * LANES:(g + 1) * LANES]
    vals = lambda g: v_ref[:, g * LANES:(g + 1) * LANES]
    _gqa_heads(q_ref, keys, vals, sink_ref, None, o_ref, SEQ)


def _gqa_ctx(q, kd, vd, sink):
    row = lambda b: (b, 0)
    return pl.pallas_call(
        _gqa_ctx_kernel,
        grid=(BATCH,),
        in_specs=[
            pl.BlockSpec(memory_space=pltpu.SMEM),
            pl.BlockSpec((SEQ, S_HEADS * S_HEAD_DIM), row),
            pl.BlockSpec((SEQ, S_KV_HEADS * LANES), row),
            pl.BlockSpec((SEQ, S_KV_HEADS * LANES), row),
        ],
        out_specs=pl.BlockSpec((SEQ, S_HEADS * S_HEAD_DIM), row),
        out_shape=jax.ShapeDtypeStruct((N_CTX, S_HEADS * S_HEAD_DIM), BF16),
        compiler_params=_cparams("arbitrary"),
        name="gqa_ctx",
    )(sink, q, kd, vd)


def _gqa_lat_kernel(sink_ref, q_ref, k_ref, v_ref, ck_ref, cv_ref, o_ref):
    n = pl.program_id(1)
    nb = DEC_SEQ // S_BLOCK
    B = S_BLOCK
    prev0 = pl.multiple_of(jnp.maximum(n - 1, 0) * B, B)
    cur0 = pl.multiple_of(n * B, B)
    next0 = pl.multiple_of(jnp.minimum(n + 1, nb - 1) * B, B)
    t_idx = lax.broadcasted_iota(jnp.int32, (PAST_LEN + 3 * B, S_GROUP * B), 1) % B
    c_idx = lax.broadcasted_iota(jnp.int32, (PAST_LEN + 3 * B, S_GROUP * B), 0)
    s_prev = c_idx - PAST_LEN
    s_next = c_idx - (PAST_LEN + 2 * B)
    in_prev = (c_idx >= PAST_LEN) & (c_idx < PAST_LEN + B)
    in_next = c_idx >= PAST_LEN + 2 * B
    far = jnp.int32(4 * B)
    bad_prev = in_prev & (s_prev < t_idx + jnp.where(n == 0, far, 0))
    bad_next = in_next & (s_next + jnp.where(n == nb - 1, far, 0) > t_idx)
    mask = jnp.logical_not(bad_prev | bad_next)

    def gather(ref, cache_ref, g):
        sl = slice(g * LANES, (g + 1) * LANES)
        return jnp.concatenate([cache_ref[0, g], ref[pl.ds(prev0, B), sl], ref[pl.ds(cur0, B), sl],
                                ref[pl.ds(next0, B), sl]], axis=0)

    keys = lambda g: gather(k_ref, ck_ref, g)
    vals = lambda g: gather(v_ref, cv_ref, g)
    _gqa_heads(q_ref, keys, vals, sink_ref, mask, o_ref, B)


def _gqa_lat(q, kd, vd, cache_k2, cache_v2, sink):
    nb = DEC_SEQ // S_BLOCK
    qb0 = N_CTX // S_BLOCK
    kb0 = N_CTX // DEC_SEQ
    return pl.pallas_call(
        _gqa_lat_kernel,
        grid=(DEC_BATCH, nb),
        in_specs=[
            pl.BlockSpec(memory_space=pltpu.SMEM),
            pl.BlockSpec((S_BLOCK, S_HEADS * S_HEAD_DIM), lambda b, n: (qb0 + b * nb + n, 0)),
            pl.BlockSpec((DEC_SEQ, S_KV_HEADS * LANES), lambda b, n: (kb0 + b, 0)),
            pl.BlockSpec((DEC_SEQ, S_KV_HEADS * LANES), lambda b, n: (kb0 + b, 0)),
            pl.BlockSpec((1, S_KV_HEADS, PAST_LEN, LANES), lambda b, n: (b, 0, 0, 0)),
            pl.BlockSpec((1, S_KV_HEADS, PAST_LEN, LANES), lambda b, n: (b, 0, 0, 0)),
        ],
        out_specs=pl.BlockSpec((S_BLOCK, S_HEADS * S_HEAD_DIM), lambda b, n: (b * nb + n, 0)),
        out_shape=jax.ShapeDtypeStruct((N_LAT, S_HEADS * S_HEAD_DIM), BF16),
        compiler_params=_cparams("arbitrary", "arbitrary"),
        name="gqa_lat",
    )(sink, q, kd, vd, cache_k2, cache_v2)


def _rope_tables(rot_dim, lane_off, reps):
    nf = rot_dim // 4
    inv = np.float32(ROPE_BASE) ** (-np.arange(nf, dtype=np.float32) / np.float32(nf))
    pos = np.arange(DEC_SEQ)
    ang_r = (pos // GRID_W).astype(np.float32)[:, None] * inv
    ang_c = (pos % GRID_W).astype(np.float32)[:, None] * inv
    cos_g = np.concatenate([np.cos(ang_r), np.cos(ang_r), np.cos(ang_c), np.cos(ang_c)], axis=1)
    sin_g = np.concatenate([-np.sin(ang_r), np.sin(ang_r), -np.sin(ang_c), np.sin(ang_c)], axis=1)
    cos_t = np.ones((TM + DEC_SEQ, LANES), np.float32)
    sin_t = np.zeros((TM + DEC_SEQ, LANES), np.float32)
    for r in range(reps):
        lo = lane_off + r * rot_dim
        cos_t[TM:, lo:lo + rot_dim] = cos_g
        sin_t[TM:, lo:lo + rot_dim] = sin_g
    return jnp.asarray(cos_t, F32), jnp.asarray(sin_t, F32)


def _even_weights(w_in, w_qb, w_kvb, gate_b):
    z = lambda n: jnp.zeros((D_MODEL, n), F32)
    idx = np.cumsum([MLA_Q_LORA, MLA_KV_LORA, MLA_ROPE, M_HEADS * M_DK, M_HEADS * M_DK, M_HEADS * M_DV,
                     M_HEADS * M_DV])
    q_a, kv_a, k_rope, mq, mk, mv, mo, mg = jnp.split(w_in, idx, axis=1)
    wp = jnp.concatenate([q_a, kv_a, z(KR_LANE), k_rope, z(LANES - KR_LANE - MLA_ROPE), mq, mk, mv, mo, mg,
                          z(LANES - 4 * M_HEADS)], axis=1).astype(BF16)
    wq = jnp.pad(w_qb.reshape(MLA_Q_LORA, MLA_HEADS, MLA_QK), ((0, 0), (0, 0), (0, LANES - MLA_QK)))
    wq = wq.reshape(MLA_Q_LORA, MLA_HEADS * LANES).astype(BF16)
    kvb = w_kvb.reshape(MLA_KV_LORA, MLA_HEADS, MLA_NOPE + MLA_V)
    wk = jnp.pad(kvb[:, :, :MLA_NOPE], ((0, 0), (0, 0), (0, LANES - MLA_NOPE))).transpose(1, 0, 2).astype(BF16)
    wv = kvb[:, :, MLA_NOPE:].reshape(MLA_KV_LORA, MLA_HEADS // 2, 2 * MLA_V).transpose(1, 0, 2).astype(BF16)
    gb = jnp.pad(gate_b, (0, LANES - 4 * M_HEADS)).reshape(1, LANES)
    return wp, wq, wk, wv, gb


def _dup_heads(w):
    w3 = w.reshape(D_MODEL, S_KV_HEADS, S_HEAD_DIM)
    return jnp.concatenate([w3, w3], axis=-1).reshape(D_MODEL, S_KV_HEADS * LANES)


def _odd_weights(w_in):
    n_q = S_HEADS * S_HEAD_DIM
    n_kv = S_KV_HEADS * S_HEAD_DIM
    return jnp.concatenate([w_in[:, :n_q], _dup_heads(w_in[:, n_q:n_q + n_kv]), _dup_heads(w_in[:, n_q + n_kv:])],
                           axis=1).astype(BF16)


def kernel(x_prompt, x_sample, cache_mla_ckv, cache_mla_krope, state_mlstm_C, state_mlstm_n, state_mlstm_m,
           cache_swa_k, cache_swa_v, c, c_ctx, ada_w, ada_b, norm_g, ffn_w_gate, ffn_w_up, ffn_w_down,
           even_w_in, mla_g_qa, mla_g_kva, mla_w_qb, mla_w_kvb, mlstm_gate_b, mlstm_g_out, even_w_out,
           odd_w_in, swa_sink, odd_w_out):
    xs = (x_prompt.reshape(N_CTX, D_MODEL), x_sample.reshape(N_LAT, D_MODEL))
    cond = jnp.concatenate([c_ctx[None], c, jnp.zeros((COND_ROWS - N_GROUPS, D_MODEL), F32)], axis=0)
    mod = _adaln(cond, ada_w, ada_b)
    cos_e, sin_e = _rope_tables(MLA_ROPE, KR_LANE, 1)
    cos_o, sin_o = _rope_tables(S_HEAD_DIM, 0, LANES // S_HEAD_DIM)
    w32 = (ffn_w_gate, ffn_w_up, ffn_w_down)
    w16 = tuple(w[0, 0].astype(BF16) for w in w32)

    ckv_l, kr_l, mc_l, mn_l, mm_l, sk_l, sv_l = [], [], [], [], [], [], []
    for l in range(DEPTH):
        mod_l = mod[l]
        g = norm_g[l]
        i = l // 2
        (x,), w16 = _ffn(xs, mod_l, g[0:2], w16, 0, next_w=(w32, l, 1))
        if l % 2 == 0:
            wp, wq, wk, wv, gb = _even_weights(even_w_in[i], mla_w_qb[i], mla_w_kvb[i], mlstm_gate_b[i])
            q, ckv, kr, mq, mk, mv, mo, mg, ckv_ctx, kr_ctx = _even_in(
                x, mod_l, g[2:3], wp, mla_g_qa[i].reshape(1, -1), mla_g_kva[i].reshape(1, -1), wq, cos_e, sin_e)
            cache_kr = jnp.pad(cache_mla_krope[:, i], ((0, 0), (0, 0), (KR_LANE, LANES - KR_LANE - MLA_ROPE)))
            att_c = _mla(q, ckv, kr, wk, wv, row0=0, n_batch=BATCH, t_own=SEQ, tq=SEQ)
            att_l = _mla(q, ckv, kr, wk, wv, row0=N_CTX, n_batch=DEC_BATCH, t_own=DEC_SEQ, tq=512,
                         cache=(cache_mla_ckv[:, i], cache_kr))
            g_out = mlstm_g_out[i].reshape(1, -1)
            mo_c, s_c, s_n, s_m = _mlstm(mq, mk, mv, mo, mg, gb, g_out, row0=0, n_batch=BATCH, t_len=SEQ)
            n_st = 2 * M_HEADS
            state = (state_mlstm_C[:, i].reshape(DEC_BATCH, n_st, M_DK, M_DV),
                     state_mlstm_n[:, i].reshape(DEC_BATCH, n_st, M_DK),
                     jnp.broadcast_to(state_mlstm_m[:, i].reshape(DEC_BATCH, n_st, 1), (DEC_BATCH, n_st, LANES)))
            mo_l, _, _, _ = _mlstm(mq, mk, mv, mo, mg, gb, g_out, row0=N_CTX, n_batch=DEC_BATCH, t_len=DEC_SEQ,
                                   state=state)
            w_out = even_w_out[i].astype(BF16)
            n_att = MLA_HEADS * MLA_V
            acts, act_w = [(att_c, att_l), (mo_c, mo_l)], [w_out[:n_att], w_out[n_att:]]
            ckv_l.append(ckv_ctx.reshape(BATCH, SEQ, MLA_KV_LORA))
            kr_l.append(kr_ctx.reshape(BATCH, SEQ, MLA_ROPE))
            mc_l.append(s_c.reshape(BATCH, 2, M_HEADS, M_DK, M_DV))
            mn_l.append(s_n.reshape(BATCH, 2, M_HEADS, M_DK))
            mm_l.append(s_m[:, :, 0].reshape(BATCH, 2, M_HEADS))
        else:
            q, kd, vd, k_ctx, v_ctx = _odd_in(x, mod_l, g[2:3], _odd_weights(odd_w_in[i]), cos_o, sin_o)
            dup = lambda a: jnp.concatenate([a, a], axis=-1).astype(BF16)
            o_c = _gqa_ctx(q, kd, vd, swa_sink[i])
            o_l = _gqa_lat(q, kd, vd, dup(cache_swa_k[:, i]), dup(cache_swa_v[:, i]), swa_sink[i])
            acts, act_w = [(o_c, o_l)], [odd_w_out[i].astype(BF16)]
            sk_l.append(k_ctx)
            sv_l.append(v_ctx)
        last = l == DEPTH - 1
        xs, w16 = _ffn((x,), mod_l, g[3:6], w16, 1, acts=acts, act_w=act_w, dual_out=last,
                       next_w=None if last else (w32, l + 1, 0))

    return (xs[0].reshape(BATCH, SEQ, D_MODEL), xs[1].reshape(DEC_BATCH, DEC_SEQ, D_MODEL),
            jnp.stack(ckv_l, 1), jnp.stack(kr_l, 1), jnp.stack(mc_l, 1), jnp.stack(mn_l, 1), jnp.stack(mm_l, 1),
            jnp.stack(sk_l, 1), jnp.stack(sv_l, 1))
```

```python
import functools

import jax
import jax.numpy as jnp
import numpy as np
from jax import lax
from jax.experimental import pallas as pl
from jax.experimental.pallas import tpu as pltpu

F32 = jnp.float32
BF16 = jnp.bfloat16

D_MODEL = 1024
BATCH = 32
SEQ = 256
DEPTH = 4
N_EVEN = (DEPTH + 1) // 2
N_ODD = DEPTH // 2
DEC_BATCH = 4
DEC_SEQ = 1024
PAST_LEN = 512
GRID_W = 64
N_MOD = 9
D_FF = 2816
EPS = 1e-6
ROPE_BASE = 10000.0
NEG = -1e30
MLA_HEADS = 8
MLA_NOPE = 64
MLA_ROPE = 32
MLA_QK = MLA_NOPE + MLA_ROPE
MLA_V = 64
MLA_Q_LORA = 256
MLA_KV_LORA = 128
M_HEADS = 4
M_DK = 64
M_DV = 128
M_CHUNK = 128
S_HEADS = 16
S_KV_HEADS = 4
S_GROUP = S_HEADS // S_KV_HEADS
S_HEAD_DIM = 64
S_WINDOW = 128
S_BLOCK = 128

N_CTX = BATCH * SEQ
N_LAT = DEC_BATCH * DEC_SEQ
N_TOK = N_CTX + N_LAT
N_GROUPS = 1 + DEC_BATCH
COND_ROWS = 8

LANES = 128
FF_CHUNK = 256
N_FF_CHUNKS = D_FF // FF_CHUNK
assert N_FF_CHUNKS * FF_CHUNK == D_FF

TM = 512
ADA_TN = 1536
M_SEQ_PER_STEP = 2
FFN_PARTS = 2
VMEM_LIMIT = 52 * 1024 * 1024

E_QA, E_KVA, E_KR, E_MQ, E_MK, E_MV, E_MO, E_MG, E_COLS = 0, 256, 384, 512, 768, 1024, 1536, 2048, 2176
KR_LANE = MLA_NOPE


def _cparams(*sem):
    return pltpu.CompilerParams(dimension_semantics=sem, vmem_limit_bytes=VMEM_LIMIT)


def _group_of_block(i, tm):
    n_ctx = N_CTX // tm
    per_lat = DEC_SEQ // tm
    return jnp.where(i < n_ctx, 0, 1 + (i - n_ctx) // per_lat)


def _rope_block(i, tm):
    n_ctx = N_CTX // tm
    per_lat = DEC_SEQ // tm
    return jnp.where(i < n_ctx, 0, 1 + (i - n_ctx) % per_lat)


def _rms(x, g):
    return x * lax.rsqrt(jnp.mean(x * x, axis=-1, keepdims=True) + EPS) * g


def _modulated(x, g_pre, mod_ref, sub):
    shift = mod_ref[0, 3 * sub:3 * sub + 1, :]
    scale = mod_ref[0, 3 * sub + 1:3 * sub + 2, :]
    return _rms(x, g_pre) * (1.0 + scale) + shift


def _silu(x):
    return x * jax.nn.sigmoid(x)


def _swap_halves(x, half):
    lane = lax.broadcasted_iota(jnp.int32, x.shape, 1)
    up = pltpu.roll(x, LANES - half, 1)
    down = pltpu.roll(x, half, 1)
    return jnp.where(lane % (2 * half) < half, up, down)


def _rope_tile(x, cos, sin, half):
    return x * cos + _swap_halves(x, half) * sin


def _run_staged(stages):
    stages = list(stages)
    while stages:
        stages = [s for s in stages if next(s, True) is None]


def _adaln_kernel(cond_ref, w_ref, b_ref, o_ref):
    s = _silu(cond_ref[...]).astype(BF16)
    o_ref[0] = jnp.dot(s, w_ref[0].astype(BF16), preferred_element_type=F32) + b_ref[0]


def _adaln(cond, ada_w, ada_b):
    n_out = N_MOD * D_MODEL
    out = pl.pallas_call(
        _adaln_kernel,
        grid=(DEPTH, n_out // ADA_TN),
        in_specs=[
            pl.BlockSpec((COND_ROWS, D_MODEL), lambda l, j: (0, 0)),
            pl.BlockSpec((1, D_MODEL, ADA_TN), lambda l, j: (l, 0, j)),
            pl.BlockSpec((1, 1, ADA_TN), lambda l, j: (l, 0, j)),
        ],
        out_specs=pl.BlockSpec((1, COND_ROWS, ADA_TN), lambda l, j: (l, 0, j)),
        out_shape=jax.ShapeDtypeStruct((DEPTH, COND_ROWS, n_out), F32),
        compiler_params=_cparams("arbitrary", "arbitrary"),
        name="adaln",
    )(cond, ada_w, ada_b.reshape(DEPTH, 1, n_out))
    return out.reshape(DEPTH, COND_ROWS, N_MOD, D_MODEL)


def _ctx_block(i):
    return jnp.minimum(i, N_CTX // TM - 1)


def _lat_block(i):
    return jnp.maximum(i - N_CTX // TM, 0)


def _dual_specs(width):
    return [pl.BlockSpec((TM, width), lambda i: (_ctx_block(i), 0)),
            pl.BlockSpec((TM, width), lambda i: (_lat_block(i), 0))]


def _ffn_kernel(sub, dual_in, dual_out, n_act, convert, *refs):
    it = iter(refs)
    take = lambda n: [next(it) for _ in range(n)]
    a_refs = take(2 * n_act)
    w_refs = take(n_act)
    x_refs = take(2 if dual_in else 1)
    mod_ref, g_ref, wg_ref, wu_ref, wd_ref = take(5)
    nxt_refs = take(3) if convert else []
    o_refs = take(2 if dual_out else 1)
    nxt_out_refs = take(3) if convert else []
    h_ref, xs_ref = take(2)
    step = pl.program_id(0)
    is_ctx = step < N_CTX // TM
    g0 = 1 if n_act else 0

    gate_row = mod_ref[0, 3 * sub + 2:3 * sub + 3, :]
    results = {}

    def rows_of(part):
        rs = slice(part * (TM // FFN_PARTS), (part + 1) * (TM // FFN_PARTS))
        x = jnp.where(is_ctx, x_refs[0][rs, :], x_refs[1][rs, :]) if dual_in else x_refs[0][rs, :]
        if n_act:
            y = None
            for k in range(n_act):
                a = jnp.where(is_ctx, a_refs[2 * k][rs, :], a_refs[2 * k + 1][rs, :])
                d = jnp.dot(a, w_refs[k][...], preferred_element_type=F32)
                y = d if y is None else y + d
            x = x + mod_ref[0, 5:6, :] * _rms(y, g_ref[0:1, :])
        xs_ref[rs, :] = x
        h = _modulated(x, g_ref[g0:g0 + 1, :], mod_ref, sub).astype(BF16)
        yield
        for c in range(N_FF_CHUNKS):
            sl = slice(c * FF_CHUNK, (c + 1) * FF_CHUNK)
            gate = jnp.dot(h, wg_ref[:, sl], preferred_element_type=F32)
            up = jnp.dot(h, wu_ref[:, sl], preferred_element_type=F32)
            h_ref[rs, sl] = (_silu(gate) * up).astype(BF16)
            yield
        y = jnp.dot(h_ref[rs, :], wd_ref[...], preferred_element_type=F32)
        yield
        results[part] = xs_ref[rs, :] + 0.5 * gate_row * _rms(y, g_ref[g0 + 1:g0 + 2, :])

    _run_staged(rows_of(part) for part in range(FFN_PARTS))
    result = jnp.concatenate([results[part] for part in range(FFN_PARTS)], axis=0)

    if dual_out:
        @pl.when(is_ctx)
        def _ctx():
            o_refs[0][...] = result

        @pl.when(jnp.logical_not(is_ctx))
        def _lat():
            o_refs[1][...] = result
    else:
        o_refs[0][...] = result

    if convert:
        @pl.when(step < N_FF_CHUNKS)
        def _next_weights():
            for src_ref, dst_ref in zip(nxt_refs, nxt_out_refs):
                dst_ref[...] = src_ref[...].astype(BF16)


def _ffn(xs, mod_l, g_rows, w16, half, *, acts=(), act_w=(), dual_out=False, next_w=None):
    dual_in = len(xs) == 2
    n_act = len(acts)
    convert = next_w is not None
    sub = 2 * half
    const = lambda i: (0, 0)
    once = dict(pipeline_mode=pl.Buffered(1))
    in_specs, args = [], []
    for a_c, a_l in acts:
        in_specs += _dual_specs(a_c.shape[1])
        args += [a_c, a_l]
    in_specs += [pl.BlockSpec(w.shape, const, **once) for w in act_w]
    args += list(act_w)
    in_specs += _dual_specs(D_MODEL) if dual_in else [pl.BlockSpec((TM, D_MODEL), lambda i: (i, 0))]
    args += list(xs)
    in_specs += [
        pl.BlockSpec((1, N_MOD, D_MODEL), lambda i: (_group_of_block(i, TM), 0, 0)),
        pl.BlockSpec(g_rows.shape, const),
        pl.BlockSpec((D_MODEL, D_FF), const, **once),
        pl.BlockSpec((D_MODEL, D_FF), const, **once),
        pl.BlockSpec((D_FF, D_MODEL), const, **once),
    ]
    args += [mod_l, g_rows, *w16]
    if dual_out:
        out_specs = _dual_specs(D_MODEL)
        out_shape = [jax.ShapeDtypeStruct((N_CTX, D_MODEL), F32), jax.ShapeDtypeStruct((N_LAT, D_MODEL), F32)]
    else:
        out_specs = [pl.BlockSpec((TM, D_MODEL), lambda i: (i, 0))]
        out_shape = [jax.ShapeDtypeStruct((N_TOK, D_MODEL), F32)]
    if convert:
        (wg32, wu32, wd32), ln, hn = next_w
        chunk = lambda i: jnp.minimum(i, N_FF_CHUNKS - 1)
        in_specs += [
            pl.BlockSpec((None, None, D_MODEL, FF_CHUNK), lambda i: (ln, hn, 0, chunk(i))),
            pl.BlockSpec((None, None, D_MODEL, FF_CHUNK), lambda i: (ln, hn, 0, chunk(i))),
            pl.BlockSpec((None, None, FF_CHUNK, D_MODEL), lambda i: (ln, hn, chunk(i), 0)),
        ]
        args += [wg32, wu32, wd32]
        out_specs += [
            pl.BlockSpec((D_MODEL, FF_CHUNK), lambda i: (0, chunk(i))),
            pl.BlockSpec((D_MODEL, FF_CHUNK), lambda i: (0, chunk(i))),
            pl.BlockSpec((FF_CHUNK, D_MODEL), lambda i: (chunk(i), 0)),
        ]
        out_shape += [jax.ShapeDtypeStruct((D_MODEL, D_FF), BF16), jax.ShapeDtypeStruct((D_MODEL, D_FF), BF16),
                      jax.ShapeDtypeStruct((D_FF, D_MODEL), BF16)]
    outs = pl.pallas_call(
        functools.partial(_ffn_kernel, sub, dual_in, dual_out, n_act, convert),
        grid=(N_TOK // TM,),
        in_specs=in_specs,
        out_specs=out_specs,
        out_shape=out_shape,
        scratch_shapes=[pltpu.VMEM((TM, D_FF), BF16), pltpu.VMEM((TM, D_MODEL), F32)],
        compiler_params=_cparams("arbitrary"),
        name="mix_ffn" if n_act else "ffn",
    )(*args)
    n_stream = 2 if dual_out else 1
    stream = tuple(outs[:n_stream])
    return stream, (tuple(outs[n_stream:]) if convert else None)


def _even_in_kernel(has_prev, x_ref, mod_ref, g_ref, wp_ref, gqa_ref, gkva_ref, wq_ref, cos_ref, sin_ref, *refs):
    prev_refs, refs = (refs[:2], refs[2:]) if has_prev else ((), refs)
    q_ref, ckv_ref, kr_ref, mq_ref, mk_ref, mv_ref, mo_ref, mg_ref, ckv_ctx_ref, kr_ctx_ref = refs
    h = _modulated(x_ref[...], g_ref[...], mod_ref, 1).astype(BF16)
    p = jnp.dot(h, wp_ref[...], preferred_element_type=F32)
    cos = cos_ref[...]
    sin = sin_ref[...]
    half = MLA_ROPE // 4
    qn = _rms(p[:, E_QA:E_QA + MLA_Q_LORA], gqa_ref[...]).astype(BF16)
    q = jnp.dot(qn, wq_ref[...], preferred_element_type=F32)
    for hd in range(MLA_HEADS):
        sl = slice(hd * LANES, (hd + 1) * LANES)
        q_ref[:, sl] = _rope_tile(q[:, sl], cos, sin, half).astype(BF16)
    ckv = _rms(p[:, E_KVA:E_KVA + MLA_KV_LORA], gkva_ref[...])
    ckv_ref[...] = ckv.astype(BF16)
    kr = _rope_tile(p[:, E_KR:E_KR + LANES], cos, sin, half)
    kr_ref[...] = kr.astype(BF16)

    @pl.when(pl.program_id(0) < N_CTX // TM)
    def _ctx_state():
        for bb in range(TM // SEQ):
            rows = slice(bb * SEQ, (bb + 1) * SEQ)
            if has_prev:
                ckv_ctx_ref[bb, 0] = prev_refs[0][bb]
                kr_ctx_ref[bb, 0] = prev_refs[1][bb]
                ckv_ctx_ref[bb, 1] = ckv[rows]
                kr_ctx_ref[bb, 1] = kr[rows, KR_LANE:KR_LANE + MLA_ROPE]
            else:
                ckv_ctx_ref[bb] = ckv[rows]
                kr_ctx_ref[bb] = kr[rows, KR_LANE:KR_LANE + MLA_ROPE]

    mq_ref[...] = p[:, E_MQ:E_MK].astype(BF16)
    mk_ref[...] = p[:, E_MK:E_MV].astype(BF16)
    mv_ref[...] = p[:, E_MV:E_MO].astype(BF16)
    mo_ref[...] = p[:, E_MO:E_MG]
    mg_ref[...] = p[:, E_MG:E_COLS]


def _state_specs(shapes, prev):
    nseq = TM // SEQ
    zeros = lambda s: (0,) * len(s)
    in_specs = [pl.BlockSpec((nseq,) + s, lambda i, s=s: (_ctx_block(i),) + zeros(s)) for s in shapes] if prev else []
    lead = (2,) if prev else ()
    out_specs = [pl.BlockSpec((nseq,) + lead + s, lambda i, s=s: (_ctx_block(i),) + zeros(lead + s)) for s in shapes]
    out_shape = [jax.ShapeDtypeStruct((BATCH,) + lead + s, F32) for s in shapes]
    return in_specs, out_specs, out_shape


def _even_in(x, mod_l, g_pre, wp, g_qa, g_kva, wq, cos_t, sin_t, prev=None):
    row = lambda i: (i, 0)
    const = lambda i: (0, 0)
    st_in, st_out, st_shape = _state_specs([(SEQ, MLA_KV_LORA), (SEQ, MLA_ROPE)], prev)
    widths = (MLA_HEADS * LANES, LANES, LANES, M_HEADS * M_DK, M_HEADS * M_DK, M_HEADS * M_DV,
              M_HEADS * M_DV, LANES)
    dtypes = (BF16, BF16, BF16, BF16, BF16, BF16, F32, F32)
    return pl.pallas_call(
        functools.partial(_even_in_kernel, prev is not None),
        grid=(N_TOK // TM,),
        in_specs=[
            pl.BlockSpec((TM, D_MODEL), row),
            pl.BlockSpec((1, N_MOD, D_MODEL), lambda i: (_group_of_block(i, TM), 0, 0)),
            pl.BlockSpec((1, D_MODEL), const),
            pl.BlockSpec((D_MODEL, E_COLS), const),
            pl.BlockSpec((1, MLA_Q_LORA), const),
            pl.BlockSpec((1, MLA_KV_LORA), const),
            pl.BlockSpec((MLA_Q_LORA, MLA_HEADS * LANES), const),
            pl.BlockSpec((TM, LANES), lambda i: (_rope_block(i, TM), 0)),
            pl.BlockSpec((TM, LANES), lambda i: (_rope_block(i, TM), 0)),
        ] + st_in,
        out_specs=[pl.BlockSpec((TM, w), row) for w in widths] + st_out,
        out_shape=[jax.ShapeDtypeStruct((N_TOK, w), dt) for w, dt in zip(widths, dtypes)] + st_shape,
        compiler_params=_cparams("arbitrary"),
        name="even_in",
    )(x, mod_l, g_pre, wp, g_qa, g_kva, wq, cos_t, sin_t, *(prev or ()))


def _mla_kernel(t_own, n_cache, tq, *refs):
    if n_cache:
        q_ref, ckv_ref, kr_ref, cckv_ref, ckr_ref, wk_ref, wv_ref, o_ref, ks_ref, vs_ref = refs
    else:
        q_ref, ckv_ref, kr_ref, wk_ref, wv_ref, o_ref, ks_ref, vs_ref = refs
    t_keys = n_cache + t_own

    @pl.when(pl.program_id(1) == 0)
    def _expand():
        if n_cache:
            ckv16 = jnp.concatenate([cckv_ref[0].astype(BF16), ckv_ref[...]], axis=0)
            kr = jnp.concatenate([ckr_ref[0].astype(BF16), kr_ref[...]], axis=0)
        else:
            ckv16, kr = ckv_ref[...], kr_ref[...]
        for hd in range(MLA_HEADS):
            ks_ref[hd] = (jnp.dot(ckv16, wk_ref[hd], preferred_element_type=F32) + kr).astype(BF16)
        for pr in range(MLA_HEADS // 2):
            vs_ref[pr] = jnp.dot(ckv16, wv_ref[pr], preferred_element_type=F32).astype(BF16)

    scale = MLA_QK ** -0.5
    lane = lax.broadcasted_iota(jnp.int32, (tq, LANES), 1)
    for pr in range(MLA_HEADS // 2):
        outs = []
        for j in range(2):
            hd = 2 * pr + j
            qh = q_ref[:, hd * LANES:(hd + 1) * LANES]
            s = lax.dot_general(qh, ks_ref[hd], (((1,), (1,)), ((), ())), preferred_element_type=F32) * scale
            m = jnp.max(s, axis=-1, keepdims=True)
            e = jnp.exp(s - m)
            l = jnp.sum(e, axis=-1, keepdims=True)
            outs.append(jnp.dot(e.astype(BF16), vs_ref[pr], preferred_element_type=F32) / l)
        o_ref[:, pr * LANES:(pr + 1) * LANES] = jnp.where(lane < MLA_V, outs[0], outs[1]).astype(BF16)
    del t_keys


def _mla(q, ckv, kr, wk, wv, *, row0, n_batch, t_own, tq, cache=None):
    n_cache = 0 if cache is None else cache[0].shape[1]
    nq = t_own // tq
    qb0 = row0 // tq
    kb0 = row0 // t_own
    in_specs = [
        pl.BlockSpec((tq, MLA_HEADS * LANES), lambda b, j: (qb0 + b * nq + j, 0)),
        pl.BlockSpec((t_own, LANES), lambda b, j: (kb0 + b, 0)),
        pl.BlockSpec((t_own, LANES), lambda b, j: (kb0 + b, 0)),
    ]
    args = [q, ckv, kr]
    if n_cache:
        in_specs += [pl.BlockSpec((1, n_cache, LANES), lambda b, j: (b, 0, 0))] * 2
        args += list(cache)
    in_specs += [
        pl.BlockSpec((MLA_HEADS, LANES, LANES), lambda b, j: (0, 0, 0)),
        pl.BlockSpec((MLA_HEADS // 2, LANES, LANES), lambda b, j: (0, 0, 0)),
    ]
    args += [wk, wv]
    t_keys = n_cache + t_own
    return pl.pallas_call(
        functools.partial(_mla_kernel, t_own, n_cache, tq),
        grid=(n_batch, nq),
        in_specs=in_specs,
        out_specs=pl.BlockSpec((tq, MLA_HEADS * MLA_V), lambda b, j: (b * nq + j, 0)),
        out_shape=jax.ShapeDtypeStruct((n_batch * t_own, MLA_HEADS * MLA_V), BF16),
        scratch_shapes=[pltpu.VMEM((MLA_HEADS, t_keys, LANES), BF16),
                        pltpu.VMEM((MLA_HEADS // 2, t_keys, LANES), BF16)],
        compiler_params=_cparams("arbitrary", "arbitrary"),
        name="mla_lat" if n_cache else "mla_ctx",
    )(*args)


def _log_sigmoid(x):
    return jnp.minimum(x, 0.0) - jnp.log1p(jnp.exp(-jnp.abs(x)))


def _mlstm_chunk(rev, r0, st, hd, allowed, cum, e_rows, e_cols, mq_ref, mk_ref, mv_ref, ct_ref, ns_ref, ms_ref,
                 h_ref):
    L = M_CHUNK
    nt = (((1,), (1,)), ((), ()))
    gi = (2 * M_HEADS if rev else 0) + hd
    gf = gi + M_HEADS
    edge = 0 if rev else L - 1
    rows = pl.ds(r0, L)
    q16 = mq_ref[rows, hd * M_DK:(hd + 1) * M_DK]
    k16 = mk_ref[rows, hd * M_DK:(hd + 1) * M_DK] * (M_DK ** -0.5)
    v_t = mv_ref[rows, hd * M_DV:(hd + 1) * M_DV].astype(F32).T
    ct_prev = ct_ref[st, :, 0:M_DK]
    n_prev = ns_ref[st, 0:1, 0:M_DK]
    m_prev = ms_ref[st, 0:1, 0:1]
    yield

    b_row = cum[gf:gf + 1, :]
    e_row = e_rows[gi:gi + 1, :]
    e_col = e_cols[:, gi:gi + 1]
    dmat = jnp.where(allowed, b_row + e_col, NEG)
    inter = m_prev + b_row
    mt = jnp.maximum(inter, jnp.max(dmat, axis=0, keepdims=True))
    w_inter = jnp.exp(inter - mt)
    qk = lax.dot_general(k16, q16, nt, preferred_element_type=F32)
    a = qk * jnp.exp(dmat - mt)
    yield
    n8 = jnp.broadcast_to(n_prev, (8, M_DK)).astype(BF16)
    nq = lax.dot_general(n8, q16, nt, preferred_element_type=F32)[0:1]
    num = w_inter * lax.dot_general(ct_prev.astype(BF16), q16, nt, preferred_element_type=F32) \
        + jnp.dot(v_t.astype(BF16), a.astype(BF16), preferred_element_type=F32)
    den = w_inter * nq + jnp.sum(a, axis=0, keepdims=True)
    h_t = num * (1.0 / jnp.maximum(jnp.abs(den), jnp.exp(-mt)))
    h_ref[rows, hd * M_DV:(hd + 1) * M_DV] = h_t.T
    yield

    m_new = mt[:, edge:edge + 1]
    b_last = b_row[:, edge:edge + 1]
    w_s = jnp.exp(b_last + e_row - m_new)
    w_c = jnp.exp(m_prev + b_last - m_new)
    ct_ref[st, :, 0:M_DK] = w_c * ct_prev + jnp.dot((v_t * w_s).astype(BF16), k16, preferred_element_type=F32)
    w8 = jnp.broadcast_to(w_s, (8, L)).astype(BF16)
    ns_ref[st, 0:1, 0:M_DK] = w_c * n_prev + jnp.dot(w8, k16, preferred_element_type=F32)[0:1]
    ms_ref[st, 0:1, :] = jnp.broadcast_to(m_new, (1, LANES))


def _mlstm_kernel(t_len, has_state, has_prev, nb, *refs):
    n_st = 2 * M_HEADS
    prev_refs = ()
    if has_prev:
        prev_refs, refs = refs[7:10], refs[:7] + refs[10:]
    if has_state:
        (mq_ref, mk_ref, mv_ref, mo_ref, mg_ref, gb_ref, gout_ref, c0_ref, n0_ref, m0_ref,
         out_ref, c_ref, n_ref, m_ref, hf_ref, hb_ref, ct_ref, ns_ref, ms_ref) = refs
        pad = jnp.zeros((LANES - M_DK, M_DV), F32)
        for bb in range(nb):
            for st in range(n_st):
                ct_ref[bb * n_st + st] = jnp.concatenate([c0_ref[bb, st], pad], axis=0).T
                ns_ref[bb * n_st + st, 0:1, 0:M_DK] = n0_ref[bb, st:st + 1, :]
                ms_ref[bb * n_st + st, 0:1, :] = m0_ref[bb, st:st + 1, :]
    else:
        (mq_ref, mk_ref, mv_ref, mo_ref, mg_ref, gb_ref, gout_ref,
         out_ref, c_ref, n_ref, m_ref, hf_ref, hb_ref, ct_ref, ns_ref, ms_ref) = refs
        ct_ref[...] = jnp.zeros_like(ct_ref)
        ns_ref[...] = jnp.zeros_like(ns_ref)
        ms_ref[...] = jnp.zeros_like(ms_ref)
    L = M_CHUNK
    nc = t_len // L
    r_idx = lax.broadcasted_iota(jnp.int32, (L, L), 0)
    c_idx = lax.broadcasted_iota(jnp.int32, (L, L), 1)
    lower = c_idx <= r_idx
    upper = c_idx >= r_idx
    lane = lax.broadcasted_iota(jnp.int32, (L, LANES), 1)
    is_forget = (lane % (2 * M_HEADS)) >= M_HEADS
    n_gate = 4 * M_HEADS
    zeros_below = jnp.zeros((L - n_gate, L), F32)

    def gate_sums(r0, rev):
        g = mg_ref[pl.ds(r0, L), :] + gb_ref[...]
        g = jnp.where(is_forget, _log_sigmoid(g), g)
        rows = g.T[0:n_gate, :]
        tri = (lower if rev else upper).astype(BF16)
        hi = rows.astype(BF16)
        rest = rows - hi.astype(F32)
        mid = rest.astype(BF16)
        lo = (rest - mid.astype(F32)).astype(BF16)
        cum = (jnp.dot(hi, tri, preferred_element_type=F32) + jnp.dot(mid, tri, preferred_element_type=F32)
               + jnp.dot(lo, tri, preferred_element_type=F32))
        e_rows = rows - pltpu.roll(cum, n_gate - M_HEADS, 0)
        e_cols = jnp.concatenate([e_rows, zeros_below], axis=0).T
        return cum, e_rows, e_cols

    def step(c, carry):
        chunks = []
        for bb in range(nb):
            for rev in (False, True):
                r0 = pl.multiple_of(bb * t_len + (nc - 1 - c if rev else c) * L, L)
                cum, e_rows, e_cols = gate_sums(r0, rev)
                allowed = lower if rev else upper
                for hd in range(M_HEADS):
                    st = bb * n_st + (M_HEADS if rev else 0) + hd
                    chunks.append(_mlstm_chunk(rev, r0, st, hd, allowed, cum, e_rows, e_cols, mq_ref, mk_ref, mv_ref,
                                               ct_ref, ns_ref, ms_ref, hb_ref if rev else hf_ref))
        _run_staged(chunks)
        return carry

    lax.fori_loop(0, nc, step, 0)
    for bb in range(nb):
        if has_prev:
            for o_ref, p_ref in zip((c_ref, n_ref, m_ref), prev_refs):
                o_ref[bb, 0] = p_ref[bb]
        own = (bb, 1) if has_prev else (bb,)
        for st in range(n_st):
            c_ref[own + (st,)] = ct_ref[bb * n_st + st].T[0:M_DK, :]
            n_ref[own + (slice(st, st + 1), slice(None))] = ns_ref[bb * n_st + st, 0:1, 0:M_DK]
            m_ref[own + (slice(st, st + 1), slice(None))] = ms_ref[bb * n_st + st, 0:1, :]
    for hd in range(M_HEADS):
        sl = slice(hd * M_DV, (hd + 1) * M_DV)
        hm = hf_ref[:, sl] + hb_ref[:, sl]
        hm = hm * lax.rsqrt(jnp.mean(hm * hm, axis=-1, keepdims=True) + EPS) * gout_ref[:, sl]
        out_ref[:, sl] = (jax.nn.sigmoid(mo_ref[:, sl]) * hm).astype(BF16)


def _mlstm(mq, mk, mv, mo, mg, gate_b, g_out, *, row0, n_batch, t_len, state=None, prev=None):
    nb = M_SEQ_PER_STEP
    rows = nb * t_len
    kb0 = row0 // rows
    row = lambda b: (kb0 + b, 0)
    const = lambda b: (0, 0)
    n_st = 2 * M_HEADS
    in_specs = [
        pl.BlockSpec((rows, M_HEADS * M_DK), row),
        pl.BlockSpec((rows, M_HEADS * M_DK), row),
        pl.BlockSpec((rows, M_HEADS * M_DV), row),
        pl.BlockSpec((rows, M_HEADS * M_DV), row),
        pl.BlockSpec((rows, LANES), row),
        pl.BlockSpec((1, LANES), const),
        pl.BlockSpec((1, M_HEADS * M_DV), const),
    ]
    args = [mq, mk, mv, mo, mg, gate_b, g_out]
    state_shapes = [(n_st, M_DK, M_DV), (n_st, M_DK), (n_st, LANES)]
    zeros = lambda s: (0,) * len(s)
    state_specs = [pl.BlockSpec((nb,) + s, lambda b, s=s: (b,) + zeros(s)) for s in state_shapes]
    lead = (2,) if prev is not None else ()
    out_state_specs = [pl.BlockSpec((nb,) + lead + s, lambda b, s=s: (b,) + zeros(lead + s)) for s in state_shapes]
    if prev is not None:
        in_specs += state_specs
        args += list(prev)
    if state is not None:
        in_specs += state_specs
        args += list(state)
    return pl.pallas_call(
        functools.partial(_mlstm_kernel, t_len, state is not None, prev is not None, nb),
        grid=(n_batch // nb,),
        in_specs=in_specs,
        out_specs=[pl.BlockSpec((rows, M_HEADS * M_DV), lambda b: (b, 0))] + out_state_specs,
        out_shape=[jax.ShapeDtypeStruct((n_batch * t_len, M_HEADS * M_DV), BF16)]
        + [jax.ShapeDtypeStruct((n_batch,) + lead + s, F32) for s in state_shapes],
        scratch_shapes=[pltpu.VMEM((rows, M_HEADS * M_DV), F32), pltpu.VMEM((rows, M_HEADS * M_DV), F32),
                        pltpu.VMEM((nb * n_st, M_DV, LANES), F32),
                        pltpu.VMEM((nb * n_st, 8, LANES), F32), pltpu.VMEM((nb * n_st, 8, LANES), F32)],
        compiler_params=_cparams("arbitrary"),
        name="mlstm_lat" if state is not None else "mlstm_ctx",
    )(*args)


def _odd_in_kernel(has_prev, x_ref, mod_ref, g_ref, wp_ref, cos_ref, sin_ref, *refs):
    prev_refs, refs = (refs[:2], refs[2:]) if has_prev else ((), refs)
    q_ref, k_ref, v_ref, k_ctx_ref, v_ctx_ref = refs
    h = _modulated(x_ref[...], g_ref[...], mod_ref, 1).astype(BF16)
    p = jnp.dot(h, wp_ref[...], preferred_element_type=F32)
    cos = cos_ref[...]
    sin = sin_ref[...]
    half = S_HEAD_DIM // 4
    n_q = S_HEADS * S_HEAD_DIM
    n_k = S_KV_HEADS * LANES
    for t in range(n_q // LANES):
        sl = slice(t * LANES, (t + 1) * LANES)
        q_ref[:, sl] = _rope_tile(p[:, sl], cos, sin, half).astype(BF16)
    k_tiles = [_rope_tile(p[:, n_q + g * LANES:n_q + (g + 1) * LANES], cos, sin, half) for g in range(S_KV_HEADS)]
    v_tiles = [p[:, n_q + n_k + g * LANES:n_q + n_k + (g + 1) * LANES] for g in range(S_KV_HEADS)]
    for g in range(S_KV_HEADS):
        k_ref[:, g * LANES:(g + 1) * LANES] = k_tiles[g].astype(BF16)
        v_ref[:, g * LANES:(g + 1) * LANES] = v_tiles[g].astype(BF16)

    @pl.when(pl.program_id(0) < N_CTX // TM)
    def _ctx_state():
        for bb in range(TM // SEQ):
            rows = slice(bb * SEQ, (bb + 1) * SEQ)
            if has_prev:
                k_ctx_ref[bb, 0] = prev_refs[0][bb]
                v_ctx_ref[bb, 0] = prev_refs[1][bb]
            for g in range(S_KV_HEADS):
                slot = (bb, 1, g) if has_prev else (bb, g)
                k_ctx_ref[slot] = k_tiles[g][rows, 0:S_HEAD_DIM]
                v_ctx_ref[slot] = v_tiles[g][rows, 0:S_HEAD_DIM]


def _odd_in(x, mod_l, g_pre, wp, cos_t, sin_t, prev=None):
    row = lambda i: (i, 0)
    const = lambda i: (0, 0)
    st_in, st_out, st_shape = _state_specs([(S_KV_HEADS, SEQ, S_HEAD_DIM)] * 2, prev)
    widths = (S_HEADS * S_HEAD_DIM, S_KV_HEADS * LANES, S_KV_HEADS * LANES)
    return pl.pallas_call(
        functools.partial(_odd_in_kernel, prev is not None),
        grid=(N_TOK // TM,),
        in_specs=[
            pl.BlockSpec((TM, D_MODEL), row),
            pl.BlockSpec((1, N_MOD, D_MODEL), lambda i: (_group_of_block(i, TM), 0, 0)),
            pl.BlockSpec((1, D_MODEL), const),
            pl.BlockSpec((D_MODEL, sum(widths)), const),
            pl.BlockSpec((TM, LANES), lambda i: (_rope_block(i, TM), 0)),
            pl.BlockSpec((TM, LANES), lambda i: (_rope_block(i, TM), 0)),
        ] + st_in,
        out_specs=[pl.BlockSpec((TM, w), row) for w in widths] + st_out,
        out_shape=[jax.ShapeDtypeStruct((N_TOK, w), BF16) for w in widths] + st_shape,
        compiler_params=_cparams("arbitrary"),
        name="odd_in",
    )(x, mod_l, g_pre, wp, cos_t, sin_t, *(prev or ()))


def _gqa_heads(q_ref, keys, vals, sink_ref, mask, o_ref, tq):
    scale = S_HEAD_DIM ** -0.5
    lane = lax.broadcasted_iota(jnp.int32, (tq, LANES), 1)
    low = lane < S_HEAD_DIM

    def kv_group(g):
        k2, v2 = keys(g), vals(g)
        col = g * S_GROUP * S_HEAD_DIM
        tiles = [q_ref[:, col:col + LANES].astype(F32) * scale, q_ref[:, col + LANES:col + 2 * LANES].astype(F32) * scale]
        q4 = jnp.concatenate([jnp.where(low, tiles[0], 0.0), jnp.where(low, 0.0, tiles[0]),
                              jnp.where(low, tiles[1], 0.0), jnp.where(low, 0.0, tiles[1])], axis=0).astype(BF16)
        yield
        s = lax.dot_general(k2, q4, (((1,), (1,)), ((), ())), preferred_element_type=F32)
        if mask is not None:
            s = jnp.where(mask, s, NEG)
        yield
        sk = jnp.concatenate([jnp.full((1, tq), sink_ref[g * S_GROUP + j], F32) for j in range(S_GROUP)], axis=1)
        m = jnp.maximum(jnp.max(s, axis=0, keepdims=True), sk)
        e = jnp.exp(s - m)
        yield
        l = jnp.sum(e, axis=0, keepdims=True) + jnp.exp(sk - m)
        o_t = lax.dot_general(v2, e.astype(BF16), (((0,), (0,)), ((), ())), preferred_element_type=F32)
        yield
        o4 = (o_t * (1.0 / l)).T
        o_ref[:, col:col + LANES] = jnp.where(low, o4[0:tq], o4[tq:2 * tq]).astype(BF16)
        o_ref[:, col + LANES:col + 2 * LANES] = jnp.where(low, o4[2 * tq:3 * tq], o4[3 * tq:4 * tq]).astype(BF16)

    _run_staged(kv_group(g) for g in range(S_KV_HEADS))


def _gqa_ctx_kernel(sink_ref, q_ref, k_ref, v_ref, o_ref):
    keys = lambda g: k_ref[:, g * LANES:(g + 1) * LANES]
    vals = lambda g: v_ref[:, g * LANES:(g + 1) * LANES]
    _gqa_heads(q_ref, keys, vals, sink_ref, None, o_ref, SEQ)


def _gqa_ctx(q, kd, vd, sink):
    row = lambda b: (b, 0)
    return pl.pallas_call(
        _gqa_ctx_kernel,
        grid=(BATCH,),
        in_specs=[
            pl.BlockSpec(memory_space=pltpu.SMEM),
            pl.BlockSpec((SEQ, S_HEADS * S_HEAD_DIM), row),
            pl.BlockSpec((SEQ, S_KV_HEADS * LANES), row),
            pl.BlockSpec((SEQ, S_KV_HEADS * LANES), row),
        ],
        out_specs=pl.BlockSpec((SEQ, S_HEADS * S_HEAD_DIM), row),
        out_shape=jax.ShapeDtypeStruct((N_CTX, S_HEADS * S_HEAD_DIM), BF16),
        compiler_params=_cparams("arbitrary"),
        name="gqa_ctx",
    )(sink, q, kd, vd)


def _gqa_lat_kernel(sink_ref, q_ref, k_ref, v_ref, ck_ref, cv_ref, o_ref):
    n = pl.program_id(1)
    nb = DEC_SEQ // S_BLOCK
    B = S_BLOCK
    prev0 = pl.multiple_of(jnp.maximum(n - 1, 0) * B, B)
    cur0 = pl.multiple_of(n * B, B)
    next0 = pl.multiple_of(jnp.minimum(n + 1, nb - 1) * B, B)
    t_idx = lax.broadcasted_iota(jnp.int32, (PAST_LEN + 3 * B, S_GROUP * B), 1) % B
    c_idx = lax.broadcasted_iota(jnp.int32, (PAST_LEN + 3 * B, S_GROUP * B), 0)
    s_prev = c_idx - PAST_LEN
    s_next = c_idx - (PAST_LEN + 2 * B)
    in_prev = (c_idx >= PAST_LEN) & (c_idx < PAST_LEN + B)
    in_next = c_idx >= PAST_LEN + 2 * B
    far = jnp.int32(4 * B)
    bad_prev = in_prev & (s_prev < t_idx + jnp.where(n == 0, far, 0))
    bad_next = in_next & (s_next + jnp.where(n == nb - 1, far, 0) > t_idx)
    mask = jnp.logical_not(bad_prev | bad_next)

    def gather(ref, cache_ref, g):
        sl = slice(g * LANES, (g + 1) * LANES)
        return jnp.concatenate([cache_ref[0, g], ref[pl.ds(prev0, B), sl], ref[pl.ds(cur0, B), sl],
                                ref[pl.ds(next0, B), sl]], axis=0)

    keys = lambda g: gather(k_ref, ck_ref, g)
    vals = lambda g: gather(v_ref, cv_ref, g)
    _gqa_heads(q_ref, keys, vals, sink_ref, mask, o_ref, B)


def _gqa_lat(q, kd, vd, cache_k2, cache_v2, sink):
    nb = DEC_SEQ // S_BLOCK
    qb0 = N_CTX // S_BLOCK
    kb0 = N_CTX // DEC_SEQ
    return pl.pallas_call(
        _gqa_lat_kernel,
        grid=(DEC_BATCH, nb),
        in_specs=[
            pl.BlockSpec(memory_space=pltpu.SMEM),
            pl.BlockSpec((S_BLOCK, S_HEADS * S_HEAD_DIM), lambda b, n: (qb0 + b * nb + n, 0)),
            pl.BlockSpec((DEC_SEQ, S_KV_HEADS * LANES), lambda b, n: (kb0 + b, 0)),
            pl.BlockSpec((DEC_SEQ, S_KV_HEADS * LANES), lambda b, n: (kb0 + b, 0)),
            pl.BlockSpec((1, S_KV_HEADS, PAST_LEN, LANES), lambda b, n: (b, 0, 0, 0)),
            pl.BlockSpec((1, S_KV_HEADS, PAST_LEN, LANES), lambda b, n: (b, 0, 0, 0)),
        ],
        out_specs=pl.BlockSpec((S_BLOCK, S_HEADS * S_HEAD_DIM), lambda b, n: (b * nb + n, 0)),
        out_shape=jax.ShapeDtypeStruct((N_LAT, S_HEADS * S_HEAD_DIM), BF16),
        compiler_params=_cparams("arbitrary", "arbitrary"),
        name="gqa_lat",
    )(sink, q, kd, vd, cache_k2, cache_v2)


def _rope_tables(rot_dim, lane_off, reps):
    nf = rot_dim // 4
    inv = np.float32(ROPE_BASE) ** (-np.arange(nf, dtype=np.float32) / np.float32(nf))
    pos = np.arange(DEC_SEQ)
    ang_r = (pos // GRID_W).astype(np.float32)[:, None] * inv
    ang_c = (pos % GRID_W).astype(np.float32)[:, None] * inv
    cos_g = np.concatenate([np.cos(ang_r), np.cos(ang_r), np.cos(ang_c), np.cos(ang_c)], axis=1)
    sin_g = np.concatenate([-np.sin(ang_r), np.sin(ang_r), -np.sin(ang_c), np.sin(ang_c)], axis=1)
    cos_t = np.ones((TM + DEC_SEQ, LANES), np.float32)
    sin_t = np.zeros((TM + DEC_SEQ, LANES), np.float32)
    for r in range(reps):
        lo = lane_off + r * rot_dim
        cos_t[TM:, lo:lo + rot_dim] = cos_g
        sin_t[TM:, lo:lo + rot_dim] = sin_g
    return jnp.asarray(cos_t, F32), jnp.asarray(sin_t, F32)


def _even_weights(w_in, w_qb, w_kvb, gate_b):
    z = lambda n: jnp.zeros((D_MODEL, n), F32)
    idx = np.cumsum([MLA_Q_LORA, MLA_KV_LORA, MLA_ROPE, M_HEADS * M_DK, M_HEADS * M_DK, M_HEADS * M_DV,
                     M_HEADS * M_DV])
    q_a, kv_a, k_rope, mq, mk, mv, mo, mg = jnp.split(w_in, idx, axis=1)
    wp = jnp.concatenate([q_a, kv_a, z(KR_LANE), k_rope, z(LANES - KR_LANE - MLA_ROPE), mq, mk, mv, mo, mg,
                          z(LANES - 4 * M_HEADS)], axis=1).astype(BF16)
    wq = jnp.pad(w_qb.reshape(MLA_Q_LORA, MLA_HEADS, MLA_QK), ((0, 0), (0, 0), (0, LANES - MLA_QK)))
    wq = wq.reshape(MLA_Q_LORA, MLA_HEADS * LANES).astype(BF16)
    kvb = w_kvb.reshape(MLA_KV_LORA, MLA_HEADS, MLA_NOPE + MLA_V)
    wk = jnp.pad(kvb[:, :, :MLA_NOPE], ((0, 0), (0, 0), (0, LANES - MLA_NOPE))).transpose(1, 0, 2).astype(BF16)
    wv = kvb[:, :, MLA_NOPE:].reshape(MLA_KV_LORA, MLA_HEADS // 2, 2 * MLA_V).transpose(1, 0, 2).astype(BF16)
    gb = jnp.pad(gate_b, (0, LANES - 4 * M_HEADS)).reshape(1, LANES)
    return wp, wq, wk, wv, gb


def _dup_heads(w):
    w3 = w.reshape(D_MODEL, S_KV_HEADS, S_HEAD_DIM)
    return jnp.concatenate([w3, w3], axis=-1).reshape(D_MODEL, S_KV_HEADS * LANES)


def _odd_weights(w_in):
    n_q = S_HEADS * S_HEAD_DIM
    n_kv = S_KV_HEADS * S_HEAD_DIM
    return jnp.concatenate([w_in[:, :n_q], _dup_heads(w_in[:, n_q:n_q + n_kv]), _dup_heads(w_in[:, n_q + n_kv:])],
                           axis=1).astype(BF16)


def kernel(x_prompt, x_sample, cache_mla_ckv, cache_mla_krope, state_mlstm_C, state_mlstm_n, state_mlstm_m,
           cache_swa_k, cache_swa_v, c, c_ctx, ada_w, ada_b, norm_g, ffn_w_gate, ffn_w_up, ffn_w_down,
           even_w_in, mla_g_qa, mla_g_kva, mla_w_qb, mla_w_kvb, mlstm_gate_b, mlstm_g_out, even_w_out,
           odd_w_in, swa_sink, odd_w_out):
    xs = (x_prompt.reshape(N_CTX, D_MODEL), x_sample.reshape(N_LAT, D_MODEL))
    cond = jnp.concatenate([c_ctx[None], c, jnp.zeros((COND_ROWS - N_GROUPS, D_MODEL), F32)], axis=0)
    mod = _adaln(cond, ada_w, ada_b)
    cos_e, sin_e = _rope_tables(MLA_ROPE, KR_LANE, 1)
    cos_o, sin_o = _rope_tables(S_HEAD_DIM, 0, LANES // S_HEAD_DIM)
    w32 = (ffn_w_gate, ffn_w_up, ffn_w_down)
    w16 = tuple(w[0, 0].astype(BF16) for w in w32)

    assert N_EVEN == 2 and N_ODD == 2
    even_state = mlstm_state = odd_state = None
    for l in range(DEPTH):
        mod_l = mod[l]
        g = norm_g[l]
        i = l // 2
        (x,), w16 = _ffn(xs, mod_l, g[0:2], w16, 0, next_w=(w32, l, 1))
        if l % 2 == 0:
            wp, wq, wk, wv, gb = _even_weights(even_w_in[i], mla_w_qb[i], mla_w_kvb[i], mlstm_gate_b[i])
            q, ckv, kr, mq, mk, mv, mo, mg, *even_state = _even_in(
                x, mod_l, g[2:3], wp, mla_g_qa[i].reshape(1, -1), mla_g_kva[i].reshape(1, -1), wq, cos_e, sin_e,
                prev=even_state)
            cache_kr = jnp.pad(cache_mla_krope[:, i], ((0, 0), (0, 0), (KR_LANE, LANES - KR_LANE - MLA_ROPE)))
            att_c = _mla(q, ckv, kr, wk, wv, row0=0, n_batch=BATCH, t_own=SEQ, tq=SEQ)
            att_l = _mla(q, ckv, kr, wk, wv, row0=N_CTX, n_batch=DEC_BATCH, t_own=DEC_SEQ, tq=512,
                         cache=(cache_mla_ckv[:, i], cache_kr))
            g_out = mlstm_g_out[i].reshape(1, -1)
            mo_c, *mlstm_state = _mlstm(mq, mk, mv, mo, mg, gb, g_out, row0=0, n_batch=BATCH, t_len=SEQ,
                                        prev=mlstm_state)
            n_st = 2 * M_HEADS
            state = (state_mlstm_C[:, i].reshape(DEC_BATCH, n_st, M_DK, M_DV),
                     state_mlstm_n[:, i].reshape(DEC_BATCH, n_st, M_DK),
                     jnp.broadcast_to(state_mlstm_m[:, i].reshape(DEC_BATCH, n_st, 1), (DEC_BATCH, n_st, LANES)))
            mo_l, _, _, _ = _mlstm(mq, mk, mv, mo, mg, gb, g_out, row0=N_CTX, n_batch=DEC_BATCH, t_len=DEC_SEQ,
                                   state=state)
            w_out = even_w_out[i].astype(BF16)
            n_att = MLA_HEADS * MLA_V
            acts, act_w = [(att_c, att_l), (mo_c, mo_l)], [w_out[:n_att], w_out[n_att:]]
        else:
            q, kd, vd, *odd_state = _odd_in(x, mod_l, g[2:3], _odd_weights(odd_w_in[i]), cos_o, sin_o,
                                            prev=odd_state)
            dup = lambda a: jnp.concatenate([a, a], axis=-1).astype(BF16)
            o_c = _gqa_ctx(q, kd, vd, swa_sink[i])
            o_l = _gqa_lat(q, kd, vd, dup(cache_swa_k[:, i]), dup(cache_swa_v[:, i]), swa_sink[i])
            acts, act_w = [(o_c, o_l)], [odd_w_out[i].astype(BF16)]
        last = l == DEPTH - 1
        xs, w16 = _ffn((x,), mod_l, g[3:6], w16, 1, acts=acts, act_w=act_w, dual_out=last,
                       next_w=None if last else (w32, l + 1, 0))

    return (xs[0].reshape(BATCH, SEQ, D_MODEL), xs[1].reshape(DEC_BATCH, DEC_SEQ, D_MODEL),
            even_state[0], even_state[1],
            mlstm_state[0].reshape(BATCH, N_EVEN, 2, M_HEADS, M_DK, M_DV),
            mlstm_state[1].reshape(BATCH, N_EVEN, 2, M_HEADS, M_DK),
            mlstm_state[2][..., 0].reshape(BATCH, N_EVEN, 2, M_HEADS),
            odd_state[0], odd_state[1])
```

```python
import functools

import jax
import jax.numpy as jnp
import numpy as np
from jax import lax
from jax.experimental import pallas as pl
from jax.experimental.pallas import tpu as pltpu

F32 = jnp.float32
BF16 = jnp.bfloat16

D_MODEL = 1024
BATCH = 32
SEQ = 256
DEPTH = 4
N_EVEN = (DEPTH + 1) // 2
N_ODD = DEPTH // 2
DEC_BATCH = 4
DEC_SEQ = 1024
PAST_LEN = 512
GRID_W = 64
N_MOD = 9
D_FF = 2816
EPS = 1e-6
ROPE_BASE = 10000.0
NEG = -1e30
MLA_HEADS = 8
MLA_NOPE = 64
MLA_ROPE = 32
MLA_QK = MLA_NOPE + MLA_ROPE
MLA_V = 64
MLA_Q_LORA = 256
MLA_KV_LORA = 128
M_HEADS = 4
M_DK = 64
M_DV = 128
M_CHUNK = 128
S_HEADS = 16
S_KV_HEADS = 4
S_GROUP = S_HEADS // S_KV_HEADS
S_HEAD_DIM = 64
S_WINDOW = 128
S_BLOCK = 128

N_CTX = BATCH * SEQ
N_LAT = DEC_BATCH * DEC_SEQ
N_TOK = N_CTX + N_LAT
N_GROUPS = 1 + DEC_BATCH
COND_ROWS = 8

LANES = 128
FF_CHUNK = 256
N_FF_CHUNKS = D_FF // FF_CHUNK
assert N_FF_CHUNKS * FF_CHUNK == D_FF

TM = 512
ADA_TN = 1536
M_SEQ_PER_STEP = 2
FFN_PARTS = 2
IN_PARTS = 2
VMEM_LIMIT = 52 * 1024 * 1024

E_QA, E_KVA, E_KR, E_MQ, E_MK, E_MV, E_MO, E_MG, E_COLS = 0, 256, 384, 512, 768, 1024, 1536, 2048, 2176
KR_LANE = MLA_NOPE


def _cparams(*sem):
    return pltpu.CompilerParams(dimension_semantics=sem, vmem_limit_bytes=VMEM_LIMIT)


def _group_of_block(i, tm):
    n_ctx = N_CTX // tm
    per_lat = DEC_SEQ // tm
    return jnp.where(i < n_ctx, 0, 1 + (i - n_ctx) // per_lat)


def _rope_block(i, tm):
    n_ctx = N_CTX // tm
    per_lat = DEC_SEQ // tm
    return jnp.where(i < n_ctx, 0, 1 + (i - n_ctx) % per_lat)


def _rms(x, g):
    return x * lax.rsqrt(jnp.mean(x * x, axis=-1, keepdims=True) + EPS) * g


def _modulated(x, g_pre, mod_ref, sub):
    shift = mod_ref[0, 3 * sub:3 * sub + 1, :]
    scale = mod_ref[0, 3 * sub + 1:3 * sub + 2, :]
    return _rms(x, g_pre) * (1.0 + scale) + shift


def _silu(x):
    return x * jax.nn.sigmoid(x)


def _swap_halves(x, half):
    lane = lax.broadcasted_iota(jnp.int32, x.shape, 1)
    up = pltpu.roll(x, LANES - half, 1)
    down = pltpu.roll(x, half, 1)
    return jnp.where(lane % (2 * half) < half, up, down)


def _rope_tile(x, cos, sin, half):
    return x * cos + _swap_halves(x, half) * sin


def _run_staged(stages, skew=0):
    pending, live, tick, started = list(stages), [], 0, 0
    while pending or live:
        while pending and tick >= started * skew:
            live.append(pending.pop(0))
            started += 1
        live = [s for s in live if next(s, True) is None]
        tick += 1


def _adaln_kernel(cond_ref, w_ref, b_ref, o_ref):
    s = _silu(cond_ref[...]).astype(BF16)
    o_ref[0] = jnp.dot(s, w_ref[0].astype(BF16), preferred_element_type=F32) + b_ref[0]


def _adaln(cond, ada_w, ada_b):
    n_out = N_MOD * D_MODEL
    out = pl.pallas_call(
        _adaln_kernel,
        grid=(DEPTH, n_out // ADA_TN),
        in_specs=[
            pl.BlockSpec((COND_ROWS, D_MODEL), lambda l, j: (0, 0)),
            pl.BlockSpec((1, D_MODEL, ADA_TN), lambda l, j: (l, 0, j)),
            pl.BlockSpec((1, 1, ADA_TN), lambda l, j: (l, 0, j)),
        ],
        out_specs=pl.BlockSpec((1, COND_ROWS, ADA_TN), lambda l, j: (l, 0, j)),
        out_shape=jax.ShapeDtypeStruct((DEPTH, COND_ROWS, n_out), F32),
        compiler_params=_cparams("arbitrary", "arbitrary"),
        name="adaln",
    )(cond, ada_w, ada_b.reshape(DEPTH, 1, n_out))
    return out.reshape(DEPTH, COND_ROWS, N_MOD, D_MODEL)


def _ctx_block(i):
    return jnp.minimum(i, N_CTX // TM - 1)


def _lat_block(i):
    return jnp.maximum(i - N_CTX // TM, 0)


def _dual_specs(width):
    return [pl.BlockSpec((TM, width), lambda i: (_ctx_block(i), 0)),
            pl.BlockSpec((TM, width), lambda i: (_lat_block(i), 0))]


def _ffn_kernel(sub, dual_in, dual_out, n_act, convert, *refs):
    it = iter(refs)
    take = lambda n: [next(it) for _ in range(n)]
    a_refs = take(2 * n_act)
    w_refs = take(n_act)
    x_refs = take(2 if dual_in else 1)
    mod_ref, g_ref, wg_ref, wu_ref, wd_ref = take(5)
    nxt_refs = take(3) if convert else []
    o_refs = take(2 if dual_out else 1)
    nxt_out_refs = take(3) if convert else []
    h_ref, xs_ref = take(2)
    step = pl.program_id(0)
    is_ctx = step < N_CTX // TM
    g0 = 1 if n_act else 0

    gate_row = mod_ref[0, 3 * sub + 2:3 * sub + 3, :]
    results = {}

    def rows_of(part):
        rs = slice(part * (TM // FFN_PARTS), (part + 1) * (TM // FFN_PARTS))
        x = jnp.where(is_ctx, x_refs[0][rs, :], x_refs[1][rs, :]) if dual_in else x_refs[0][rs, :]
        if n_act:
            y = None
            for k in range(n_act):
                a = jnp.where(is_ctx, a_refs[2 * k][rs, :], a_refs[2 * k + 1][rs, :])
                d = jnp.dot(a, w_refs[k][...], preferred_element_type=F32)
                y = d if y is None else y + d
            x = x + mod_ref[0, 5:6, :] * _rms(y, g_ref[0:1, :])
        xs_ref[rs, :] = x
        h = _modulated(x, g_ref[g0:g0 + 1, :], mod_ref, sub).astype(BF16)
        yield
        for c in range(N_FF_CHUNKS):
            sl = slice(c * FF_CHUNK, (c + 1) * FF_CHUNK)
            gate = jnp.dot(h, wg_ref[:, sl], preferred_element_type=F32)
            up = jnp.dot(h, wu_ref[:, sl], preferred_element_type=F32)
            h_ref[rs, sl] = (_silu(gate) * up).astype(BF16)
            yield
        y = jnp.dot(h_ref[rs, :], wd_ref[...], preferred_element_type=F32)
        yield
        results[part] = xs_ref[rs, :] + 0.5 * gate_row * _rms(y, g_ref[g0 + 1:g0 + 2, :])

    _run_staged(rows_of(part) for part in range(FFN_PARTS))
    result = jnp.concatenate([results[part] for part in range(FFN_PARTS)], axis=0)

    if dual_out:
        @pl.when(is_ctx)
        def _ctx():
            o_refs[0][...] = result

        @pl.when(jnp.logical_not(is_ctx))
        def _lat():
            o_refs[1][...] = result
    else:
        o_refs[0][...] = result

    if convert:
        @pl.when(step < N_FF_CHUNKS)
        def _next_weights():
            for src_ref, dst_ref in zip(nxt_refs, nxt_out_refs):
                dst_ref[...] = src_ref[...].astype(BF16)


def _ffn(xs, mod_l, g_rows, w16, half, *, acts=(), act_w=(), dual_out=False, next_w=None):
    dual_in = len(xs) == 2
    n_act = len(acts)
    convert = next_w is not None
    sub = 2 * half
    const = lambda i: (0, 0)
    once = dict(pipeline_mode=pl.Buffered(1))
    in_specs, args = [], []
    for a_c, a_l in acts:
        in_specs += _dual_specs(a_c.shape[1])
        args += [a_c, a_l]
    in_specs += [pl.BlockSpec(w.shape, const, **once) for w in act_w]
    args += list(act_w)
    in_specs += _dual_specs(D_MODEL) if dual_in else [pl.BlockSpec((TM, D_MODEL), lambda i: (i, 0))]
    args += list(xs)
    in_specs += [
        pl.BlockSpec((1, N_MOD, D_MODEL), lambda i: (_group_of_block(i, TM), 0, 0)),
        pl.BlockSpec(g_rows.shape, const),
        pl.BlockSpec((D_MODEL, D_FF), const, **once),
        pl.BlockSpec((D_MODEL, D_FF), const, **once),
        pl.BlockSpec((D_FF, D_MODEL), const, **once),
    ]
    args += [mod_l, g_rows, *w16]
    if dual_out:
        out_specs = _dual_specs(D_MODEL)
        out_shape = [jax.ShapeDtypeStruct((N_CTX, D_MODEL), F32), jax.ShapeDtypeStruct((N_LAT, D_MODEL), F32)]
    else:
        out_specs = [pl.BlockSpec((TM, D_MODEL), lambda i: (i, 0))]
        out_shape = [jax.ShapeDtypeStruct((N_TOK, D_MODEL), F32)]
    if convert:
        (wg32, wu32, wd32), ln, hn = next_w
        chunk = lambda i: jnp.minimum(i, N_FF_CHUNKS - 1)
        in_specs += [
            pl.BlockSpec((None, None, D_MODEL, FF_CHUNK), lambda i: (ln, hn, 0, chunk(i))),
            pl.BlockSpec((None, None, D_MODEL, FF_CHUNK), lambda i: (ln, hn, 0, chunk(i))),
            pl.BlockSpec((None, None, FF_CHUNK, D_MODEL), lambda i: (ln, hn, chunk(i), 0)),
        ]
        args += [wg32, wu32, wd32]
        out_specs += [
            pl.BlockSpec((D_MODEL, FF_CHUNK), lambda i: (0, chunk(i))),
            pl.BlockSpec((D_MODEL, FF_CHUNK), lambda i: (0, chunk(i))),
            pl.BlockSpec((FF_CHUNK, D_MODEL), lambda i: (chunk(i), 0)),
        ]
        out_shape += [jax.ShapeDtypeStruct((D_MODEL, D_FF), BF16), jax.ShapeDtypeStruct((D_MODEL, D_FF), BF16),
                      jax.ShapeDtypeStruct((D_FF, D_MODEL), BF16)]
    outs = pl.pallas_call(
        functools.partial(_ffn_kernel, sub, dual_in, dual_out, n_act, convert),
        grid=(N_TOK // TM,),
        in_specs=in_specs,
        out_specs=out_specs,
        out_shape=out_shape,
        scratch_shapes=[pltpu.VMEM((TM, D_FF), BF16), pltpu.VMEM((TM, D_MODEL), F32)],
        compiler_params=_cparams("arbitrary"),
        name="mix_ffn" if n_act else "ffn",
    )(*args)
    n_stream = 2 if dual_out else 1
    stream = tuple(outs[:n_stream])
    return stream, (tuple(outs[n_stream:]) if convert else None)


def _even_in_kernel(has_prev, x_ref, mod_ref, g_ref, wp_ref, gqa_ref, gkva_ref, wq_ref, cos_ref, sin_ref, *refs):
    prev_refs, refs = (refs[:2], refs[2:]) if has_prev else ((), refs)
    q_ref, ckv_ref, kr_ref, mq_ref, mk_ref, mv_ref, mo_ref, mg_ref, ckv_ctx_ref, kr_ctx_ref = refs
    half = MLA_ROPE // 4
    assert TM // IN_PARTS == SEQ
    ctx_state = {}

    def rows_of(part):
        rs = slice(part * SEQ, (part + 1) * SEQ)
        h = _modulated(x_ref[rs, :], g_ref[...], mod_ref, 1).astype(BF16)
        p = jnp.dot(h, wp_ref[...], preferred_element_type=F32)
        cos = cos_ref[rs, :]
        sin = sin_ref[rs, :]
        yield
        qn = _rms(p[:, E_QA:E_QA + MLA_Q_LORA], gqa_ref[...]).astype(BF16)
        q = jnp.dot(qn, wq_ref[...], preferred_element_type=F32)
        ckv = _rms(p[:, E_KVA:E_KVA + MLA_KV_LORA], gkva_ref[...])
        ckv_ref[rs, :] = ckv.astype(BF16)
        kr = _rope_tile(p[:, E_KR:E_KR + LANES], cos, sin, half)
        kr_ref[rs, :] = kr.astype(BF16)
        ctx_state[part] = (ckv, kr[:, KR_LANE:KR_LANE + MLA_ROPE])
        mq_ref[rs, :] = p[:, E_MQ:E_MK].astype(BF16)
        mk_ref[rs, :] = p[:, E_MK:E_MV].astype(BF16)
        mv_ref[rs, :] = p[:, E_MV:E_MO].astype(BF16)
        mo_ref[rs, :] = p[:, E_MO:E_MG]
        mg_ref[rs, :] = p[:, E_MG:E_COLS]
        yield
        for hd in range(MLA_HEADS):
            sl = slice(hd * LANES, (hd + 1) * LANES)
            q_ref[rs, sl] = _rope_tile(q[:, sl], cos, sin, half).astype(BF16)

    _run_staged((rows_of(part) for part in range(IN_PARTS)), skew=1)

    @pl.when(pl.program_id(0) < N_CTX // TM)
    def _ctx_state():
        for bb in range(IN_PARTS):
            ckv, kr = ctx_state[bb]
            if has_prev:
                ckv_ctx_ref[bb, 0] = prev_refs[0][bb]
                kr_ctx_ref[bb, 0] = prev_refs[1][bb]
                ckv_ctx_ref[bb, 1] = ckv
                kr_ctx_ref[bb, 1] = kr
            else:
                ckv_ctx_ref[bb] = ckv
                kr_ctx_ref[bb] = kr


def _state_specs(shapes, prev):
    nseq = TM // SEQ
    zeros = lambda s: (0,) * len(s)
    in_specs = [pl.BlockSpec((nseq,) + s, lambda i, s=s: (_ctx_block(i),) + zeros(s)) for s in shapes] if prev else []
    lead = (2,) if prev else ()
    out_specs = [pl.BlockSpec((nseq,) + lead + s, lambda i, s=s: (_ctx_block(i),) + zeros(lead + s)) for s in shapes]
    out_shape = [jax.ShapeDtypeStruct((BATCH,) + lead + s, F32) for s in shapes]
    return in_specs, out_specs, out_shape


def _even_in(x, mod_l, g_pre, wp, g_qa, g_kva, wq, cos_t, sin_t, prev=None):
    row = lambda i: (i, 0)
    const = lambda i: (0, 0)
    st_in, st_out, st_shape = _state_specs([(SEQ, MLA_KV_LORA), (SEQ, MLA_ROPE)], prev)
    widths = (MLA_HEADS * LANES, LANES, LANES, M_HEADS * M_DK, M_HEADS * M_DK, M_HEADS * M_DV,
              M_HEADS * M_DV, LANES)
    dtypes = (BF16, BF16, BF16, BF16, BF16, BF16, F32, F32)
    return pl.pallas_call(
        functools.partial(_even_in_kernel, prev is not None),
        grid=(N_TOK // TM,),
        in_specs=[
            pl.BlockSpec((TM, D_MODEL), row),
            pl.BlockSpec((1, N_MOD, D_MODEL), lambda i: (_group_of_block(i, TM), 0, 0)),
            pl.BlockSpec((1, D_MODEL), const),
            pl.BlockSpec((D_MODEL, E_COLS), const),
            pl.BlockSpec((1, MLA_Q_LORA), const),
            pl.BlockSpec((1, MLA_KV_LORA), const),
            pl.BlockSpec((MLA_Q_LORA, MLA_HEADS * LANES), const),
            pl.BlockSpec((TM, LANES), lambda i: (_rope_block(i, TM), 0)),
            pl.BlockSpec((TM, LANES), lambda i: (_rope_block(i, TM), 0)),
        ] + st_in,
        out_specs=[pl.BlockSpec((TM, w), row) for w in widths] + st_out,
        out_shape=[jax.ShapeDtypeStruct((N_TOK, w), dt) for w, dt in zip(widths, dtypes)] + st_shape,
        compiler_params=_cparams("arbitrary"),
        name="even_in",
    )(x, mod_l, g_pre, wp, g_qa, g_kva, wq, cos_t, sin_t, *(prev or ()))


def _mla_kernel(t_own, n_cache, tq, *refs):
    if n_cache:
        q_ref, ckv_ref, kr_ref, cckv_ref, ckr_ref, wk_ref, wv_ref, o_ref, ks_ref, vs_ref = refs
    else:
        q_ref, ckv_ref, kr_ref, wk_ref, wv_ref, o_ref, ks_ref, vs_ref = refs
    t_keys = n_cache + t_own

    @pl.when(pl.program_id(1) == 0)
    def _expand():
        if n_cache:
            ckv16 = jnp.concatenate([cckv_ref[0].astype(BF16), ckv_ref[...]], axis=0)
            kr = jnp.concatenate([ckr_ref[0].astype(BF16), kr_ref[...]], axis=0)
        else:
            ckv16, kr = ckv_ref[...], kr_ref[...]
        for hd in range(MLA_HEADS):
            ks_ref[hd] = (jnp.dot(ckv16, wk_ref[hd], preferred_element_type=F32) + kr).astype(BF16)
        for pr in range(MLA_HEADS // 2):
            vs_ref[pr] = jnp.dot(ckv16, wv_ref[pr], preferred_element_type=F32).astype(BF16)

    scale = MLA_QK ** -0.5
    lane = lax.broadcasted_iota(jnp.int32, (tq, LANES), 1)
    for pr in range(MLA_HEADS // 2):
        outs = []
        for j in range(2):
            hd = 2 * pr + j
            qh = q_ref[:, hd * LANES:(hd + 1) * LANES]
            s = lax.dot_general(qh, ks_ref[hd], (((1,), (1,)), ((), ())), preferred_element_type=F32) * scale
            m = jnp.max(s, axis=-1, keepdims=True)
            e = jnp.exp(s - m)
            l = jnp.sum(e, axis=-1, keepdims=True)
            outs.append(jnp.dot(e.astype(BF16), vs_ref[pr], preferred_element_type=F32) / l)
        o_ref[:, pr * LANES:(pr + 1) * LANES] = jnp.where(lane < MLA_V, outs[0], outs[1]).astype(BF16)
    del t_keys


def _mla(q, ckv, kr, wk, wv, *, row0, n_batch, t_own, tq, cache=None):
    n_cache = 0 if cache is None else cache[0].shape[1]
    nq = t_own // tq
    qb0 = row0 // tq
    kb0 = row0 // t_own
    in_specs = [
        pl.BlockSpec((tq, MLA_HEADS * LANES), lambda b, j: (qb0 + b * nq + j, 0)),
        pl.BlockSpec((t_own, LANES), lambda b, j: (kb0 + b, 0)),
        pl.BlockSpec((t_own, LANES), lambda b, j: (kb0 + b, 0)),
    ]
    args = [q, ckv, kr]
    if n_cache:
        in_specs += [pl.BlockSpec((1, n_cache, LANES), lambda b, j: (b, 0, 0))] * 2
        args += list(cache)
    in_specs += [
        pl.BlockSpec((MLA_HEADS, LANES, LANES), lambda b, j: (0, 0, 0)),
        pl.BlockSpec((MLA_HEADS // 2, LANES, LANES), lambda b, j: (0, 0, 0)),
    ]
    args += [wk, wv]
    t_keys = n_cache + t_own
    return pl.pallas_call(
        functools.partial(_mla_kernel, t_own, n_cache, tq),
        grid=(n_batch, nq),
        in_specs=in_specs,
        out_specs=pl.BlockSpec((tq, MLA_HEADS * MLA_V), lambda b, j: (b * nq + j, 0)),
        out_shape=jax.ShapeDtypeStruct((n_batch * t_own, MLA_HEADS * MLA_V), BF16),
        scratch_shapes=[pltpu.VMEM((MLA_HEADS, t_keys, LANES), BF16),
                        pltpu.VMEM((MLA_HEADS // 2, t_keys, LANES), BF16)],
        compiler_params=_cparams("arbitrary", "arbitrary"),
        name="mla_lat" if n_cache else "mla_ctx",
    )(*args)


def _log_sigmoid(x):
    return jnp.minimum(x, 0.0) - jnp.log1p(jnp.exp(-jnp.abs(x)))


def _mlstm_chunk(rev, r0, st, hd, allowed, cum, e_rows, e_cols, mq_ref, mk_ref, mv_ref, ct_ref, ns_ref, ms_ref,
                 h_ref):
    L = M_CHUNK
    nt = (((1,), (1,)), ((), ()))
    gi = (2 * M_HEADS if rev else 0) + hd
    gf = gi + M_HEADS
    edge = 0 if rev else L - 1
    rows = pl.ds(r0, L)
    q16 = mq_ref[rows, hd * M_DK:(hd + 1) * M_DK]
    k16 = mk_ref[rows, hd * M_DK:(hd + 1) * M_DK] * (M_DK ** -0.5)
    v_t16 = mv_ref[rows, hd * M_DV:(hd + 1) * M_DV].T
    v_t = v_t16.astype(F32)
    ct_prev = ct_ref[st, :, 0:M_DK]
    n_prev = ns_ref[st, 0:1, 0:M_DK]
    m_prev = ms_ref[st, 0:1, 0:1]
    yield

    b_row = cum[gf:gf + 1, :]
    e_row = e_rows[gi:gi + 1, :]
    e_col = e_cols[:, gi:gi + 1]
    dmat = jnp.where(allowed, b_row + e_col, NEG)
    inter = m_prev + b_row
    mt = jnp.maximum(inter, jnp.max(dmat, axis=0, keepdims=True))
    w_inter = jnp.exp(inter - mt)
    qk = lax.dot_general(k16, q16, nt, preferred_element_type=F32)
    a = qk * jnp.exp(dmat - mt)
    yield
    n8 = jnp.broadcast_to(n_prev, (8, M_DK)).astype(BF16)
    nq = lax.dot_general(n8, q16, nt, preferred_element_type=F32)[0:1]
    num = w_inter * lax.dot_general(ct_prev.astype(BF16), q16, nt, preferred_element_type=F32) \
        + jnp.dot(v_t16, a.astype(BF16), preferred_element_type=F32)
    den = w_inter * nq + jnp.sum(a, axis=0, keepdims=True)
    h_t = num * (1.0 / jnp.maximum(jnp.abs(den), jnp.exp(-mt)))
    h_ref[rows, hd * M_DV:(hd + 1) * M_DV] = h_t.T
    yield

    m_new = mt[:, edge:edge + 1]
    b_last = b_row[:, edge:edge + 1]
    w_s = jnp.exp(b_last + e_row - m_new)
    w_c = jnp.exp(m_prev + b_last - m_new)
    ct_ref[st, :, 0:M_DK] = w_c * ct_prev + jnp.dot((v_t * w_s).astype(BF16), k16, preferred_element_type=F32)
    w8 = jnp.broadcast_to(w_s, (8, L)).astype(BF16)
    ns_ref[st, 0:1, 0:M_DK] = w_c * n_prev + jnp.dot(w8, k16, preferred_element_type=F32)[0:1]
    ms_ref[st, 0:1, :] = jnp.broadcast_to(m_new, (1, LANES))


def _mlstm_kernel(t_len, has_state, has_prev, nb, *refs):
    n_st = 2 * M_HEADS
    prev_refs = ()
    if has_prev:
        prev_refs, refs = refs[7:10], refs[:7] + refs[10:]
    if has_state:
        (mq_ref, mk_ref, mv_ref, mo_ref, mg_ref, gb_ref, gout_ref, c0_ref, n0_ref, m0_ref,
         out_ref, c_ref, n_ref, m_ref, hf_ref, hb_ref, ct_ref, ns_ref, ms_ref) = refs
        pad = jnp.zeros((LANES - M_DK, M_DV), F32)
        for bb in range(nb):
            for st in range(n_st):
                ct_ref[bb * n_st + st] = jnp.concatenate([c0_ref[bb, st], pad], axis=0).T
                ns_ref[bb * n_st + st, 0:1, 0:M_DK] = n0_ref[bb, st:st + 1, :]
                ms_ref[bb * n_st + st, 0:1, :] = m0_ref[bb, st:st + 1, :]
    else:
        (mq_ref, mk_ref, mv_ref, mo_ref, mg_ref, gb_ref, gout_ref,
         out_ref, c_ref, n_ref, m_ref, hf_ref, hb_ref, ct_ref, ns_ref, ms_ref) = refs
        ct_ref[...] = jnp.zeros_like(ct_ref)
        ns_ref[...] = jnp.zeros_like(ns_ref)
        ms_ref[...] = jnp.zeros_like(ms_ref)
    L = M_CHUNK
    nc = t_len // L
    r_idx = lax.broadcasted_iota(jnp.int32, (L, L), 0)
    c_idx = lax.broadcasted_iota(jnp.int32, (L, L), 1)
    lower = c_idx <= r_idx
    upper = c_idx >= r_idx
    lane = lax.broadcasted_iota(jnp.int32, (L, LANES), 1)
    is_forget = (lane % (2 * M_HEADS)) >= M_HEADS
    n_gate = 4 * M_HEADS
    zeros_below = jnp.zeros((L - n_gate, L), F32)

    def gate_sums(r0, rev):
        g = mg_ref[pl.ds(r0, L), :] + gb_ref[...]
        g = jnp.where(is_forget, _log_sigmoid(g), g)
        rows = g.T[0:n_gate, :]
        tri = (lower if rev else upper).astype(BF16)
        hi = rows.astype(BF16)
        rest = rows - hi.astype(F32)
        mid = rest.astype(BF16)
        lo = (rest - mid.astype(F32)).astype(BF16)
        cum = (jnp.dot(hi, tri, preferred_element_type=F32) + jnp.dot(mid, tri, preferred_element_type=F32)
               + jnp.dot(lo, tri, preferred_element_type=F32))
        e_rows = rows - pltpu.roll(cum, n_gate - M_HEADS, 0)
        e_cols = jnp.concatenate([e_rows, zeros_below], axis=0).T
        return cum, e_rows, e_cols

    def step(c, carry):
        chunks = []
        for bb in range(nb):
            for rev in (False, True):
                r0 = pl.multiple_of(bb * t_len + (nc - 1 - c if rev else c) * L, L)
                cum, e_rows, e_cols = gate_sums(r0, rev)
                allowed = lower if rev else upper
                for hd in range(M_HEADS):
                    st = bb * n_st + (M_HEADS if rev else 0) + hd
                    chunks.append(_mlstm_chunk(rev, r0, st, hd, allowed, cum, e_rows, e_cols, mq_ref, mk_ref, mv_ref,
                                               ct_ref, ns_ref, ms_ref, hb_ref if rev else hf_ref))
        _run_staged(chunks)
        return carry

    lax.fori_loop(0, nc, step, 0)
    for bb in range(nb):
        if has_prev:
            for o_ref, p_ref in zip((c_ref, n_ref, m_ref), prev_refs):
                o_ref[bb, 0] = p_ref[bb]
        own = (bb, 1) if has_prev else (bb,)
        for st in range(n_st):
            c_ref[own + (st,)] = ct_ref[bb * n_st + st].T[0:M_DK, :]
            n_ref[own + (slice(st, st + 1), slice(None))] = ns_ref[bb * n_st + st, 0:1, 0:M_DK]
            m_ref[own + (slice(st, st + 1), slice(None))] = ms_ref[bb * n_st + st, 0:1, :]
    for hd in range(M_HEADS):
        sl = slice(hd * M_DV, (hd + 1) * M_DV)
        hm = hf_ref[:, sl] + hb_ref[:, sl]
        hm = hm * lax.rsqrt(jnp.mean(hm * hm, axis=-1, keepdims=True) + EPS) * gout_ref[:, sl]
        out_ref[:, sl] = (jax.nn.sigmoid(mo_ref[:, sl]) * hm).astype(BF16)


def _mlstm(mq, mk, mv, mo, mg, gate_b, g_out, *, row0, n_batch, t_len, state=None, prev=None):
    nb = M_SEQ_PER_STEP
    rows = nb * t_len
    kb0 = row0 // rows
    row = lambda b: (kb0 + b, 0)
    const = lambda b: (0, 0)
    n_st = 2 * M_HEADS
    in_specs = [
        pl.BlockSpec((rows, M_HEADS * M_DK), row),
        pl.BlockSpec((rows, M_HEADS * M_DK), row),
        pl.BlockSpec((rows, M_HEADS * M_DV), row),
        pl.BlockSpec((rows, M_HEADS * M_DV), row),
        pl.BlockSpec((rows, LANES), row),
        pl.BlockSpec((1, LANES), const),
        pl.BlockSpec((1, M_HEADS * M_DV), const),
    ]
    args = [mq, mk, mv, mo, mg, gate_b, g_out]
    state_shapes = [(n_st, M_DK, M_DV), (n_st, M_DK), (n_st, LANES)]
    zeros = lambda s: (0,) * len(s)
    state_specs = [pl.BlockSpec((nb,) + s, lambda b, s=s: (b,) + zeros(s)) for s in state_shapes]
    lead = (2,) if prev is not None else ()
    out_state_specs = [pl.BlockSpec((nb,) + lead + s, lambda b, s=s: (b,) + zeros(lead + s)) for s in state_shapes]
    if prev is not None:
        in_specs += state_specs
        args += list(prev)
    if state is not None:
        in_specs += state_specs
        args += list(state)
    return pl.pallas_call(
        functools.partial(_mlstm_kernel, t_len, state is not None, prev is not None, nb),
        grid=(n_batch // nb,),
        in_specs=in_specs,
        out_specs=[pl.BlockSpec((rows, M_HEADS * M_DV), lambda b: (b, 0))] + out_state_specs,
        out_shape=[jax.ShapeDtypeStruct((n_batch * t_len, M_HEADS * M_DV), BF16)]
        + [jax.ShapeDtypeStruct((n_batch,) + lead + s, F32) for s in state_shapes],
        scratch_shapes=[pltpu.VMEM((rows, M_HEADS * M_DV), F32), pltpu.VMEM((rows, M_HEADS * M_DV), F32),
                        pltpu.VMEM((nb * n_st, M_DV, LANES), F32),
                        pltpu.VMEM((nb * n_st, 8, LANES), F32), pltpu.VMEM((nb * n_st, 8, LANES), F32)],
        compiler_params=_cparams("arbitrary"),
        name="mlstm_lat" if state is not None else "mlstm_ctx",
    )(*args)


def _odd_in_kernel(has_prev, x_ref, mod_ref, g_ref, wp_ref, cos_ref, sin_ref, *refs):
    prev_refs, refs = (refs[:2], refs[2:]) if has_prev else ((), refs)
    q_ref, k_ref, v_ref, k_ctx_ref, v_ctx_ref = refs
    half = S_HEAD_DIM // 4
    n_q = S_HEADS * S_HEAD_DIM
    n_k = S_KV_HEADS * LANES
    assert TM // IN_PARTS == SEQ
    ctx_state = {}

    def rows_of(part):
        rs = slice(part * SEQ, (part + 1) * SEQ)
        h = _modulated(x_ref[rs, :], g_ref[...], mod_ref, 1).astype(BF16)
        p = jnp.dot(h, wp_ref[...], preferred_element_type=F32)
        cos = cos_ref[rs, :]
        sin = sin_ref[rs, :]
        yield
        for t in range(n_q // LANES):
            sl = slice(t * LANES, (t + 1) * LANES)
            q_ref[rs, sl] = _rope_tile(p[:, sl], cos, sin, half).astype(BF16)
        k_tiles = [_rope_tile(p[:, n_q + g * LANES:n_q + (g + 1) * LANES], cos, sin, half) for g in range(S_KV_HEADS)]
        v_tiles = [p[:, n_q + n_k + g * LANES:n_q + n_k + (g + 1) * LANES] for g in range(S_KV_HEADS)]
        for g in range(S_KV_HEADS):
            k_ref[rs, g * LANES:(g + 1) * LANES] = k_tiles[g].astype(BF16)
            v_ref[rs, g * LANES:(g + 1) * LANES] = v_tiles[g].astype(BF16)
        ctx_state[part] = (k_tiles, v_tiles)

    _run_staged((rows_of(part) for part in range(IN_PARTS)), skew=1)

    @pl.when(pl.program_id(0) < N_CTX // TM)
    def _ctx_state():
        for bb in range(IN_PARTS):
            k_tiles, v_tiles = ctx_state[bb]
            if has_prev:
                k_ctx_ref[bb, 0] = prev_refs[0][bb]
                v_ctx_ref[bb, 0] = prev_refs[1][bb]
            for g in range(S_KV_HEADS):
                slot = (bb, 1, g) if has_prev else (bb, g)
                k_ctx_ref[slot] = k_tiles[g][:, 0:S_HEAD_DIM]
                v_ctx_ref[slot] = v_tiles[g][:, 0:S_HEAD_DIM]


def _odd_in(x, mod_l, g_pre, wp, cos_t, sin_t, prev=None):
    row = lambda i: (i, 0)
    const = lambda i: (0, 0)
    st_in, st_out, st_shape = _state_specs([(S_KV_HEADS, SEQ, S_HEAD_DIM)] * 2, prev)
    widths = (S_HEADS * S_HEAD_DIM, S_KV_HEADS * LANES, S_KV_HEADS * LANES)
    return pl.pallas_call(
        functools.partial(_odd_in_kernel, prev is not None),
        grid=(N_TOK // TM,),
        in_specs=[
            pl.BlockSpec((TM, D_MODEL), row),
            pl.BlockSpec((1, N_MOD, D_MODEL), lambda i: (_group_of_block(i, TM), 0, 0)),
            pl.BlockSpec((1, D_MODEL), const),
            pl.BlockSpec((D_MODEL, sum(widths)), const),
            pl.BlockSpec((TM, LANES), lambda i: (_rope_block(i, TM), 0)),
            pl.BlockSpec((TM, LANES), lambda i: (_rope_block(i, TM), 0)),
        ] + st_in,
        out_specs=[pl.BlockSpec((TM, w), row) for w in widths] + st_out,
        out_shape=[jax.ShapeDtypeStruct((N_TOK, w), BF16) for w in widths] + st_shape,
        compiler_params=_cparams("arbitrary"),
        name="odd_in",
    )(x, mod_l, g_pre, wp, cos_t, sin_t, *(prev or ()))


def _gqa_heads(q_ref, keys, vals, sink_ref, mask, o_ref, tq):
    scale = S_HEAD_DIM ** -0.5
    lane = lax.broadcasted_iota(jnp.int32, (tq, LANES), 1)
    low = lane < S_HEAD_DIM

    def kv_group(g):
        k2, v2 = keys(g), vals(g)
        col = g * S_GROUP * S_HEAD_DIM
        tiles = [q_ref[:, col:col + LANES].astype(F32) * scale, q_ref[:, col + LANES:col + 2 * LANES].astype(F32) * scale]
        q4 = jnp.concatenate([jnp.where(low, tiles[0], 0.0), jnp.where(low, 0.0, tiles[0]),
                              jnp.where(low, tiles[1], 0.0), jnp.where(low, 0.0, tiles[1])], axis=0).astype(BF16)
        yield
        s = lax.dot_general(k2, q4, (((1,), (1,)), ((), ())), preferred_element_type=F32)
        if mask is not None:
            pieces, row = [], 0
            for start, keep in sorted(mask.items()):
                if start > row:
                    pieces.append(s[row:start])
                pieces.append(jnp.where(keep, s[start:start + keep.shape[0]], NEG))
                row = start + keep.shape[0]
            if row < s.shape[0]:
                pieces.append(s[row:])
            s = jnp.concatenate(pieces, axis=0)
        yield
        sk = jnp.concatenate([jnp.full((1, tq), sink_ref[g * S_GROUP + j], F32) for j in range(S_GROUP)], axis=1)
        m = jnp.maximum(jnp.max(s, axis=0, keepdims=True), sk)
        e = jnp.exp(s - m)
        yield
        l = jnp.sum(e, axis=0, keepdims=True) + jnp.exp(sk - m)
        o_t = lax.dot_general(v2, e.astype(BF16), (((0,), (0,)), ((), ())), preferred_element_type=F32)
        yield
        o4 = (o_t * (1.0 / l)).T
        o_ref[:, col:col + LANES] = jnp.where(low, o4[0:tq], o4[tq:2 * tq]).astype(BF16)
        o_ref[:, col + LANES:col + 2 * LANES] = jnp.where(low, o4[2 * tq:3 * tq], o4[3 * tq:4 * tq]).astype(BF16)

    _run_staged(kv_group(g) for g in range(S_KV_HEADS))


def _gqa_ctx_kernel(sink_ref, q_ref, k_ref, v_ref, o_ref):
    keys = lambda g: k_ref[:, g * LANES:(g + 1) * LANES]
    vals = lambda g: v_ref[:, g * LANES:(g + 1) * LANES]
    _gqa_heads(q_ref, keys, vals, sink_ref, None, o_ref, SEQ)


def _gqa_ctx(q, kd, vd, sink):
    row = lambda b: (b, 0)
    return pl.pallas_call(
        _gqa_ctx_kernel,
        grid=(BATCH,),
        in_specs=[
            pl.BlockSpec(memory_space=pltpu.SMEM),
            pl.BlockSpec((SEQ, S_HEADS * S_HEAD_DIM), row),
            pl.BlockSpec((SEQ, S_KV_HEADS * LANES), row),
            pl.BlockSpec((SEQ, S_KV_HEADS * LANES), row),
        ],
        out_specs=pl.BlockSpec((SEQ, S_HEADS * S_HEAD_DIM), row),
        out_shape=jax.ShapeDtypeStruct((N_CTX, S_HEADS * S_HEAD_DIM), BF16),
        compiler_params=_cparams("arbitrary"),
        name="gqa_ctx",
    )(sink, q, kd, vd)


def _gqa_lat_kernel(sink_ref, q_ref, k_ref, v_ref, ck_ref, cv_ref, o_ref):
    n = pl.program_id(1)
    nb = DEC_SEQ // S_BLOCK
    B = S_BLOCK
    prev0 = pl.multiple_of(jnp.maximum(n - 1, 0) * B, B)
    cur0 = pl.multiple_of(n * B, B)
    next0 = pl.multiple_of(jnp.minimum(n + 1, nb - 1) * B, B)
    t_idx = lax.broadcasted_iota(jnp.int32, (B, S_GROUP * B), 1) % B
    s_idx = lax.broadcasted_iota(jnp.int32, (B, S_GROUP * B), 0)
    far = jnp.int32(4 * B)
    keep_prev = s_idx >= t_idx + jnp.where(n == 0, far, 0)
    keep_next = s_idx + jnp.where(n == nb - 1, far, 0) <= t_idx
    mask = {PAST_LEN: keep_prev, PAST_LEN + 2 * B: keep_next}

    def gather(ref, cache_ref, g):
        sl = slice(g * LANES, (g + 1) * LANES)
        return jnp.concatenate([cache_ref[0, g], ref[pl.ds(prev0, B), sl], ref[pl.ds(cur0, B), sl],
                                ref[pl.ds(next0, B), sl]], axis=0)

    keys = lambda g: gather(k_ref, ck_ref, g)
    vals = lambda g: gather(v_ref, cv_ref, g)
    _gqa_heads(q_ref, keys, vals, sink_ref, mask, o_ref, B)


def _gqa_lat(q, kd, vd, cache_k2, cache_v2, sink):
    nb = DEC_SEQ // S_BLOCK
    qb0 = N_CTX // S_BLOCK
    kb0 = N_CTX // DEC_SEQ
    return pl.pallas_call(
        _gqa_lat_kernel,
        grid=(DEC_BATCH, nb),
        in_specs=[
            pl.BlockSpec(memory_space=pltpu.SMEM),
            pl.BlockSpec((S_BLOCK, S_HEADS * S_HEAD_DIM), lambda b, n: (qb0 + b * nb + n, 0)),
            pl.BlockSpec((DEC_SEQ, S_KV_HEADS * LANES), lambda b, n: (kb0 + b, 0)),
            pl.BlockSpec((DEC_SEQ, S_KV_HEADS * LANES), lambda b, n: (kb0 + b, 0)),
            pl.BlockSpec((1, S_KV_HEADS, PAST_LEN, LANES), lambda b, n: (b, 0, 0, 0)),
            pl.BlockSpec((1, S_KV_HEADS, PAST_LEN, LANES), lambda b, n: (b, 0, 0, 0)),
        ],
        out_specs=pl.BlockSpec((S_BLOCK, S_HEADS * S_HEAD_DIM), lambda b, n: (b * nb + n, 0)),
        out_shape=jax.ShapeDtypeStruct((N_LAT, S_HEADS * S_HEAD_DIM), BF16),
        compiler_params=_cparams("arbitrary", "arbitrary"),
        name="gqa_lat",
    )(sink, q, kd, vd, cache_k2, cache_v2)


def _rope_tables(rot_dim, lane_off, reps):
    nf = rot_dim // 4
    inv = np.float32(ROPE_BASE) ** (-np.arange(nf, dtype=np.float32) / np.float32(nf))
    pos = np.arange(DEC_SEQ)
    ang_r = (pos // GRID_W).astype(np.float32)[:, None] * inv
    ang_c = (pos % GRID_W).astype(np.float32)[:, None] * inv
    cos_g = np.concatenate([np.cos(ang_r), np.cos(ang_r), np.cos(ang_c), np.cos(ang_c)], axis=1)
    sin_g = np.concatenate([-np.sin(ang_r), np.sin(ang_r), -np.sin(ang_c), np.sin(ang_c)], axis=1)
    cos_t = np.ones((TM + DEC_SEQ, LANES), np.float32)
    sin_t = np.zeros((TM + DEC_SEQ, LANES), np.float32)
    for r in range(reps):
        lo = lane_off + r * rot_dim
        cos_t[TM:, lo:lo + rot_dim] = cos_g
        sin_t[TM:, lo:lo + rot_dim] = sin_g
    return jnp.asarray(cos_t, F32), jnp.asarray(sin_t, F32)


def _even_weights(w_in, w_qb, w_kvb, gate_b):
    z = lambda n: jnp.zeros((D_MODEL, n), F32)
    idx = np.cumsum([MLA_Q_LORA, MLA_KV_LORA, MLA_ROPE, M_HEADS * M_DK, M_HEADS * M_DK, M_HEADS * M_DV,
                     M_HEADS * M_DV])
    q_a, kv_a, k_rope, mq, mk, mv, mo, mg = jnp.split(w_in, idx, axis=1)
    wp = jnp.concatenate([q_a, kv_a, z(KR_LANE), k_rope, z(LANES - KR_LANE - MLA_ROPE), mq, mk, mv, mo, mg,
                          z(LANES - 4 * M_HEADS)], axis=1).astype(BF16)
    wq = jnp.pad(w_qb.reshape(MLA_Q_LORA, MLA_HEADS, MLA_QK), ((0, 0), (0, 0), (0, LANES - MLA_QK)))
    wq = wq.reshape(MLA_Q_LORA, MLA_HEADS * LANES).astype(BF16)
    kvb = w_kvb.reshape(MLA_KV_LORA, MLA_HEADS, MLA_NOPE + MLA_V)
    wk = jnp.pad(kvb[:, :, :MLA_NOPE], ((0, 0), (0, 0), (0, LANES - MLA_NOPE))).transpose(1, 0, 2).astype(BF16)
    wv = kvb[:, :, MLA_NOPE:].reshape(MLA_KV_LORA, MLA_HEADS // 2, 2 * MLA_V).transpose(1, 0, 2).astype(BF16)
    gb = jnp.pad(gate_b, (0, LANES - 4 * M_HEADS)).reshape(1, LANES)
    return wp, wq, wk, wv, gb


def _dup_heads(w):
    w3 = w.reshape(D_MODEL, S_KV_HEADS, S_HEAD_DIM)
    return jnp.concatenate([w3, w3], axis=-1).reshape(D_MODEL, S_KV_HEADS * LANES)


def _odd_weights(w_in):
    n_q = S_HEADS * S_HEAD_DIM
    n_kv = S_KV_HEADS * S_HEAD_DIM
    return jnp.concatenate([w_in[:, :n_q], _dup_heads(w_in[:, n_q:n_q + n_kv]), _dup_heads(w_in[:, n_q + n_kv:])],
                           axis=1).astype(BF16)


def kernel(x_prompt, x_sample, cache_mla_ckv, cache_mla_krope, state_mlstm_C, state_mlstm_n, state_mlstm_m,
           cache_swa_k, cache_swa_v, c, c_ctx, ada_w, ada_b, norm_g, ffn_w_gate, ffn_w_up, ffn_w_down,
           even_w_in, mla_g_qa, mla_g_kva, mla_w_qb, mla_w_kvb, mlstm_gate_b, mlstm_g_out, even_w_out,
           odd_w_in, swa_sink, odd_w_out):
    xs = (x_prompt.reshape(N_CTX, D_MODEL), x_sample.reshape(N_LAT, D_MODEL))
    cond = jnp.concatenate([c_ctx[None], c, jnp.zeros((COND_ROWS - N_GROUPS, D_MODEL), F32)], axis=0)
    mod = _adaln(cond, ada_w, ada_b)
    cos_e, sin_e = _rope_tables(MLA_ROPE, KR_LANE, 1)
    cos_o, sin_o = _rope_tables(S_HEAD_DIM, 0, LANES // S_HEAD_DIM)
    w32 = (ffn_w_gate, ffn_w_up, ffn_w_down)
    w16 = tuple(w[0, 0].astype(BF16) for w in w32)

    assert N_EVEN == 2 and N_ODD == 2
    even_state = mlstm_state = odd_state = None
    for l in range(DEPTH):
        mod_l = mod[l]
        g = norm_g[l]
        i = l // 2
        (x,), w16 = _ffn(xs, mod_l, g[0:2], w16, 0, next_w=(w32, l, 1))
        if l % 2 == 0:
            wp, wq, wk, wv, gb = _even_weights(even_w_in[i], mla_w_qb[i], mla_w_kvb[i], mlstm_gate_b[i])
            q, ckv, kr, mq, mk, mv, mo, mg, *even_state = _even_in(
                x, mod_l, g[2:3], wp, mla_g_qa[i].reshape(1, -1), mla_g_kva[i].reshape(1, -1), wq, cos_e, sin_e,
                prev=even_state)
            cache_kr = jnp.pad(cache_mla_krope[:, i], ((0, 0), (0, 0), (KR_LANE, LANES - KR_LANE - MLA_ROPE)))
            att_c = _mla(q, ckv, kr, wk, wv, row0=0, n_batch=BATCH, t_own=SEQ, tq=SEQ)
            att_l = _mla(q, ckv, kr, wk, wv, row0=N_CTX, n_batch=DEC_BATCH, t_own=DEC_SEQ, tq=512,
                         cache=(cache_mla_ckv[:, i], cache_kr))
            g_out = mlstm_g_out[i].reshape(1, -1)
            mo_c, *mlstm_state = _mlstm(mq, mk, mv, mo, mg, gb, g_out, row0=0, n_batch=BATCH, t_len=SEQ,
                                        prev=mlstm_state)
            n_st = 2 * M_HEADS
            state = (state_mlstm_C[:, i].reshape(DEC_BATCH, n_st, M_DK, M_DV),
                     state_mlstm_n[:, i].reshape(DEC_BATCH, n_st, M_DK),
                     jnp.broadcast_to(state_mlstm_m[:, i].reshape(DEC_BATCH, n_st, 1), (DEC_BATCH, n_st, LANES)))
            mo_l, _, _, _ = _mlstm(mq, mk, mv, mo, mg, gb, g_out, row0=N_CTX, n_batch=DEC_BATCH, t_len=DEC_SEQ,
                                   state=state)
            w_out = even_w_out[i].astype(BF16)
            n_att = MLA_HEADS * MLA_V
            acts, act_w = [(att_c, att_l), (mo_c, mo_l)], [w_out[:n_att], w_out[n_att:]]
        else:
            q, kd, vd, *odd_state = _odd_in(x, mod_l, g[2:3], _odd_weights(odd_w_in[i]), cos_o, sin_o,
                                            prev=odd_state)
            dup = lambda a: jnp.concatenate([a, a], axis=-1).astype(BF16)
            o_c = _gqa_ctx(q, kd, vd, swa_sink[i])
            o_l = _gqa_lat(q, kd, vd, dup(cache_swa_k[:, i]), dup(cache_swa_v[:, i]), swa_sink[i])
            acts, act_w = [(o_c, o_l)], [odd_w_out[i].astype(BF16)]
        last = l == DEPTH - 1
        xs, w16 = _ffn((x,), mod_l, g[3:6], w16, 1, acts=acts, act_w=act_w, dual_out=last,
                       next_w=None if last else (w32, l + 1, 0))

    return (xs[0].reshape(BATCH, SEQ, D_MODEL), xs[1].reshape(DEC_BATCH, DEC_SEQ, D_MODEL),
            even_state[0], even_state[1],
            mlstm_state[0].reshape(BATCH, N_EVEN, 2, M_HEADS, M_DK, M_DV),
            mlstm_state[1].reshape(BATCH, N_EVEN, 2, M_HEADS, M_DK),
            mlstm_state[2][..., 0].reshape(BATCH, N_EVEN, 2, M_HEADS),
            odd_state[0], odd_state[1])
```

```python
import functools

import jax
import jax.numpy as jnp
import numpy as np
from jax import lax
from jax.experimental import pallas as pl
from jax.experimental.pallas import tpu as pltpu

F32 = jnp.float32
BF16 = jnp.bfloat16

D_MODEL = 1024
BATCH = 32
SEQ = 256
DEPTH = 4
N_EVEN = (DEPTH + 1) // 2
N_ODD = DEPTH // 2
DEC_BATCH = 4
DEC_SEQ = 1024
PAST_LEN = 512
GRID_W = 64
N_MOD = 9
D_FF = 2816
EPS = 1e-6
ROPE_BASE = 10000.0
NEG = -1e30
MLA_HEADS = 8
MLA_NOPE = 64
MLA_ROPE = 32
MLA_QK = MLA_NOPE + MLA_ROPE
MLA_V = 64
MLA_Q_LORA = 256
MLA_KV_LORA = 128
M_HEADS = 4
M_DK = 64
M_DV = 128
M_CHUNK = 128
S_HEADS = 16
S_KV_HEADS = 4
S_GROUP = S_HEADS // S_KV_HEADS
S_HEAD_DIM = 64
S_WINDOW = 128
S_BLOCK = 128

N_CTX = BATCH * SEQ
N_LAT = DEC_BATCH * DEC_SEQ
N_TOK = N_CTX + N_LAT
N_GROUPS = 1 + DEC_BATCH
COND_ROWS = 8

LANES = 128
FF_CHUNK = 256
N_FF_CHUNKS = D_FF // FF_CHUNK
assert N_FF_CHUNKS * FF_CHUNK == D_FF

TM = 512
ADA_TN = 1536
M_SEQ_PER_STEP = 2
M_CTX_SEQ_PER_STEP = 4
S_SEQ_PER_STEP = 2
FFN_PARTS = 2
IN_PARTS = 2
VMEM_LIMIT = 52 * 1024 * 1024

E_QA, E_KVA, E_KR, E_MQ, E_MK, E_MV, E_MO, E_MG, E_COLS = 0, 256, 384, 512, 768, 1024, 1536, 2048, 2176
KR_LANE = MLA_NOPE


def _cparams(*sem):
    return pltpu.CompilerParams(dimension_semantics=sem, vmem_limit_bytes=VMEM_LIMIT)


def _group_of_block(i, tm):
    n_ctx = N_CTX // tm
    per_lat = DEC_SEQ // tm
    return jnp.where(i < n_ctx, 0, 1 + (i - n_ctx) // per_lat)


def _rope_block(i, tm):
    n_ctx = N_CTX // tm
    per_lat = DEC_SEQ // tm
    return jnp.where(i < n_ctx, 0, 1 + (i - n_ctx) % per_lat)


def _rms(x, g):
    return x * lax.rsqrt(jnp.mean(x * x, axis=-1, keepdims=True) + EPS) * g


def _modulated(x, g_pre, mod_ref, sub):
    shift = mod_ref[0, 3 * sub:3 * sub + 1, :]
    scale = mod_ref[0, 3 * sub + 1:3 * sub + 2, :]
    return _rms(x, g_pre) * (1.0 + scale) + shift


def _silu(x):
    return x * jax.nn.sigmoid(x)


def _swap_halves(x, half):
    lane = lax.broadcasted_iota(jnp.int32, x.shape, 1)
    up = pltpu.roll(x, LANES - half, 1)
    down = pltpu.roll(x, half, 1)
    return jnp.where(lane % (2 * half) < half, up, down)


def _rope_tile(x, cos, sin, half):
    return x * cos + _swap_halves(x, half) * sin


def _run_staged(stages, skew=0):
    pending, live, tick, started = list(stages), [], 0, 0
    while pending or live:
        while pending and tick >= started * skew:
            live.append(pending.pop(0))
            started += 1
        live = [s for s in live if next(s, True) is None]
        tick += 1


def _adaln_kernel(cond_ref, w_ref, b_ref, o_ref):
    s = _silu(cond_ref[...]).astype(BF16)
    o_ref[0] = jnp.dot(s, w_ref[0].astype(BF16), preferred_element_type=F32) + b_ref[0]


def _adaln(cond, ada_w, ada_b):
    n_out = N_MOD * D_MODEL
    out = pl.pallas_call(
        _adaln_kernel,
        grid=(DEPTH, n_out // ADA_TN),
        in_specs=[
            pl.BlockSpec((COND_ROWS, D_MODEL), lambda l, j: (0, 0)),
            pl.BlockSpec((1, D_MODEL, ADA_TN), lambda l, j: (l, 0, j)),
            pl.BlockSpec((1, 1, ADA_TN), lambda l, j: (l, 0, j)),
        ],
        out_specs=pl.BlockSpec((1, COND_ROWS, ADA_TN), lambda l, j: (l, 0, j)),
        out_shape=jax.ShapeDtypeStruct((DEPTH, COND_ROWS, n_out), F32),
        compiler_params=_cparams("arbitrary", "arbitrary"),
        name="adaln",
    )(cond, ada_w, ada_b.reshape(DEPTH, 1, n_out))
    return out.reshape(DEPTH, COND_ROWS, N_MOD, D_MODEL)


def _ctx_block(i):
    return jnp.minimum(i, N_CTX // TM - 1)


def _lat_block(i):
    return jnp.maximum(i - N_CTX // TM, 0)


def _dual_specs(width):
    return [pl.BlockSpec((TM, width), lambda i: (_ctx_block(i), 0)),
            pl.BlockSpec((TM, width), lambda i: (_lat_block(i), 0))]


def _ffn_kernel(sub, dual_in, dual_out, n_act, convert, *refs):
    it = iter(refs)
    take = lambda n: [next(it) for _ in range(n)]
    a_refs = take(2 * n_act)
    w_refs = take(n_act)
    x_refs = take(2 if dual_in else 1)
    mod_ref, g_ref, wg_ref, wu_ref, wd_ref = take(5)
    nxt_refs = take(3) if convert else []
    o_refs = take(2 if dual_out else 1)
    nxt_out_refs = take(3) if convert else []
    h_ref, xs_ref = take(2)
    step = pl.program_id(0)
    is_ctx = step < N_CTX // TM
    g0 = 1 if n_act else 0

    gate_row = mod_ref[0, 3 * sub + 2:3 * sub + 3, :]
    results = {}

    def rows_of(part):
        rs = slice(part * (TM // FFN_PARTS), (part + 1) * (TM // FFN_PARTS))
        x = jnp.where(is_ctx, x_refs[0][rs, :], x_refs[1][rs, :]) if dual_in else x_refs[0][rs, :]
        if n_act:
            y = None
            for k in range(n_act):
                a = jnp.where(is_ctx, a_refs[2 * k][rs, :], a_refs[2 * k + 1][rs, :])
                d = jnp.dot(a, w_refs[k][...], preferred_element_type=F32)
                y = d if y is None else y + d
            x = x + mod_ref[0, 5:6, :] * _rms(y, g_ref[0:1, :])
        xs_ref[rs, :] = x
        h = _modulated(x, g_ref[g0:g0 + 1, :], mod_ref, sub).astype(BF16)
        yield
        for c in range(N_FF_CHUNKS):
            sl = slice(c * FF_CHUNK, (c + 1) * FF_CHUNK)
            gate = jnp.dot(h, wg_ref[:, sl], preferred_element_type=F32)
            up = jnp.dot(h, wu_ref[:, sl], preferred_element_type=F32)
            h_ref[rs, sl] = (_silu(gate) * up).astype(BF16)
            yield
        y = jnp.dot(h_ref[rs, :], wd_ref[...], preferred_element_type=F32)
        yield
        results[part] = xs_ref[rs, :] + 0.5 * gate_row * _rms(y, g_ref[g0 + 1:g0 + 2, :])

    _run_staged(rows_of(part) for part in range(FFN_PARTS))
    result = jnp.concatenate([results[part] for part in range(FFN_PARTS)], axis=0)

    if dual_out:
        @pl.when(is_ctx)
        def _ctx():
            o_refs[0][...] = result

        @pl.when(jnp.logical_not(is_ctx))
        def _lat():
            o_refs[1][...] = result
    else:
        o_refs[0][...] = result

    if convert:
        @pl.when(step < N_FF_CHUNKS)
        def _next_weights():
            for src_ref, dst_ref in zip(nxt_refs, nxt_out_refs):
                dst_ref[...] = src_ref[...].astype(BF16)


def _ffn(xs, mod_l, g_rows, w16, half, *, acts=(), act_w=(), dual_out=False, next_w=None):
    dual_in = len(xs) == 2
    n_act = len(acts)
    convert = next_w is not None
    sub = 2 * half
    const = lambda i: (0, 0)
    once = dict(pipeline_mode=pl.Buffered(1))
    in_specs, args = [], []
    for a_c, a_l in acts:
        in_specs += _dual_specs(a_c.shape[1])
        args += [a_c, a_l]
    in_specs += [pl.BlockSpec(w.shape, const, **once) for w in act_w]
    args += list(act_w)
    in_specs += _dual_specs(D_MODEL) if dual_in else [pl.BlockSpec((TM, D_MODEL), lambda i: (i, 0))]
    args += list(xs)
    in_specs += [
        pl.BlockSpec((1, N_MOD, D_MODEL), lambda i: (_group_of_block(i, TM), 0, 0)),
        pl.BlockSpec(g_rows.shape, const),
        pl.BlockSpec((D_MODEL, D_FF), const, **once),
        pl.BlockSpec((D_MODEL, D_FF), const, **once),
        pl.BlockSpec((D_FF, D_MODEL), const, **once),
    ]
    args += [mod_l, g_rows, *w16]
    if dual_out:
        out_specs = _dual_specs(D_MODEL)
        out_shape = [jax.ShapeDtypeStruct((N_CTX, D_MODEL), F32), jax.ShapeDtypeStruct((N_LAT, D_MODEL), F32)]
    else:
        out_specs = [pl.BlockSpec((TM, D_MODEL), lambda i: (i, 0))]
        out_shape = [jax.ShapeDtypeStruct((N_TOK, D_MODEL), F32)]
    if convert:
        (wg32, wu32, wd32), ln, hn = next_w
        chunk = lambda i: jnp.minimum(i, N_FF_CHUNKS - 1)
        in_specs += [
            pl.BlockSpec((None, None, D_MODEL, FF_CHUNK), lambda i: (ln, hn, 0, chunk(i))),
            pl.BlockSpec((None, None, D_MODEL, FF_CHUNK), lambda i: (ln, hn, 0, chunk(i))),
            pl.BlockSpec((None, None, FF_CHUNK, D_MODEL), lambda i: (ln, hn, chunk(i), 0)),
        ]
        args += [wg32, wu32, wd32]
        out_specs += [
            pl.BlockSpec((D_MODEL, FF_CHUNK), lambda i: (0, chunk(i))),
            pl.BlockSpec((D_MODEL, FF_CHUNK), lambda i: (0, chunk(i))),
            pl.BlockSpec((FF_CHUNK, D_MODEL), lambda i: (chunk(i), 0)),
        ]
        out_shape += [jax.ShapeDtypeStruct((D_MODEL, D_FF), BF16), jax.ShapeDtypeStruct((D_MODEL, D_FF), BF16),
                      jax.ShapeDtypeStruct((D_FF, D_MODEL), BF16)]
    outs = pl.pallas_call(
        functools.partial(_ffn_kernel, sub, dual_in, dual_out, n_act, convert),
        grid=(N_TOK // TM,),
        in_specs=in_specs,
        out_specs=out_specs,
        out_shape=out_shape,
        scratch_shapes=[pltpu.VMEM((TM, D_FF), BF16), pltpu.VMEM((TM, D_MODEL), F32)],
        compiler_params=_cparams("arbitrary"),
        name="mix_ffn" if n_act else "ffn",
    )(*args)
    n_stream = 2 if dual_out else 1
    stream = tuple(outs[:n_stream])
    return stream, (tuple(outs[n_stream:]) if convert else None)


def _even_in_kernel(has_prev, x_ref, mod_ref, g_ref, wp_ref, gqa_ref, gkva_ref, wq_ref, cos_ref, sin_ref, *refs):
    prev_refs, refs = (refs[:2], refs[2:]) if has_prev else ((), refs)
    q_ref, ckv_ref, kr_ref, mq_ref, mk_ref, mv_ref, mo_ref, mg_ref, ckv_ctx_ref, kr_ctx_ref = refs
    half = MLA_ROPE // 4
    assert TM // IN_PARTS == SEQ
    ctx_state = {}

    def rows_of(part):
        rs = slice(part * SEQ, (part + 1) * SEQ)
        h = _modulated(x_ref[rs, :], g_ref[...], mod_ref, 1).astype(BF16)
        p = jnp.dot(h, wp_ref[...], preferred_element_type=F32)
        cos = cos_ref[rs, :]
        sin = sin_ref[rs, :]
        yield
        qn = _rms(p[:, E_QA:E_QA + MLA_Q_LORA], gqa_ref[...]).astype(BF16)
        q = jnp.dot(qn, wq_ref[...], preferred_element_type=F32)
        ckv = _rms(p[:, E_KVA:E_KVA + MLA_KV_LORA], gkva_ref[...])
        ckv_ref[rs, :] = ckv.astype(BF16)
        kr = _rope_tile(p[:, E_KR:E_KR + LANES], cos, sin, half)
        kr_ref[rs, :] = kr.astype(BF16)
        ctx_state[part] = (ckv, kr[:, KR_LANE:KR_LANE + MLA_ROPE])
        mq_ref[rs, :] = p[:, E_MQ:E_MK].astype(BF16)
        mk_ref[rs, :] = p[:, E_MK:E_MV].astype(BF16)
        mv_ref[rs, :] = p[:, E_MV:E_MO].astype(BF16)
        mo_ref[rs, :] = p[:, E_MO:E_MG]
        mg_ref[rs, :] = p[:, E_MG:E_COLS]
        yield
        for hd in range(MLA_HEADS):
            sl = slice(hd * LANES, (hd + 1) * LANES)
            q_ref[rs, sl] = _rope_tile(q[:, sl], cos, sin, half).astype(BF16)

    _run_staged((rows_of(part) for part in range(IN_PARTS)), skew=1)

    @pl.when(pl.program_id(0) < N_CTX // TM)
    def _ctx_state():
        for bb in range(IN_PARTS):
            ckv, kr = ctx_state[bb]
            if has_prev:
                ckv_ctx_ref[bb, 0] = prev_refs[0][bb]
                kr_ctx_ref[bb, 0] = prev_refs[1][bb]
                ckv_ctx_ref[bb, 1] = ckv
                kr_ctx_ref[bb, 1] = kr
            else:
                ckv_ctx_ref[bb] = ckv
                kr_ctx_ref[bb] = kr


def _state_specs(shapes, prev):
    nseq = TM // SEQ
    zeros = lambda s: (0,) * len(s)
    in_specs = [pl.BlockSpec((nseq,) + s, lambda i, s=s: (_ctx_block(i),) + zeros(s)) for s in shapes] if prev else []
    lead = (2,) if prev else ()
    out_specs = [pl.BlockSpec((nseq,) + lead + s, lambda i, s=s: (_ctx_block(i),) + zeros(lead + s)) for s in shapes]
    out_shape = [jax.ShapeDtypeStruct((BATCH,) + lead + s, F32) for s in shapes]
    return in_specs, out_specs, out_shape


def _even_in(x, mod_l, g_pre, wp, g_qa, g_kva, wq, cos_t, sin_t, prev=None):
    row = lambda i: (i, 0)
    const = lambda i: (0, 0)
    st_in, st_out, st_shape = _state_specs([(SEQ, MLA_KV_LORA), (SEQ, MLA_ROPE)], prev)
    widths = (MLA_HEADS * LANES, LANES, LANES, M_HEADS * M_DK, M_HEADS * M_DK, M_HEADS * M_DV,
              M_HEADS * M_DV, LANES)
    dtypes = (BF16, BF16, BF16, BF16, BF16, BF16, F32, F32)
    return pl.pallas_call(
        functools.partial(_even_in_kernel, prev is not None),
        grid=(N_TOK // TM,),
        in_specs=[
            pl.BlockSpec((TM, D_MODEL), row),
            pl.BlockSpec((1, N_MOD, D_MODEL), lambda i: (_group_of_block(i, TM), 0, 0)),
            pl.BlockSpec((1, D_MODEL), const),
            pl.BlockSpec((D_MODEL, E_COLS), const),
            pl.BlockSpec((1, MLA_Q_LORA), const),
            pl.BlockSpec((1, MLA_KV_LORA), const),
            pl.BlockSpec((MLA_Q_LORA, MLA_HEADS * LANES), const),
            pl.BlockSpec((TM, LANES), lambda i: (_rope_block(i, TM), 0)),
            pl.BlockSpec((TM, LANES), lambda i: (_rope_block(i, TM), 0)),
        ] + st_in,
        out_specs=[pl.BlockSpec((TM, w), row) for w in widths] + st_out,
        out_shape=[jax.ShapeDtypeStruct((N_TOK, w), dt) for w, dt in zip(widths, dtypes)] + st_shape,
        compiler_params=_cparams("arbitrary"),
        name="even_in",
    )(x, mod_l, g_pre, wp, g_qa, g_kva, wq, cos_t, sin_t, *(prev or ()))


def _mla_kernel(t_own, n_cache, tq, *refs):
    if n_cache:
        q_ref, ckv_ref, kr_ref, cckv_ref, ckr_ref, wk_ref, wv_ref, o_ref, ks_ref, vs_ref = refs
    else:
        q_ref, ckv_ref, kr_ref, wk_ref, wv_ref, o_ref, ks_ref, vs_ref = refs
    t_keys = n_cache + t_own

    @pl.when(pl.program_id(1) == 0)
    def _expand():
        if n_cache:
            ckv16 = jnp.concatenate([cckv_ref[0].astype(BF16), ckv_ref[...]], axis=0)
            kr = jnp.concatenate([ckr_ref[0].astype(BF16), kr_ref[...]], axis=0)
        else:
            ckv16, kr = ckv_ref[...], kr_ref[...]
        for hd in range(MLA_HEADS):
            ks_ref[hd] = (jnp.dot(ckv16, wk_ref[hd], preferred_element_type=F32) + kr).astype(BF16)
        for pr in range(MLA_HEADS // 2):
            vs_ref[pr] = jnp.dot(ckv16, wv_ref[pr], preferred_element_type=F32).astype(BF16)

    scale = MLA_QK ** -0.5
    lane = lax.broadcasted_iota(jnp.int32, (tq, LANES), 1)
    for pr in range(MLA_HEADS // 2):
        outs = []
        for j in range(2):
            hd = 2 * pr + j
            qh = q_ref[:, hd * LANES:(hd + 1) * LANES]
            s = lax.dot_general(qh, ks_ref[hd], (((1,), (1,)), ((), ())), preferred_element_type=F32) * scale
            m = jnp.max(s, axis=-1, keepdims=True)
            e = jnp.exp(s - m)
            l = jnp.sum(e, axis=-1, keepdims=True)
            outs.append(jnp.dot(e.astype(BF16), vs_ref[pr], preferred_element_type=F32) / l)
        o_ref[:, pr * LANES:(pr + 1) * LANES] = jnp.where(lane < MLA_V, outs[0], outs[1]).astype(BF16)
    del t_keys


def _mla(q, ckv, kr, wk, wv, *, row0, n_batch, t_own, tq, cache=None):
    n_cache = 0 if cache is None else cache[0].shape[1]
    nq = t_own // tq
    qb0 = row0 // tq
    kb0 = row0 // t_own
    in_specs = [
        pl.BlockSpec((tq, MLA_HEADS * LANES), lambda b, j: (qb0 + b * nq + j, 0)),
        pl.BlockSpec((t_own, LANES), lambda b, j: (kb0 + b, 0)),
        pl.BlockSpec((t_own, LANES), lambda b, j: (kb0 + b, 0)),
    ]
    args = [q, ckv, kr]
    if n_cache:
        in_specs += [pl.BlockSpec((1, n_cache, LANES), lambda b, j: (b, 0, 0))] * 2
        args += list(cache)
    in_specs += [
        pl.BlockSpec((MLA_HEADS, LANES, LANES), lambda b, j: (0, 0, 0)),
        pl.BlockSpec((MLA_HEADS // 2, LANES, LANES), lambda b, j: (0, 0, 0)),
    ]
    args += [wk, wv]
    t_keys = n_cache + t_own
    return pl.pallas_call(
        functools.partial(_mla_kernel, t_own, n_cache, tq),
        grid=(n_batch, nq),
        in_specs=in_specs,
        out_specs=pl.BlockSpec((tq, MLA_HEADS * MLA_V), lambda b, j: (b * nq + j, 0)),
        out_shape=jax.ShapeDtypeStruct((n_batch * t_own, MLA_HEADS * MLA_V), BF16),
        scratch_shapes=[pltpu.VMEM((MLA_HEADS, t_keys, LANES), BF16),
                        pltpu.VMEM((MLA_HEADS // 2, t_keys, LANES), BF16)],
        compiler_params=_cparams("arbitrary", "arbitrary"),
        name="mla_lat" if n_cache else "mla_ctx",
    )(*args)


def _log_sigmoid(x):
    return jnp.minimum(x, 0.0) - jnp.log1p(jnp.exp(-jnp.abs(x)))


def _mlstm_chunk(rev, r0, st, hd, allowed, cum, e_rows, e_cols, mq_ref, mk_ref, mv_ref, ct_ref, ns_ref, ms_ref,
                 h_ref):
    L = M_CHUNK
    nt = (((1,), (1,)), ((), ()))
    gi = (2 * M_HEADS if rev else 0) + hd
    gf = gi + M_HEADS
    edge = 0 if rev else L - 1
    rows = pl.ds(r0, L)
    q16 = mq_ref[rows, hd * M_DK:(hd + 1) * M_DK]
    k16 = mk_ref[rows, hd * M_DK:(hd + 1) * M_DK] * (M_DK ** -0.5)
    v_t16 = mv_ref[rows, hd * M_DV:(hd + 1) * M_DV].T
    v_t = v_t16.astype(F32)
    ct_prev = ct_ref[st, :, 0:M_DK]
    n_prev = ns_ref[st, 0:1, 0:M_DK]
    m_prev = ms_ref[st, 0:1, 0:1]
    yield

    b_row = cum[gf:gf + 1, :]
    e_row = e_rows[gi:gi + 1, :]
    e_col = e_cols[:, gi:gi + 1]
    dmat = jnp.where(allowed, b_row + e_col, NEG)
    inter = m_prev + b_row
    mt = jnp.maximum(inter, jnp.max(dmat, axis=0, keepdims=True))
    w_inter = jnp.exp(inter - mt)
    qk = lax.dot_general(k16, q16, nt, preferred_element_type=F32)
    a = qk * jnp.exp(dmat - mt)
    yield
    n8 = jnp.broadcast_to(n_prev, (8, M_DK)).astype(BF16)
    nq = lax.dot_general(n8, q16, nt, preferred_element_type=F32)[0:1]
    num = w_inter * lax.dot_general(ct_prev.astype(BF16), q16, nt, preferred_element_type=F32) \
        + jnp.dot(v_t16, a.astype(BF16), preferred_element_type=F32)
    den = w_inter * nq + jnp.sum(a, axis=0, keepdims=True)
    h_t = num * (1.0 / jnp.maximum(jnp.abs(den), jnp.exp(-mt)))
    h_ref[rows, hd * M_DV:(hd + 1) * M_DV] = h_t.T
    yield

    m_new = mt[:, edge:edge + 1]
    b_last = b_row[:, edge:edge + 1]
    w_s = jnp.exp(b_last + e_row - m_new)
    w_c = jnp.exp(m_prev + b_last - m_new)
    ct_ref[st, :, 0:M_DK] = w_c * ct_prev + jnp.dot((v_t * w_s).astype(BF16), k16, preferred_element_type=F32)
    w8 = jnp.broadcast_to(w_s, (8, L)).astype(BF16)
    ns_ref[st, 0:1, 0:M_DK] = w_c * n_prev + jnp.dot(w8, k16, preferred_element_type=F32)[0:1]
    ms_ref[st, 0:1, :] = jnp.broadcast_to(m_new, (1, LANES))


def _mlstm_kernel(t_len, has_state, has_prev, nb, *refs):
    n_st = 2 * M_HEADS
    prev_refs = ()
    if has_prev:
        prev_refs, refs = refs[7:10], refs[:7] + refs[10:]
    if has_state:
        (mq_ref, mk_ref, mv_ref, mo_ref, mg_ref, gb_ref, gout_ref, c0_ref, n0_ref, m0_ref,
         out_ref, c_ref, n_ref, m_ref, hf_ref, hb_ref, ct_ref, ns_ref, ms_ref) = refs
        pad = jnp.zeros((LANES - M_DK, M_DV), F32)
        for bb in range(nb):
            for st in range(n_st):
                ct_ref[bb * n_st + st] = jnp.concatenate([c0_ref[bb, st], pad], axis=0).T
                ns_ref[bb * n_st + st, 0:1, 0:M_DK] = n0_ref[bb, st:st + 1, :]
                ms_ref[bb * n_st + st, 0:1, :] = m0_ref[bb, st:st + 1, :]
    else:
        (mq_ref, mk_ref, mv_ref, mo_ref, mg_ref, gb_ref, gout_ref,
         out_ref, c_ref, n_ref, m_ref, hf_ref, hb_ref, ct_ref, ns_ref, ms_ref) = refs
        ct_ref[...] = jnp.zeros_like(ct_ref)
        ns_ref[...] = jnp.zeros_like(ns_ref)
        ms_ref[...] = jnp.zeros_like(ms_ref)
    L = M_CHUNK
    nc = t_len // L
    r_idx = lax.broadcasted_iota(jnp.int32, (L, L), 0)
    c_idx = lax.broadcasted_iota(jnp.int32, (L, L), 1)
    lower = c_idx <= r_idx
    upper = c_idx >= r_idx
    lane = lax.broadcasted_iota(jnp.int32, (L, LANES), 1)
    is_forget = (lane % (2 * M_HEADS)) >= M_HEADS
    n_gate = 4 * M_HEADS
    zeros_below = jnp.zeros((L - n_gate, L), F32)

    def gate_sums(r0, rev):
        g = mg_ref[pl.ds(r0, L), :] + gb_ref[...]
        g = jnp.where(is_forget, _log_sigmoid(g), g)
        rows = g.T[0:n_gate, :]
        tri = (lower if rev else upper).astype(BF16)
        hi = rows.astype(BF16)
        rest = rows - hi.astype(F32)
        mid = rest.astype(BF16)
        lo = (rest - mid.astype(F32)).astype(BF16)
        cum = (jnp.dot(hi, tri, preferred_element_type=F32) + jnp.dot(mid, tri, preferred_element_type=F32)
               + jnp.dot(lo, tri, preferred_element_type=F32))
        e_rows = rows - pltpu.roll(cum, n_gate - M_HEADS, 0)
        e_cols = jnp.concatenate([e_rows, zeros_below], axis=0).T
        return cum, e_rows, e_cols

    def step(c, carry):
        chunks = []
        for bb in range(nb):
            for rev in (False, True):
                r0 = pl.multiple_of(bb * t_len + (nc - 1 - c if rev else c) * L, L)
                cum, e_rows, e_cols = gate_sums(r0, rev)
                allowed = lower if rev else upper
                for hd in range(M_HEADS):
                    st = bb * n_st + (M_HEADS if rev else 0) + hd
                    chunks.append(_mlstm_chunk(rev, r0, st, hd, allowed, cum, e_rows, e_cols, mq_ref, mk_ref, mv_ref,
                                               ct_ref, ns_ref, ms_ref, hb_ref if rev else hf_ref))
        _run_staged(chunks)
        return carry

    lax.fori_loop(0, nc, step, 0)
    for bb in range(nb):
        if has_prev:
            for o_ref, p_ref in zip((c_ref, n_ref, m_ref), prev_refs):
                o_ref[bb, 0] = p_ref[bb]
        own = (bb, 1) if has_prev else (bb,)
        for st in range(n_st):
            c_ref[own + (st,)] = ct_ref[bb * n_st + st].T[0:M_DK, :]
            n_ref[own + (slice(st, st + 1), slice(None))] = ns_ref[bb * n_st + st, 0:1, 0:M_DK]
            m_ref[own + (slice(st, st + 1), slice(None))] = ms_ref[bb * n_st + st, 0:1, :]
    for hd in range(M_HEADS):
        sl = slice(hd * M_DV, (hd + 1) * M_DV)
        hm = hf_ref[:, sl] + hb_ref[:, sl]
        hm = hm * lax.rsqrt(jnp.mean(hm * hm, axis=-1, keepdims=True) + EPS) * gout_ref[:, sl]
        out_ref[:, sl] = (jax.nn.sigmoid(mo_ref[:, sl]) * hm).astype(BF16)


def _mlstm(mq, mk, mv, mo, mg, gate_b, g_out, *, row0, n_batch, t_len, nb, state=None, prev=None):
    rows = nb * t_len
    kb0 = row0 // rows
    row = lambda b: (kb0 + b, 0)
    const = lambda b: (0, 0)
    n_st = 2 * M_HEADS
    in_specs = [
        pl.BlockSpec((rows, M_HEADS * M_DK), row),
        pl.BlockSpec((rows, M_HEADS * M_DK), row),
        pl.BlockSpec((rows, M_HEADS * M_DV), row),
        pl.BlockSpec((rows, M_HEADS * M_DV), row),
        pl.BlockSpec((rows, LANES), row),
        pl.BlockSpec((1, LANES), const),
        pl.BlockSpec((1, M_HEADS * M_DV), const),
    ]
    args = [mq, mk, mv, mo, mg, gate_b, g_out]
    state_shapes = [(n_st, M_DK, M_DV), (n_st, M_DK), (n_st, LANES)]
    zeros = lambda s: (0,) * len(s)
    state_specs = [pl.BlockSpec((nb,) + s, lambda b, s=s: (b,) + zeros(s)) for s in state_shapes]
    lead = (2,) if prev is not None else ()
    out_state_specs = [pl.BlockSpec((nb,) + lead + s, lambda b, s=s: (b,) + zeros(lead + s)) for s in state_shapes]
    if prev is not None:
        in_specs += state_specs
        args += list(prev)
    if state is not None:
        in_specs += state_specs
        args += list(state)
    return pl.pallas_call(
        functools.partial(_mlstm_kernel, t_len, state is not None, prev is not None, nb),
        grid=(n_batch // nb,),
        in_specs=in_specs,
        out_specs=[pl.BlockSpec((rows, M_HEADS * M_DV), lambda b: (b, 0))] + out_state_specs,
        out_shape=[jax.ShapeDtypeStruct((n_batch * t_len, M_HEADS * M_DV), BF16)]
        + [jax.ShapeDtypeStruct((n_batch,) + lead + s, F32) for s in state_shapes],
        scratch_shapes=[pltpu.VMEM((rows, M_HEADS * M_DV), F32), pltpu.VMEM((rows, M_HEADS * M_DV), F32),
                        pltpu.VMEM((nb * n_st, M_DV, LANES), F32),
                        pltpu.VMEM((nb * n_st, 8, LANES), F32), pltpu.VMEM((nb * n_st, 8, LANES), F32)],
        compiler_params=_cparams("arbitrary"),
        name="mlstm_lat" if state is not None else "mlstm_ctx",
    )(*args)


def _odd_in_kernel(has_prev, x_ref, mod_ref, g_ref, wp_ref, cos_ref, sin_ref, *refs):
    prev_refs, refs = (refs[:2], refs[2:]) if has_prev else ((), refs)
    q_ref, k_ref, v_ref, k_ctx_ref, v_ctx_ref = refs
    half = S_HEAD_DIM // 4
    n_q = S_HEADS * S_HEAD_DIM
    n_k = S_KV_HEADS * LANES
    assert TM // IN_PARTS == SEQ
    ctx_state = {}

    def rows_of(part):
        rs = slice(part * SEQ, (part + 1) * SEQ)
        h = _modulated(x_ref[rs, :], g_ref[...], mod_ref, 1).astype(BF16)
        p = jnp.dot(h, wp_ref[...], preferred_element_type=F32)
        cos = cos_ref[rs, :]
        sin = sin_ref[rs, :]
        yield
        for t in range(n_q // LANES):
            sl = slice(t * LANES, (t + 1) * LANES)
            q_ref[rs, sl] = _rope_tile(p[:, sl], cos, sin, half).astype(BF16)
        k_tiles = [_rope_tile(p[:, n_q + g * LANES:n_q + (g + 1) * LANES], cos, sin, half) for g in range(S_KV_HEADS)]
        v_tiles = [p[:, n_q + n_k + g * LANES:n_q + n_k + (g + 1) * LANES] for g in range(S_KV_HEADS)]
        for g in range(S_KV_HEADS):
            k_ref[rs, g * LANES:(g + 1) * LANES] = k_tiles[g].astype(BF16)
            v_ref[rs, g * LANES:(g + 1) * LANES] = v_tiles[g].astype(BF16)
        ctx_state[part] = (k_tiles, v_tiles)

    _run_staged((rows_of(part) for part in range(IN_PARTS)), skew=1)

    @pl.when(pl.program_id(0) < N_CTX // TM)
    def _ctx_state():
        for bb in range(IN_PARTS):
            k_tiles, v_tiles = ctx_state[bb]
            if has_prev:
                k_ctx_ref[bb, 0] = prev_refs[0][bb]
                v_ctx_ref[bb, 0] = prev_refs[1][bb]
            for g in range(S_KV_HEADS):
                slot = (bb, 1, g) if has_prev else (bb, g)
                k_ctx_ref[slot] = k_tiles[g][:, 0:S_HEAD_DIM]
                v_ctx_ref[slot] = v_tiles[g][:, 0:S_HEAD_DIM]


def _odd_in(x, mod_l, g_pre, wp, cos_t, sin_t, prev=None):
    row = lambda i: (i, 0)
    const = lambda i: (0, 0)
    st_in, st_out, st_shape = _state_specs([(S_KV_HEADS, SEQ, S_HEAD_DIM)] * 2, prev)
    widths = (S_HEADS * S_HEAD_DIM, S_KV_HEADS * LANES, S_KV_HEADS * LANES)
    return pl.pallas_call(
        functools.partial(_odd_in_kernel, prev is not None),
        grid=(N_TOK // TM,),
        in_specs=[
            pl.BlockSpec((TM, D_MODEL), row),
            pl.BlockSpec((1, N_MOD, D_MODEL), lambda i: (_group_of_block(i, TM), 0, 0)),
            pl.BlockSpec((1, D_MODEL), const),
            pl.BlockSpec((D_MODEL, sum(widths)), const),
            pl.BlockSpec((TM, LANES), lambda i: (_rope_block(i, TM), 0)),
            pl.BlockSpec((TM, LANES), lambda i: (_rope_block(i, TM), 0)),
        ] + st_in,
        out_specs=[pl.BlockSpec((TM, w), row) for w in widths] + st_out,
        out_shape=[jax.ShapeDtypeStruct((N_TOK, w), BF16) for w in widths] + st_shape,
        compiler_params=_cparams("arbitrary"),
        name="odd_in",
    )(x, mod_l, g_pre, wp, cos_t, sin_t, *(prev or ()))


def _gqa_heads(q_ref, keys, vals, sink_ref, mask, o_ref, tq, row0=0):
    scale = S_HEAD_DIM ** -0.5
    lane = lax.broadcasted_iota(jnp.int32, (tq, LANES), 1)
    low = lane < S_HEAD_DIM
    rs = slice(row0, row0 + tq)

    def kv_group(g):
        k2, v2 = keys(g), vals(g)
        col = g * S_GROUP * S_HEAD_DIM
        tiles = [q_ref[rs, col:col + LANES].astype(F32) * scale, q_ref[rs, col + LANES:col + 2 * LANES].astype(F32) * scale]
        q4 = jnp.concatenate([jnp.where(low, tiles[0], 0.0), jnp.where(low, 0.0, tiles[0]),
                              jnp.where(low, tiles[1], 0.0), jnp.where(low, 0.0, tiles[1])], axis=0).astype(BF16)
        yield
        s = lax.dot_general(k2, q4, (((1,), (1,)), ((), ())), preferred_element_type=F32)
        if mask is not None:
            pieces, row = [], 0
            for start, keep in sorted(mask.items()):
                if start > row:
                    pieces.append(s[row:start])
                pieces.append(jnp.where(keep, s[start:start + keep.shape[0]], NEG))
                row = start + keep.shape[0]
            if row < s.shape[0]:
                pieces.append(s[row:])
            s = jnp.concatenate(pieces, axis=0)
        yield
        sk = jnp.concatenate([jnp.full((1, tq), sink_ref[g * S_GROUP + j], F32) for j in range(S_GROUP)], axis=1)
        m = jnp.maximum(jnp.max(s, axis=0, keepdims=True), sk)
        e = jnp.exp(s - m)
        yield
        l = jnp.sum(e, axis=0, keepdims=True) + jnp.exp(sk - m)
        o_t = lax.dot_general(v2, e.astype(BF16), (((0,), (0,)), ((), ())), preferred_element_type=F32)
        yield
        o4 = (o_t * (1.0 / l)).T
        o_ref[rs, col:col + LANES] = jnp.where(low, o4[0:tq], o4[tq:2 * tq]).astype(BF16)
        o_ref[rs, col + LANES:col + 2 * LANES] = jnp.where(low, o4[2 * tq:3 * tq], o4[3 * tq:4 * tq]).astype(BF16)

    return [kv_group(g) for g in range(S_KV_HEADS)]


def _gqa_ctx_kernel(sink_ref, q_ref, k_ref, v_ref, o_ref):
    groups = []
    for bb in range(S_SEQ_PER_STEP):
        rows = slice(bb * SEQ, (bb + 1) * SEQ)
        keys = lambda g, rows=rows: k_ref[rows, g * LANES:(g + 1) * LANES]
        vals = lambda g, rows=rows: v_ref[rows, g * LANES:(g + 1) * LANES]
        groups += _gqa_heads(q_ref, keys, vals, sink_ref, None, o_ref, SEQ, row0=bb * SEQ)
    _run_staged(groups)


def _gqa_ctx(q, kd, vd, sink):
    row = lambda b: (b, 0)
    return pl.pallas_call(
        _gqa_ctx_kernel,
        grid=(BATCH // S_SEQ_PER_STEP,),
        in_specs=[
            pl.BlockSpec(memory_space=pltpu.SMEM),
            pl.BlockSpec((S_SEQ_PER_STEP * SEQ, S_HEADS * S_HEAD_DIM), row),
            pl.BlockSpec((S_SEQ_PER_STEP * SEQ, S_KV_HEADS * LANES), row),
            pl.BlockSpec((S_SEQ_PER_STEP * SEQ, S_KV_HEADS * LANES), row),
        ],
        out_specs=pl.BlockSpec((S_SEQ_PER_STEP * SEQ, S_HEADS * S_HEAD_DIM), row),
        out_shape=jax.ShapeDtypeStruct((N_CTX, S_HEADS * S_HEAD_DIM), BF16),
        compiler_params=_cparams("arbitrary"),
        name="gqa_ctx",
    )(sink, q, kd, vd)


def _gqa_lat_kernel(sink_ref, q_ref, k_ref, v_ref, ck_ref, cv_ref, o_ref):
    n = pl.program_id(1)
    nb = DEC_SEQ // S_BLOCK
    B = S_BLOCK
    prev0 = pl.multiple_of(jnp.maximum(n - 1, 0) * B, B)
    cur0 = pl.multiple_of(n * B, B)
    next0 = pl.multiple_of(jnp.minimum(n + 1, nb - 1) * B, B)
    t_idx = lax.broadcasted_iota(jnp.int32, (B, S_GROUP * B), 1) % B
    s_idx = lax.broadcasted_iota(jnp.int32, (B, S_GROUP * B), 0)
    far = jnp.int32(4 * B)
    keep_prev = s_idx >= t_idx + jnp.where(n == 0, far, 0)
    keep_next = s_idx + jnp.where(n == nb - 1, far, 0) <= t_idx
    mask = {PAST_LEN: keep_prev, PAST_LEN + 2 * B: keep_next}

    def gather(ref, cache_ref, g):
        sl = slice(g * LANES, (g + 1) * LANES)
        return jnp.concatenate([cache_ref[0, g], ref[pl.ds(prev0, B), sl], ref[pl.ds(cur0, B), sl],
                                ref[pl.ds(next0, B), sl]], axis=0)

    keys = lambda g: gather(k_ref, ck_ref, g)
    vals = lambda g: gather(v_ref, cv_ref, g)
    _run_staged(_gqa_heads(q_ref, keys, vals, sink_ref, mask, o_ref, B))


def _gqa_lat(q, kd, vd, cache_k2, cache_v2, sink):
    nb = DEC_SEQ // S_BLOCK
    qb0 = N_CTX // S_BLOCK
    kb0 = N_CTX // DEC_SEQ
    return pl.pallas_call(
        _gqa_lat_kernel,
        grid=(DEC_BATCH, nb),
        in_specs=[
            pl.BlockSpec(memory_space=pltpu.SMEM),
            pl.BlockSpec((S_BLOCK, S_HEADS * S_HEAD_DIM), lambda b, n: (qb0 + b * nb + n, 0)),
            pl.BlockSpec((DEC_SEQ, S_KV_HEADS * LANES), lambda b, n: (kb0 + b, 0)),
            pl.BlockSpec((DEC_SEQ, S_KV_HEADS * LANES), lambda b, n: (kb0 + b, 0)),
            pl.BlockSpec((1, S_KV_HEADS, PAST_LEN, LANES), lambda b, n: (b, 0, 0, 0)),
            pl.BlockSpec((1, S_KV_HEADS, PAST_LEN, LANES), lambda b, n: (b, 0, 0, 0)),
        ],
        out_specs=pl.BlockSpec((S_BLOCK, S_HEADS * S_HEAD_DIM), lambda b, n: (b * nb + n, 0)),
        out_shape=jax.ShapeDtypeStruct((N_LAT, S_HEADS * S_HEAD_DIM), BF16),
        compiler_params=_cparams("arbitrary", "arbitrary"),
        name="gqa_lat",
    )(sink, q, kd, vd, cache_k2, cache_v2)


def _rope_tables(rot_dim, lane_off, reps):
    nf = rot_dim // 4
    inv = np.float32(ROPE_BASE) ** (-np.arange(nf, dtype=np.float32) / np.float32(nf))
    pos = np.arange(DEC_SEQ)
    ang_r = (pos // GRID_W).astype(np.float32)[:, None] * inv
    ang_c = (pos % GRID_W).astype(np.float32)[:, None] * inv
    cos_g = np.concatenate([np.cos(ang_r), np.cos(ang_r), np.cos(ang_c), np.cos(ang_c)], axis=1)
    sin_g = np.concatenate([-np.sin(ang_r), np.sin(ang_r), -np.sin(ang_c), np.sin(ang_c)], axis=1)
    cos_t = np.ones((TM + DEC_SEQ, LANES), np.float32)
    sin_t = np.zeros((TM + DEC_SEQ, LANES), np.float32)
    for r in range(reps):
        lo = lane_off + r * rot_dim
        cos_t[TM:, lo:lo + rot_dim] = cos_g
        sin_t[TM:, lo:lo + rot_dim] = sin_g
    return jnp.asarray(cos_t, F32), jnp.asarray(sin_t, F32)


def _even_weights(w_in, w_qb, w_kvb, gate_b):
    z = lambda n: jnp.zeros((D_MODEL, n), F32)
    idx = np.cumsum([MLA_Q_LORA, MLA_KV_LORA, MLA_ROPE, M_HEADS * M_DK, M_HEADS * M_DK, M_HEADS * M_DV,
                     M_HEADS * M_DV])
    q_a, kv_a, k_rope, mq, mk, mv, mo, mg = jnp.split(w_in, idx, axis=1)
    wp = jnp.concatenate([q_a, kv_a, z(KR_LANE), k_rope, z(LANES - KR_LANE - MLA_ROPE), mq, mk, mv, mo, mg,
                          z(LANES - 4 * M_HEADS)], axis=1).astype(BF16)
    wq = jnp.pad(w_qb.reshape(MLA_Q_LORA, MLA_HEADS, MLA_QK), ((0, 0), (0, 0), (0, LANES - MLA_QK)))
    wq = wq.reshape(MLA_Q_LORA, MLA_HEADS * LANES).astype(BF16)
    kvb = w_kvb.reshape(MLA_KV_LORA, MLA_HEADS, MLA_NOPE + MLA_V)
    wk = jnp.pad(kvb[:, :, :MLA_NOPE], ((0, 0), (0, 0), (0, LANES - MLA_NOPE))).transpose(1, 0, 2).astype(BF16)
    wv = kvb[:, :, MLA_NOPE:].reshape(MLA_KV_LORA, MLA_HEADS // 2, 2 * MLA_V).transpose(1, 0, 2).astype(BF16)
    gb = jnp.pad(gate_b, (0, LANES - 4 * M_HEADS)).reshape(1, LANES)
    return wp, wq, wk, wv, gb


def _dup_heads(w):
    w3 = w.reshape(D_MODEL, S_KV_HEADS, S_HEAD_DIM)
    return jnp.concatenate([w3, w3], axis=-1).reshape(D_MODEL, S_KV_HEADS * LANES)


def _odd_weights(w_in):
    n_q = S_HEADS * S_HEAD_DIM
    n_kv = S_KV_HEADS * S_HEAD_DIM
    return jnp.concatenate([w_in[:, :n_q], _dup_heads(w_in[:, n_q:n_q + n_kv]), _dup_heads(w_in[:, n_q + n_kv:])],
                           axis=1).astype(BF16)


def kernel(x_prompt, x_sample, cache_mla_ckv, cache_mla_krope, state_mlstm_C, state_mlstm_n, state_mlstm_m,
           cache_swa_k, cache_swa_v, c, c_ctx, ada_w, ada_b, norm_g, ffn_w_gate, ffn_w_up, ffn_w_down,
           even_w_in, mla_g_qa, mla_g_kva, mla_w_qb, mla_w_kvb, mlstm_gate_b, mlstm_g_out, even_w_out,
           odd_w_in, swa_sink, odd_w_out):
    xs = (x_prompt.reshape(N_CTX, D_MODEL), x_sample.reshape(N_LAT, D_MODEL))
    cond = jnp.concatenate([c_ctx[None], c, jnp.zeros((COND_ROWS - N_GROUPS, D_MODEL), F32)], axis=0)
    mod = _adaln(cond, ada_w, ada_b)
    cos_e, sin_e = _rope_tables(MLA_ROPE, KR_LANE, 1)
    cos_o, sin_o = _rope_tables(S_HEAD_DIM, 0, LANES // S_HEAD_DIM)
    w32 = (ffn_w_gate, ffn_w_up, ffn_w_down)
    w16 = tuple(w[0, 0].astype(BF16) for w in w32)

    assert N_EVEN == 2 and N_ODD == 2
    even_state = mlstm_state = odd_state = None
    for l in range(DEPTH):
        mod_l = mod[l]
        g = norm_g[l]
        i = l // 2
        (x,), w16 = _ffn(xs, mod_l, g[0:2], w16, 0, next_w=(w32, l, 1))
        if l % 2 == 0:
            wp, wq, wk, wv, gb = _even_weights(even_w_in[i], mla_w_qb[i], mla_w_kvb[i], mlstm_gate_b[i])
            q, ckv, kr, mq, mk, mv, mo, mg, *even_state = _even_in(
                x, mod_l, g[2:3], wp, mla_g_qa[i].reshape(1, -1), mla_g_kva[i].reshape(1, -1), wq, cos_e, sin_e,
                prev=even_state)
            cache_kr = jnp.pad(cache_mla_krope[:, i], ((0, 0), (0, 0), (KR_LANE, LANES - KR_LANE - MLA_ROPE)))
            att_c = _mla(q, ckv, kr, wk, wv, row0=0, n_batch=BATCH, t_own=SEQ, tq=SEQ)
            att_l = _mla(q, ckv, kr, wk, wv, row0=N_CTX, n_batch=DEC_BATCH, t_own=DEC_SEQ, tq=512,
                         cache=(cache_mla_ckv[:, i], cache_kr))
            g_out = mlstm_g_out[i].reshape(1, -1)
            mo_c, *mlstm_state = _mlstm(mq, mk, mv, mo, mg, gb, g_out, row0=0, n_batch=BATCH, t_len=SEQ,
                                        nb=M_CTX_SEQ_PER_STEP, prev=mlstm_state)
            n_st = 2 * M_HEADS
            state = (state_mlstm_C[:, i].reshape(DEC_BATCH, n_st, M_DK, M_DV),
                     state_mlstm_n[:, i].reshape(DEC_BATCH, n_st, M_DK),
                     jnp.broadcast_to(state_mlstm_m[:, i].reshape(DEC_BATCH, n_st, 1), (DEC_BATCH, n_st, LANES)))
            mo_l, _, _, _ = _mlstm(mq, mk, mv, mo, mg, gb, g_out, row0=N_CTX, n_batch=DEC_BATCH, t_len=DEC_SEQ,
                                   nb=M_SEQ_PER_STEP, state=state)
            w_out = even_w_out[i].astype(BF16)
            n_att = MLA_HEADS * MLA_V
            acts, act_w = [(att_c, att_l), (mo_c, mo_l)], [w_out[:n_att], w_out[n_att:]]
        else:
            q, kd, vd, *odd_state = _odd_in(x, mod_l, g[2:3], _odd_weights(odd_w_in[i]), cos_o, sin_o,
                                            prev=odd_state)
            dup = lambda a: jnp.concatenate([a, a], axis=-1).astype(BF16)
            o_c = _gqa_ctx(q, kd, vd, swa_sink[i])
            o_l = _gqa_lat(q, kd, vd, dup(cache_swa_k[:, i]), dup(cache_swa_v[:, i]), swa_sink[i])
            acts, act_w = [(o_c, o_l)], [odd_w_out[i].astype(BF16)]
        last = l == DEPTH - 1
        xs, w16 = _ffn((x,), mod_l, g[3:6], w16, 1, acts=acts, act_w=act_w, dual_out=last,
                       next_w=None if last else (w32, l + 1, 0))

    return (xs[0].reshape(BATCH, SEQ, D_MODEL), xs[1].reshape(DEC_BATCH, DEC_SEQ, D_MODEL),
            even_state[0], even_state[1],
            mlstm_state[0].reshape(BATCH, N_EVEN, 2, M_HEADS, M_DK, M_DV),
            mlstm_state[1].reshape(BATCH, N_EVEN, 2, M_HEADS, M_DK),
            mlstm_state[2][..., 0].reshape(BATCH, N_EVEN, 2, M_HEADS),
            odd_state[0], odd_state[1])
```

```python
import functools

import jax
import jax.numpy as jnp
import numpy as np
from jax import lax
from jax.experimental import pallas as pl
from jax.experimental.pallas import tpu as pltpu

F32 = jnp.float32
BF16 = jnp.bfloat16

D_MODEL = 1024
BATCH = 32
SEQ = 256
DEPTH = 4
N_EVEN = (DEPTH + 1) // 2
N_ODD = DEPTH // 2
DEC_BATCH = 4
DEC_SEQ = 1024
PAST_LEN = 512
GRID_W = 64
N_MOD = 9
D_FF = 2816
EPS = 1e-6
ROPE_BASE = 10000.0
NEG = -1e30
MLA_HEADS = 8
MLA_NOPE = 64
MLA_ROPE = 32
MLA_QK = MLA_NOPE + MLA_ROPE
MLA_V = 64
MLA_Q_LORA = 256
MLA_KV_LORA = 128
M_HEADS = 4
M_DK = 64
M_DV = 128
M_CHUNK = 128
S_HEADS = 16
S_KV_HEADS = 4
S_GROUP = S_HEADS // S_KV_HEADS
S_HEAD_DIM = 64
S_WINDOW = 128
S_BLOCK = 128
assert S_WINDOW == S_BLOCK

N_CTX = BATCH * SEQ
N_LAT = DEC_BATCH * DEC_SEQ
N_TOK = N_CTX + N_LAT
N_GROUPS = 1 + DEC_BATCH
COND_ROWS = 8

LANES = 128
FF_CHUNK = 256
N_FF_CHUNKS = D_FF // FF_CHUNK
assert N_FF_CHUNKS * FF_CHUNK == D_FF

TM = 512
ADA_TN = 1536
M_SEQ_PER_STEP = 2
M_CTX_SEQ_PER_STEP = 4
S_SEQ_PER_STEP = 2
S_LAT_BLOCKS_PER_STEP = 2
FFN_PARTS = 2
IN_PARTS = 2
VMEM_LIMIT = 52 * 1024 * 1024

E_QA, E_KVA, E_KR, E_MQ, E_MK, E_MV, E_MO, E_MG, E_COLS = 0, 256, 384, 512, 768, 1024, 1536, 2048, 2176
KR_LANE = MLA_NOPE


def _cparams(*sem):
    return pltpu.CompilerParams(dimension_semantics=sem, vmem_limit_bytes=VMEM_LIMIT)


def _group_of_block(i, tm):
    n_ctx = N_CTX // tm
    per_lat = DEC_SEQ // tm
    return jnp.where(i < n_ctx, 0, 1 + (i - n_ctx) // per_lat)


def _rope_block(i, tm):
    n_ctx = N_CTX // tm
    per_lat = DEC_SEQ // tm
    return jnp.where(i < n_ctx, 0, 1 + (i - n_ctx) % per_lat)


def _rms(x, g):
    return x * lax.rsqrt(jnp.mean(x * x, axis=-1, keepdims=True) + EPS) * g


def _modulated(x, g_pre, mod_ref, sub):
    shift = mod_ref[0, 3 * sub:3 * sub + 1, :]
    scale = mod_ref[0, 3 * sub + 1:3 * sub + 2, :]
    return _rms(x, g_pre) * (1.0 + scale) + shift


def _silu(x):
    return x * jax.nn.sigmoid(x)


def _swap_halves(x, half):
    lane = lax.broadcasted_iota(jnp.int32, x.shape, 1)
    up = pltpu.roll(x, LANES - half, 1)
    down = pltpu.roll(x, half, 1)
    return jnp.where(lane % (2 * half) < half, up, down)


def _rope_tile(x, cos, sin, half):
    return x * cos + _swap_halves(x, half) * sin


def _run_staged(stages, skew=0):
    pending, live, tick, started = list(stages), [], 0, 0
    while pending or live:
        while pending and tick >= started * skew:
            live.append(pending.pop(0))
            started += 1
        live = [s for s in live if next(s, True) is None]
        tick += 1


def _adaln_kernel(cond_ref, w_ref, b_ref, o_ref):
    s = _silu(cond_ref[...]).astype(BF16)
    o_ref[0] = jnp.dot(s, w_ref[0].astype(BF16), preferred_element_type=F32) + b_ref[0]


def _adaln(cond, ada_w, ada_b):
    n_out = N_MOD * D_MODEL
    out = pl.pallas_call(
        _adaln_kernel,
        grid=(DEPTH, n_out // ADA_TN),
        in_specs=[
            pl.BlockSpec((COND_ROWS, D_MODEL), lambda l, j: (0, 0)),
            pl.BlockSpec((1, D_MODEL, ADA_TN), lambda l, j: (l, 0, j)),
            pl.BlockSpec((1, 1, ADA_TN), lambda l, j: (l, 0, j)),
        ],
        out_specs=pl.BlockSpec((1, COND_ROWS, ADA_TN), lambda l, j: (l, 0, j)),
        out_shape=jax.ShapeDtypeStruct((DEPTH, COND_ROWS, n_out), F32),
        compiler_params=_cparams("arbitrary", "arbitrary"),
        name="adaln",
    )(cond, ada_w, ada_b.reshape(DEPTH, 1, n_out))
    return out.reshape(DEPTH, COND_ROWS, N_MOD, D_MODEL)


def _ctx_block(i):
    return jnp.minimum(i, N_CTX // TM - 1)


def _lat_block(i):
    return jnp.maximum(i - N_CTX // TM, 0)


def _dual_specs(width):
    return [pl.BlockSpec((TM, width), lambda i: (_ctx_block(i), 0)),
            pl.BlockSpec((TM, width), lambda i: (_lat_block(i), 0))]


def _ffn_kernel(sub, dual_in, dual_out, n_act, convert, *refs):
    it = iter(refs)
    take = lambda n: [next(it) for _ in range(n)]
    a_refs = take(2 * n_act)
    w_refs = take(n_act)
    x_refs = take(2 if dual_in else 1)
    mod_ref, g_ref, wg_ref, wu_ref, wd_ref = take(5)
    nxt_refs = take(3) if convert else []
    o_refs = take(2 if dual_out else 1)
    nxt_out_refs = take(3) if convert else []
    h_ref, xs_ref = take(2)
    step = pl.program_id(0)
    is_ctx = step < N_CTX // TM
    g0 = 1 if n_act else 0

    gate_row = mod_ref[0, 3 * sub + 2:3 * sub + 3, :]
    results = {}

    def rows_of(part):
        rs = slice(part * (TM // FFN_PARTS), (part + 1) * (TM // FFN_PARTS))
        x = jnp.where(is_ctx, x_refs[0][rs, :], x_refs[1][rs, :]) if dual_in else x_refs[0][rs, :]
        if n_act:
            y = None
            for k in range(n_act):
                a = jnp.where(is_ctx, a_refs[2 * k][rs, :], a_refs[2 * k + 1][rs, :])
                d = jnp.dot(a, w_refs[k][...], preferred_element_type=F32)
                y = d if y is None else y + d
            x = x + mod_ref[0, 5:6, :] * _rms(y, g_ref[0:1, :])
        xs_ref[rs, :] = x
        h = _modulated(x, g_ref[g0:g0 + 1, :], mod_ref, sub).astype(BF16)
        yield
        for c in range(N_FF_CHUNKS):
            sl = slice(c * FF_CHUNK, (c + 1) * FF_CHUNK)
            gate = jnp.dot(h, wg_ref[:, sl], preferred_element_type=F32)
            up = jnp.dot(h, wu_ref[:, sl], preferred_element_type=F32)
            h_ref[rs, sl] = (_silu(gate) * up).astype(BF16)
            yield
        y = jnp.dot(h_ref[rs, :], wd_ref[...], preferred_element_type=F32)
        yield
        results[part] = xs_ref[rs, :] + 0.5 * gate_row * _rms(y, g_ref[g0 + 1:g0 + 2, :])

    _run_staged(rows_of(part) for part in range(FFN_PARTS))
    result = jnp.concatenate([results[part] for part in range(FFN_PARTS)], axis=0)

    if dual_out:
        @pl.when(is_ctx)
        def _ctx():
            o_refs[0][...] = result

        @pl.when(jnp.logical_not(is_ctx))
        def _lat():
            o_refs[1][...] = result
    else:
        o_refs[0][...] = result

    if convert:
        @pl.when(step < N_FF_CHUNKS)
        def _next_weights():
            for src_ref, dst_ref in zip(nxt_refs, nxt_out_refs):
                dst_ref[...] = src_ref[...].astype(BF16)


def _ffn(xs, mod_l, g_rows, w16, half, *, acts=(), act_w=(), dual_out=False, next_w=None):
    dual_in = len(xs) == 2
    n_act = len(acts)
    convert = next_w is not None
    sub = 2 * half
    const = lambda i: (0, 0)
    once = dict(pipeline_mode=pl.Buffered(1))
    in_specs, args = [], []
    for a_c, a_l in acts:
        in_specs += _dual_specs(a_c.shape[1])
        args += [a_c, a_l]
    in_specs += [pl.BlockSpec(w.shape, const, **once) for w in act_w]
    args += list(act_w)
    in_specs += _dual_specs(D_MODEL) if dual_in else [pl.BlockSpec((TM, D_MODEL), lambda i: (i, 0))]
    args += list(xs)
    in_specs += [
        pl.BlockSpec((1, N_MOD, D_MODEL), lambda i: (_group_of_block(i, TM), 0, 0)),
        pl.BlockSpec(g_rows.shape, const),
        pl.BlockSpec((D_MODEL, D_FF), const, **once),
        pl.BlockSpec((D_MODEL, D_FF), const, **once),
        pl.BlockSpec((D_FF, D_MODEL), const, **once),
    ]
    args += [mod_l, g_rows, *w16]
    if dual_out:
        out_specs = _dual_specs(D_MODEL)
        out_shape = [jax.ShapeDtypeStruct((N_CTX, D_MODEL), F32), jax.ShapeDtypeStruct((N_LAT, D_MODEL), F32)]
    else:
        out_specs = [pl.BlockSpec((TM, D_MODEL), lambda i: (i, 0))]
        out_shape = [jax.ShapeDtypeStruct((N_TOK, D_MODEL), F32)]
    if convert:
        (wg32, wu32, wd32), ln, hn = next_w
        chunk = lambda i: jnp.minimum(i, N_FF_CHUNKS - 1)
        in_specs += [
            pl.BlockSpec((None, None, D_MODEL, FF_CHUNK), lambda i: (ln, hn, 0, chunk(i))),
            pl.BlockSpec((None, None, D_MODEL, FF_CHUNK), lambda i: (ln, hn, 0, chunk(i))),
            pl.BlockSpec((None, None, FF_CHUNK, D_MODEL), lambda i: (ln, hn, chunk(i), 0)),
        ]
        args += [wg32, wu32, wd32]
        out_specs += [
            pl.BlockSpec((D_MODEL, FF_CHUNK), lambda i: (0, chunk(i))),
            pl.BlockSpec((D_MODEL, FF_CHUNK), lambda i: (0, chunk(i))),
            pl.BlockSpec((FF_CHUNK, D_MODEL), lambda i: (chunk(i), 0)),
        ]
        out_shape += [jax.ShapeDtypeStruct((D_MODEL, D_FF), BF16), jax.ShapeDtypeStruct((D_MODEL, D_FF), BF16),
                      jax.ShapeDtypeStruct((D_FF, D_MODEL), BF16)]
    outs = pl.pallas_call(
        functools.partial(_ffn_kernel, sub, dual_in, dual_out, n_act, convert),
        grid=(N_TOK // TM,),
        in_specs=in_specs,
        out_specs=out_specs,
        out_shape=out_shape,
        scratch_shapes=[pltpu.VMEM((TM, D_FF), BF16), pltpu.VMEM((TM, D_MODEL), F32)],
        compiler_params=_cparams("arbitrary"),
        name="mix_ffn" if n_act else "ffn",
    )(*args)
    n_stream = 2 if dual_out else 1
    stream = tuple(outs[:n_stream])
    return stream, (tuple(outs[n_stream:]) if convert else None)


def _even_in_kernel(has_prev, x_ref, mod_ref, g_ref, wp_ref, gqa_ref, gkva_ref, wq_ref, cos_ref, sin_ref, *refs):
    prev_refs, refs = (refs[:2], refs[2:]) if has_prev else ((), refs)
    q_ref, ckv_ref, kr_ref, mq_ref, mk_ref, mv_ref, mo_ref, mg_ref, ckv_ctx_ref, kr_ctx_ref = refs
    half = MLA_ROPE // 4
    assert TM // IN_PARTS == SEQ
    ctx_state = {}

    def rows_of(part):
        rs = slice(part * SEQ, (part + 1) * SEQ)
        h = _modulated(x_ref[rs, :], g_ref[...], mod_ref, 1).astype(BF16)
        p = jnp.dot(h, wp_ref[...], preferred_element_type=F32)
        cos = cos_ref[rs, :]
        sin = sin_ref[rs, :]
        yield
        qn = _rms(p[:, E_QA:E_QA + MLA_Q_LORA], gqa_ref[...]).astype(BF16)
        q = jnp.dot(qn, wq_ref[...], preferred_element_type=F32)
        ckv = _rms(p[:, E_KVA:E_KVA + MLA_KV_LORA], gkva_ref[...])
        ckv_ref[rs, :] = ckv.astype(BF16)
        kr = _rope_tile(p[:, E_KR:E_KR + LANES], cos, sin, half)
        kr_ref[rs, :] = kr.astype(BF16)
        ctx_state[part] = (ckv, kr[:, KR_LANE:KR_LANE + MLA_ROPE])
        mq_ref[rs, :] = p[:, E_MQ:E_MK].astype(BF16)
        mk_ref[rs, :] = p[:, E_MK:E_MV].astype(BF16)
        mv_ref[rs, :] = p[:, E_MV:E_MO].astype(BF16)
        mo_ref[rs, :] = p[:, E_MO:E_MG]
        mg_ref[rs, :] = p[:, E_MG:E_COLS]
        yield
        for hd in range(MLA_HEADS):
            sl = slice(hd * LANES, (hd + 1) * LANES)
            q_ref[rs, sl] = _rope_tile(q[:, sl], cos, sin, half).astype(BF16)

    _run_staged((rows_of(part) for part in range(IN_PARTS)), skew=1)

    @pl.when(pl.program_id(0) < N_CTX // TM)
    def _ctx_state():
        for bb in range(IN_PARTS):
            ckv, kr = ctx_state[bb]
            if has_prev:
                ckv_ctx_ref[bb, 0] = prev_refs[0][bb]
                kr_ctx_ref[bb, 0] = prev_refs[1][bb]
                ckv_ctx_ref[bb, 1] = ckv
                kr_ctx_ref[bb, 1] = kr
            else:
                ckv_ctx_ref[bb] = ckv
                kr_ctx_ref[bb] = kr


def _state_specs(shapes, prev):
    nseq = TM // SEQ
    zeros = lambda s: (0,) * len(s)
    in_specs = [pl.BlockSpec((nseq,) + s, lambda i, s=s: (_ctx_block(i),) + zeros(s)) for s in shapes] if prev else []
    lead = (2,) if prev else ()
    out_specs = [pl.BlockSpec((nseq,) + lead + s, lambda i, s=s: (_ctx_block(i),) + zeros(lead + s)) for s in shapes]
    out_shape = [jax.ShapeDtypeStruct((BATCH,) + lead + s, F32) for s in shapes]
    return in_specs, out_specs, out_shape


def _even_in(x, mod_l, g_pre, wp, g_qa, g_kva, wq, cos_t, sin_t, prev=None):
    row = lambda i: (i, 0)
    const = lambda i: (0, 0)
    st_in, st_out, st_shape = _state_specs([(SEQ, MLA_KV_LORA), (SEQ, MLA_ROPE)], prev)
    widths = (MLA_HEADS * LANES, LANES, LANES, M_HEADS * M_DK, M_HEADS * M_DK, M_HEADS * M_DV,
              M_HEADS * M_DV, LANES)
    dtypes = (BF16, BF16, BF16, BF16, BF16, BF16, F32, F32)
    return pl.pallas_call(
        functools.partial(_even_in_kernel, prev is not None),
        grid=(N_TOK // TM,),
        in_specs=[
            pl.BlockSpec((TM, D_MODEL), row),
            pl.BlockSpec((1, N_MOD, D_MODEL), lambda i: (_group_of_block(i, TM), 0, 0)),
            pl.BlockSpec((1, D_MODEL), const),
            pl.BlockSpec((D_MODEL, E_COLS), const),
            pl.BlockSpec((1, MLA_Q_LORA), const),
            pl.BlockSpec((1, MLA_KV_LORA), const),
            pl.BlockSpec((MLA_Q_LORA, MLA_HEADS * LANES), const),
            pl.BlockSpec((TM, LANES), lambda i: (_rope_block(i, TM), 0)),
            pl.BlockSpec((TM, LANES), lambda i: (_rope_block(i, TM), 0)),
        ] + st_in,
        out_specs=[pl.BlockSpec((TM, w), row) for w in widths] + st_out,
        out_shape=[jax.ShapeDtypeStruct((N_TOK, w), dt) for w, dt in zip(widths, dtypes)] + st_shape,
        compiler_params=_cparams("arbitrary"),
        name="even_in",
    )(x, mod_l, g_pre, wp, g_qa, g_kva, wq, cos_t, sin_t, *(prev or ()))


def _mla_kernel(n_cache, tq, *refs):
    if n_cache:
        q_ref, ckv_ref, kr_ref, cckv_ref, ckr_ref, wk_ref, wv_ref, o_ref, ks_ref, vs_ref = refs
    else:
        q_ref, ckv_ref, kr_ref, wk_ref, wv_ref, o_ref, ks_ref, vs_ref = refs

    @pl.when(pl.program_id(1) == 0)
    def _expand():
        if n_cache:
            ckv16 = jnp.concatenate([cckv_ref[0].astype(BF16), ckv_ref[...]], axis=0)
            kr = jnp.concatenate([ckr_ref[0].astype(BF16), kr_ref[...]], axis=0)
        else:
            ckv16, kr = ckv_ref[...], kr_ref[...]
        for hd in range(MLA_HEADS):
            ks_ref[hd] = (jnp.dot(ckv16, wk_ref[hd], preferred_element_type=F32) + kr).astype(BF16)
        for pr in range(MLA_HEADS // 2):
            vs_ref[pr] = jnp.dot(ckv16, wv_ref[pr], preferred_element_type=F32).astype(BF16)

    scale = MLA_QK ** -0.5
    lane = lax.broadcasted_iota(jnp.int32, (tq, LANES), 1)
    for pr in range(MLA_HEADS // 2):
        outs = []
        for j in range(2):
            hd = 2 * pr + j
            qh = q_ref[:, hd * LANES:(hd + 1) * LANES]
            s = lax.dot_general(qh, ks_ref[hd], (((1,), (1,)), ((), ())), preferred_element_type=F32) * scale
            m = jnp.max(s, axis=-1, keepdims=True)
            e = jnp.exp(s - m)
            l = jnp.sum(e, axis=-1, keepdims=True)
            outs.append(jnp.dot(e.astype(BF16), vs_ref[pr], preferred_element_type=F32) / l)
        o_ref[:, pr * LANES:(pr + 1) * LANES] = jnp.where(lane < MLA_V, outs[0], outs[1]).astype(BF16)


def _mla(q, ckv, kr, wk, wv, *, row0, n_batch, t_own, tq, cache=None):
    n_cache = 0 if cache is None else cache[0].shape[1]
    nq = t_own // tq
    qb0 = row0 // tq
    kb0 = row0 // t_own
    in_specs = [
        pl.BlockSpec((tq, MLA_HEADS * LANES), lambda b, j: (qb0 + b * nq + j, 0)),
        pl.BlockSpec((t_own, LANES), lambda b, j: (kb0 + b, 0)),
        pl.BlockSpec((t_own, LANES), lambda b, j: (kb0 + b, 0)),
    ]
    args = [q, ckv, kr]
    if n_cache:
        in_specs += [pl.BlockSpec((1, n_cache, LANES), lambda b, j: (b, 0, 0))] * 2
        args += list(cache)
    in_specs += [
        pl.BlockSpec((MLA_HEADS, LANES, LANES), lambda b, j: (0, 0, 0)),
        pl.BlockSpec((MLA_HEADS // 2, LANES, LANES), lambda b, j: (0, 0, 0)),
    ]
    args += [wk, wv]
    t_keys = n_cache + t_own
    return pl.pallas_call(
        functools.partial(_mla_kernel, n_cache, tq),
        grid=(n_batch, nq),
        in_specs=in_specs,
        out_specs=pl.BlockSpec((tq, MLA_HEADS * MLA_V), lambda b, j: (b * nq + j, 0)),
        out_shape=jax.ShapeDtypeStruct((n_batch * t_own, MLA_HEADS * MLA_V), BF16),
        scratch_shapes=[pltpu.VMEM((MLA_HEADS, t_keys, LANES), BF16),
                        pltpu.VMEM((MLA_HEADS // 2, t_keys, LANES), BF16)],
        compiler_params=_cparams("arbitrary", "arbitrary"),
        name="mla_lat" if n_cache else "mla_ctx",
    )(*args)


def _log_sigmoid(x):
    return jnp.minimum(x, 0.0) - jnp.log1p(jnp.exp(-jnp.abs(x)))


def _mlstm_chunk(rev, r0, st, hd, allowed, cum, e_rows, e_cols, mq_ref, mk_ref, mv_ref, ct_ref, ns_ref, ms_ref,
                 h_ref):
    L = M_CHUNK
    nt = (((1,), (1,)), ((), ()))
    gi = (2 * M_HEADS if rev else 0) + hd
    gf = gi + M_HEADS
    edge = 0 if rev else L - 1
    rows = pl.ds(r0, L)
    q16 = mq_ref[rows, hd * M_DK:(hd + 1) * M_DK]
    k16 = mk_ref[rows, hd * M_DK:(hd + 1) * M_DK] * (M_DK ** -0.5)
    v_t16 = mv_ref[rows, hd * M_DV:(hd + 1) * M_DV].T
    v_t = v_t16.astype(F32)
    ct_prev = ct_ref[st, :, 0:M_DK]
    n_prev = ns_ref[st, 0:1, 0:M_DK]
    m_prev = ms_ref[st, 0:1, 0:1]
    yield

    b_row = cum[gf:gf + 1, :]
    e_row = e_rows[gi:gi + 1, :]
    e_col = e_cols[:, gi:gi + 1]
    dmat = jnp.where(allowed, b_row + e_col, NEG)
    inter = m_prev + b_row
    mt = jnp.maximum(inter, jnp.max(dmat, axis=0, keepdims=True))
    w_inter = jnp.exp(inter - mt)
    qk = lax.dot_general(k16, q16, nt, preferred_element_type=F32)
    a = qk * jnp.exp(dmat - mt)
    yield
    n8 = jnp.broadcast_to(n_prev, (8, M_DK)).astype(BF16)
    nq = lax.dot_general(n8, q16, nt, preferred_element_type=F32)[0:1]
    num = w_inter * lax.dot_general(ct_prev.astype(BF16), q16, nt, preferred_element_type=F32) \
        + jnp.dot(v_t16, a.astype(BF16), preferred_element_type=F32)
    den = w_inter * nq + jnp.sum(a, axis=0, keepdims=True)
    h_t = num * (1.0 / jnp.maximum(jnp.abs(den), jnp.exp(-mt)))
    h_ref[rows, hd * M_DV:(hd + 1) * M_DV] = h_t.T
    yield

    m_new = mt[:, edge:edge + 1]
    b_last = b_row[:, edge:edge + 1]
    w_s = jnp.exp(b_last + e_row - m_new)
    w_c = jnp.exp(m_prev + b_last - m_new)
    ct_ref[st, :, 0:M_DK] = w_c * ct_prev + jnp.dot((v_t * w_s).astype(BF16), k16, preferred_element_type=F32)
    w8 = jnp.broadcast_to(w_s, (8, L)).astype(BF16)
    ns_ref[st, 0:1, 0:M_DK] = w_c * n_prev + jnp.dot(w8, k16, preferred_element_type=F32)[0:1]
    ms_ref[st, 0:1, :] = jnp.broadcast_to(m_new, (1, LANES))


def _mlstm_kernel(t_len, has_state, has_prev, nb, *refs):
    n_st = 2 * M_HEADS
    prev_refs = ()
    if has_prev:
        prev_refs, refs = refs[7:10], refs[:7] + refs[10:]
    if has_state:
        (mq_ref, mk_ref, mv_ref, mo_ref, mg_ref, gb_ref, gout_ref, c0_ref, n0_ref, m0_ref,
         out_ref, c_ref, n_ref, m_ref, hf_ref, hb_ref, ct_ref, ns_ref, ms_ref) = refs
        pad = jnp.zeros((LANES - M_DK, M_DV), F32)
        for bb in range(nb):
            for st in range(n_st):
                ct_ref[bb * n_st + st] = jnp.concatenate([c0_ref[bb, st], pad], axis=0).T
                ns_ref[bb * n_st + st, 0:1, 0:M_DK] = n0_ref[bb, st:st + 1, :]
                ms_ref[bb * n_st + st, 0:1, :] = m0_ref[bb, st:st + 1, :]
    else:
        (mq_ref, mk_ref, mv_ref, mo_ref, mg_ref, gb_ref, gout_ref,
         out_ref, c_ref, n_ref, m_ref, hf_ref, hb_ref, ct_ref, ns_ref, ms_ref) = refs
        ct_ref[...] = jnp.zeros_like(ct_ref)
        ns_ref[...] = jnp.zeros_like(ns_ref)
        ms_ref[...] = jnp.zeros_like(ms_ref)
    L = M_CHUNK
    nc = t_len // L
    r_idx = lax.broadcasted_iota(jnp.int32, (L, L), 0)
    c_idx = lax.broadcasted_iota(jnp.int32, (L, L), 1)
    lower = c_idx <= r_idx
    upper = c_idx >= r_idx
    lane = lax.broadcasted_iota(jnp.int32, (L, LANES), 1)
    is_forget = (lane % (2 * M_HEADS)) >= M_HEADS
    n_gate = 4 * M_HEADS
    zeros_below = jnp.zeros((L - n_gate, L), F32)

    def gate_sums(r0, rev):
        g = mg_ref[pl.ds(r0, L), :] + gb_ref[...]
        g = jnp.where(is_forget, _log_sigmoid(g), g)
        rows = g.T[0:n_gate, :]
        tri = (lower if rev else upper).astype(BF16)
        hi = rows.astype(BF16)
        rest = rows - hi.astype(F32)
        mid = rest.astype(BF16)
        lo = (rest - mid.astype(F32)).astype(BF16)
        cum = (jnp.dot(hi, tri, preferred_element_type=F32) + jnp.dot(mid, tri, preferred_element_type=F32)
               + jnp.dot(lo, tri, preferred_element_type=F32))
        e_rows = rows - pltpu.roll(cum, n_gate - M_HEADS, 0)
        e_cols = jnp.concatenate([e_rows, zeros_below], axis=0).T
        return cum, e_rows, e_cols

    def step(c, carry):
        chunks = []
        for bb in range(nb):
            for rev in (False, True):
                r0 = pl.multiple_of(bb * t_len + (nc - 1 - c if rev else c) * L, L)
                cum, e_rows, e_cols = gate_sums(r0, rev)
                allowed = lower if rev else upper
                for hd in range(M_HEADS):
                    st = bb * n_st + (M_HEADS if rev else 0) + hd
                    chunks.append(_mlstm_chunk(rev, r0, st, hd, allowed, cum, e_rows, e_cols, mq_ref, mk_ref, mv_ref,
                                               ct_ref, ns_ref, ms_ref, hb_ref if rev else hf_ref))
        _run_staged(chunks)
        return carry

    lax.fori_loop(0, nc, step, 0)
    for bb in range(nb):
        if has_prev:
            for o_ref, p_ref in zip((c_ref, n_ref, m_ref), prev_refs):
                o_ref[bb, 0] = p_ref[bb]
        own = (bb, 1) if has_prev else (bb,)
        for st in range(n_st):
            c_ref[own + (st,)] = ct_ref[bb * n_st + st].T[0:M_DK, :]
            n_ref[own + (slice(st, st + 1), slice(None))] = ns_ref[bb * n_st + st, 0:1, 0:M_DK]
            m_ref[own + (slice(st, st + 1), slice(None))] = ms_ref[bb * n_st + st, 0:1, :]
    for hd in range(M_HEADS):
        sl = slice(hd * M_DV, (hd + 1) * M_DV)
        hm = hf_ref[:, sl] + hb_ref[:, sl]
        hm = hm * lax.rsqrt(jnp.mean(hm * hm, axis=-1, keepdims=True) + EPS) * gout_ref[:, sl]
        out_ref[:, sl] = (jax.nn.sigmoid(mo_ref[:, sl]) * hm).astype(BF16)


def _mlstm(mq, mk, mv, mo, mg, gate_b, g_out, *, row0, n_batch, t_len, nb, state=None, prev=None):
    rows = nb * t_len
    kb0 = row0 // rows
    row = lambda b: (kb0 + b, 0)
    const = lambda b: (0, 0)
    n_st = 2 * M_HEADS
    in_specs = [
        pl.BlockSpec((rows, M_HEADS * M_DK), row),
        pl.BlockSpec((rows, M_HEADS * M_DK), row),
        pl.BlockSpec((rows, M_HEADS * M_DV), row),
        pl.BlockSpec((rows, M_HEADS * M_DV), row),
        pl.BlockSpec((rows, LANES), row),
        pl.BlockSpec((1, LANES), const),
        pl.BlockSpec((1, M_HEADS * M_DV), const),
    ]
    args = [mq, mk, mv, mo, mg, gate_b, g_out]
    state_shapes = [(n_st, M_DK, M_DV), (n_st, M_DK), (n_st, LANES)]
    zeros = lambda s: (0,) * len(s)
    state_specs = [pl.BlockSpec((nb,) + s, lambda b, s=s: (b,) + zeros(s)) for s in state_shapes]
    lead = (2,) if prev is not None else ()
    out_state_specs = [pl.BlockSpec((nb,) + lead + s, lambda b, s=s: (b,) + zeros(lead + s)) for s in state_shapes]
    if prev is not None:
        in_specs += state_specs
        args += list(prev)
    if state is not None:
        in_specs += state_specs
        args += list(state)
    return pl.pallas_call(
        functools.partial(_mlstm_kernel, t_len, state is not None, prev is not None, nb),
        grid=(n_batch // nb,),
        in_specs=in_specs,
        out_specs=[pl.BlockSpec((rows, M_HEADS * M_DV), lambda b: (b, 0))] + out_state_specs,
        out_shape=[jax.ShapeDtypeStruct((n_batch * t_len, M_HEADS * M_DV), BF16)]
        + [jax.ShapeDtypeStruct((n_batch,) + lead + s, F32) for s in state_shapes],
        scratch_shapes=[pltpu.VMEM((rows, M_HEADS * M_DV), F32), pltpu.VMEM((rows, M_HEADS * M_DV), F32),
                        pltpu.VMEM((nb * n_st, M_DV, LANES), F32),
                        pltpu.VMEM((nb * n_st, 8, LANES), F32), pltpu.VMEM((nb * n_st, 8, LANES), F32)],
        compiler_params=_cparams("arbitrary"),
        name="mlstm_lat" if state is not None else "mlstm_ctx",
    )(*args)


def _odd_in_kernel(has_prev, x_ref, mod_ref, g_ref, wp_ref, cos_ref, sin_ref, *refs):
    prev_refs, refs = (refs[:2], refs[2:]) if has_prev else ((), refs)
    q_ref, k_ref, v_ref, k_ctx_ref, v_ctx_ref = refs
    half = S_HEAD_DIM // 4
    n_q = S_HEADS * S_HEAD_DIM
    n_k = S_KV_HEADS * LANES
    assert TM // IN_PARTS == SEQ
    ctx_state = {}

    def rows_of(part):
        rs = slice(part * SEQ, (part + 1) * SEQ)
        h = _modulated(x_ref[rs, :], g_ref[...], mod_ref, 1).astype(BF16)
        p = jnp.dot(h, wp_ref[...], preferred_element_type=F32)
        cos = cos_ref[rs, :]
        sin = sin_ref[rs, :]
        yield
        for t in range(n_q // LANES):
            sl = slice(t * LANES, (t + 1) * LANES)
            q_ref[rs, sl] = _rope_tile(p[:, sl], cos, sin, half).astype(BF16)
        k_tiles = [_rope_tile(p[:, n_q + g * LANES:n_q + (g + 1) * LANES], cos, sin, half) for g in range(S_KV_HEADS)]
        v_tiles = [p[:, n_q + n_k + g * LANES:n_q + n_k + (g + 1) * LANES] for g in range(S_KV_HEADS)]
        for g in range(S_KV_HEADS):
            k_ref[rs, g * LANES:(g + 1) * LANES] = k_tiles[g].astype(BF16)
            v_ref[rs, g * LANES:(g + 1) * LANES] = v_tiles[g].astype(BF16)
        ctx_state[part] = (k_tiles, v_tiles)

    _run_staged((rows_of(part) for part in range(IN_PARTS)), skew=1)

    @pl.when(pl.program_id(0) < N_CTX // TM)
    def _ctx_state():
        for bb in range(IN_PARTS):
            k_tiles, v_tiles = ctx_state[bb]
            if has_prev:
                k_ctx_ref[bb, 0] = prev_refs[0][bb]
                v_ctx_ref[bb, 0] = prev_refs[1][bb]
            for g in range(S_KV_HEADS):
                slot = (bb, 1, g) if has_prev else (bb, g)
                k_ctx_ref[slot] = k_tiles[g][:, 0:S_HEAD_DIM]
                v_ctx_ref[slot] = v_tiles[g][:, 0:S_HEAD_DIM]


def _odd_in(x, mod_l, g_pre, wp, cos_t, sin_t, prev=None):
    row = lambda i: (i, 0)
    const = lambda i: (0, 0)
    st_in, st_out, st_shape = _state_specs([(S_KV_HEADS, SEQ, S_HEAD_DIM)] * 2, prev)
    widths = (S_HEADS * S_HEAD_DIM, S_KV_HEADS * LANES, S_KV_HEADS * LANES)
    return pl.pallas_call(
        functools.partial(_odd_in_kernel, prev is not None),
        grid=(N_TOK // TM,),
        in_specs=[
            pl.BlockSpec((TM, D_MODEL), row),
            pl.BlockSpec((1, N_MOD, D_MODEL), lambda i: (_group_of_block(i, TM), 0, 0)),
            pl.BlockSpec((1, D_MODEL), const),
            pl.BlockSpec((D_MODEL, sum(widths)), const),
            pl.BlockSpec((TM, LANES), lambda i: (_rope_block(i, TM), 0)),
            pl.BlockSpec((TM, LANES), lambda i: (_rope_block(i, TM), 0)),
        ] + st_in,
        out_specs=[pl.BlockSpec((TM, w), row) for w in widths] + st_out,
        out_shape=[jax.ShapeDtypeStruct((N_TOK, w), BF16) for w in widths] + st_shape,
        compiler_params=_cparams("arbitrary"),
        name="odd_in",
    )(x, mod_l, g_pre, wp, cos_t, sin_t, *(prev or ()))


def _gqa_heads(q_ref, keys, vals, sink_ref, mask, o_ref, tq, row0=0):
    scale = S_HEAD_DIM ** -0.5
    lane = lax.broadcasted_iota(jnp.int32, (tq, LANES), 1)
    low = lane < S_HEAD_DIM
    rs = slice(row0, row0 + tq)

    def kv_group(g):
        k2, v2 = keys(g), vals(g)
        col = g * S_GROUP * S_HEAD_DIM
        tiles = [q_ref[rs, col:col + LANES].astype(F32) * scale, q_ref[rs, col + LANES:col + 2 * LANES].astype(F32) * scale]
        q4 = jnp.concatenate([jnp.where(low, tiles[0], 0.0), jnp.where(low, 0.0, tiles[0]),
                              jnp.where(low, tiles[1], 0.0), jnp.where(low, 0.0, tiles[1])], axis=0).astype(BF16)
        yield
        s = lax.dot_general(k2, q4, (((1,), (1,)), ((), ())), preferred_element_type=F32)
        if mask is not None:
            pieces, row = [], 0
            for start, keep in sorted(mask.items()):
                if start > row:
                    pieces.append(s[row:start])
                pieces.append(jnp.where(keep, s[start:start + keep.shape[0]], NEG))
                row = start + keep.shape[0]
            if row < s.shape[0]:
                pieces.append(s[row:])
            s = jnp.concatenate(pieces, axis=0)
        yield
        sk = jnp.concatenate([jnp.full((1, tq), sink_ref[g * S_GROUP + j], F32) for j in range(S_GROUP)], axis=1)
        m = jnp.maximum(jnp.max(s, axis=0, keepdims=True), sk)
        e = jnp.exp(s - m)
        yield
        l = jnp.sum(e, axis=0, keepdims=True) + jnp.exp(sk - m)
        o_t = lax.dot_general(v2, e.astype(BF16), (((0,), (0,)), ((), ())), preferred_element_type=F32)
        yield
        o4 = (o_t * (1.0 / l)).T
        o_ref[rs, col:col + LANES] = jnp.where(low, o4[0:tq], o4[tq:2 * tq]).astype(BF16)
        o_ref[rs, col + LANES:col + 2 * LANES] = jnp.where(low, o4[2 * tq:3 * tq], o4[3 * tq:4 * tq]).astype(BF16)

    return [kv_group(g) for g in range(S_KV_HEADS)]


def _gqa_ctx_kernel(sink_ref, q_ref, k_ref, v_ref, o_ref):
    groups = []
    for bb in range(S_SEQ_PER_STEP):
        rows = slice(bb * SEQ, (bb + 1) * SEQ)
        keys = lambda g, rows=rows: k_ref[rows, g * LANES:(g + 1) * LANES]
        vals = lambda g, rows=rows: v_ref[rows, g * LANES:(g + 1) * LANES]
        groups += _gqa_heads(q_ref, keys, vals, sink_ref, None, o_ref, SEQ, row0=bb * SEQ)
    _run_staged(groups)


def _gqa_ctx(q, kd, vd, sink):
    row = lambda b: (b, 0)
    return pl.pallas_call(
        _gqa_ctx_kernel,
        grid=(BATCH // S_SEQ_PER_STEP,),
        in_specs=[
            pl.BlockSpec(memory_space=pltpu.SMEM),
            pl.BlockSpec((S_SEQ_PER_STEP * SEQ, S_HEADS * S_HEAD_DIM), row),
            pl.BlockSpec((S_SEQ_PER_STEP * SEQ, S_KV_HEADS * LANES), row),
            pl.BlockSpec((S_SEQ_PER_STEP * SEQ, S_KV_HEADS * LANES), row),
        ],
        out_specs=pl.BlockSpec((S_SEQ_PER_STEP * SEQ, S_HEADS * S_HEAD_DIM), row),
        out_shape=jax.ShapeDtypeStruct((N_CTX, S_HEADS * S_HEAD_DIM), BF16),
        compiler_params=_cparams("arbitrary"),
        name="gqa_ctx",
    )(sink, q, kd, vd)


def _gqa_lat_kernel(sink_ref, q_ref, k_ref, v_ref, ck_ref, cv_ref, o_ref):
    nb = DEC_SEQ // S_BLOCK
    B = S_BLOCK
    t_idx = lax.broadcasted_iota(jnp.int32, (B, S_GROUP * B), 1) % B
    s_idx = lax.broadcasted_iota(jnp.int32, (B, S_GROUP * B), 0)
    far = jnp.int32(4 * B)
    groups = []
    for qb in range(S_LAT_BLOCKS_PER_STEP):
        n = pl.program_id(1) * S_LAT_BLOCKS_PER_STEP + qb
        prev0 = pl.multiple_of(jnp.maximum(n - 1, 0) * B, B)
        cur0 = pl.multiple_of(n * B, B)
        next0 = pl.multiple_of(jnp.minimum(n + 1, nb - 1) * B, B)
        keep_prev = s_idx >= t_idx + jnp.where(n == 0, far, 0)
        keep_next = s_idx + jnp.where(n == nb - 1, far, 0) <= t_idx
        mask = {PAST_LEN: keep_prev, PAST_LEN + 2 * B: keep_next}

        def gather(ref, cache_ref, g, starts=(prev0, cur0, next0)):
            sl = slice(g * LANES, (g + 1) * LANES)
            return jnp.concatenate([cache_ref[0, g]] + [ref[pl.ds(r0, B), sl] for r0 in starts], axis=0)

        keys = lambda g, gather=gather: gather(k_ref, ck_ref, g)
        vals = lambda g, gather=gather: gather(v_ref, cv_ref, g)
        groups += _gqa_heads(q_ref, keys, vals, sink_ref, mask, o_ref, B, row0=qb * B)
    _run_staged(groups)


def _gqa_lat(q, kd, vd, cache_k2, cache_v2, sink):
    rows = S_LAT_BLOCKS_PER_STEP * S_BLOCK
    nb = DEC_SEQ // rows
    qb0 = N_CTX // rows
    kb0 = N_CTX // DEC_SEQ
    return pl.pallas_call(
        _gqa_lat_kernel,
        grid=(DEC_BATCH, nb),
        in_specs=[
            pl.BlockSpec(memory_space=pltpu.SMEM),
            pl.BlockSpec((rows, S_HEADS * S_HEAD_DIM), lambda b, n: (qb0 + b * nb + n, 0)),
            pl.BlockSpec((DEC_SEQ, S_KV_HEADS * LANES), lambda b, n: (kb0 + b, 0)),
            pl.BlockSpec((DEC_SEQ, S_KV_HEADS * LANES), lambda b, n: (kb0 + b, 0)),
            pl.BlockSpec((1, S_KV_HEADS, PAST_LEN, LANES), lambda b, n: (b, 0, 0, 0)),
            pl.BlockSpec((1, S_KV_HEADS, PAST_LEN, LANES), lambda b, n: (b, 0, 0, 0)),
        ],
        out_specs=pl.BlockSpec((rows, S_HEADS * S_HEAD_DIM), lambda b, n: (b * nb + n, 0)),
        out_shape=jax.ShapeDtypeStruct((N_LAT, S_HEADS * S_HEAD_DIM), BF16),
        compiler_params=_cparams("arbitrary", "arbitrary"),
        name="gqa_lat",
    )(sink, q, kd, vd, cache_k2, cache_v2)


def _rope_tables(rot_dim, lane_off, reps):
    nf = rot_dim // 4
    inv = np.float32(ROPE_BASE) ** (-np.arange(nf, dtype=np.float32) / np.float32(nf))
    pos = np.arange(DEC_SEQ)
    ang_r = (pos // GRID_W).astype(np.float32)[:, None] * inv
    ang_c = (pos % GRID_W).astype(np.float32)[:, None] * inv
    cos_g = np.concatenate([np.cos(ang_r), np.cos(ang_r), np.cos(ang_c), np.cos(ang_c)], axis=1)
    sin_g = np.concatenate([-np.sin(ang_r), np.sin(ang_r), -np.sin(ang_c), np.sin(ang_c)], axis=1)
    cos_t = np.ones((TM + DEC_SEQ, LANES), np.float32)
    sin_t = np.zeros((TM + DEC_SEQ, LANES), np.float32)
    for r in range(reps):
        lo = lane_off + r * rot_dim
        cos_t[TM:, lo:lo + rot_dim] = cos_g
        sin_t[TM:, lo:lo + rot_dim] = sin_g
    return jnp.asarray(cos_t, F32), jnp.asarray(sin_t, F32)


def _even_weights(w_in, w_qb, w_kvb, gate_b):
    z = lambda n: jnp.zeros((D_MODEL, n), F32)
    idx = np.cumsum([MLA_Q_LORA, MLA_KV_LORA, MLA_ROPE, M_HEADS * M_DK, M_HEADS * M_DK, M_HEADS * M_DV,
                     M_HEADS * M_DV])
    q_a, kv_a, k_rope, mq, mk, mv, mo, mg = jnp.split(w_in, idx, axis=1)
    wp = jnp.concatenate([q_a, kv_a, z(KR_LANE), k_rope, z(LANES - KR_LANE - MLA_ROPE), mq, mk, mv, mo, mg,
                          z(LANES - 4 * M_HEADS)], axis=1).astype(BF16)
    wq = jnp.pad(w_qb.reshape(MLA_Q_LORA, MLA_HEADS, MLA_QK), ((0, 0), (0, 0), (0, LANES - MLA_QK)))
    wq = wq.reshape(MLA_Q_LORA, MLA_HEADS * LANES).astype(BF16)
    kvb = w_kvb.reshape(MLA_KV_LORA, MLA_HEADS, MLA_NOPE + MLA_V)
    wk = jnp.pad(kvb[:, :, :MLA_NOPE], ((0, 0), (0, 0), (0, LANES - MLA_NOPE))).transpose(1, 0, 2).astype(BF16)
    wv = kvb[:, :, MLA_NOPE:].reshape(MLA_KV_LORA, MLA_HEADS // 2, 2 * MLA_V).transpose(1, 0, 2).astype(BF16)
    gb = jnp.pad(gate_b, (0, LANES - 4 * M_HEADS)).reshape(1, LANES)
    return wp, wq, wk, wv, gb


def _dup_heads(w):
    w3 = w.reshape(D_MODEL, S_KV_HEADS, S_HEAD_DIM)
    return jnp.concatenate([w3, w3], axis=-1).reshape(D_MODEL, S_KV_HEADS * LANES)


def _odd_weights(w_in):
    n_q = S_HEADS * S_HEAD_DIM
    n_kv = S_KV_HEADS * S_HEAD_DIM
    return jnp.concatenate([w_in[:, :n_q], _dup_heads(w_in[:, n_q:n_q + n_kv]), _dup_heads(w_in[:, n_q + n_kv:])],
                           axis=1).astype(BF16)


def kernel(x_prompt, x_sample, cache_mla_ckv, cache_mla_krope, state_mlstm_C, state_mlstm_n, state_mlstm_m,
           cache_swa_k, cache_swa_v, c, c_ctx, ada_w, ada_b, norm_g, ffn_w_gate, ffn_w_up, ffn_w_down,
           even_w_in, mla_g_qa, mla_g_kva, mla_w_qb, mla_w_kvb, mlstm_gate_b, mlstm_g_out, even_w_out,
           odd_w_in, swa_sink, odd_w_out):
    xs = (x_prompt.reshape(N_CTX, D_MODEL), x_sample.reshape(N_LAT, D_MODEL))
    cond = jnp.concatenate([c_ctx[None], c, jnp.zeros((COND_ROWS - N_GROUPS, D_MODEL), F32)], axis=0)
    mod = _adaln(cond, ada_w, ada_b)
    cos_e, sin_e = _rope_tables(MLA_ROPE, KR_LANE, 1)
    cos_o, sin_o = _rope_tables(S_HEAD_DIM, 0, LANES // S_HEAD_DIM)
    w32 = (ffn_w_gate, ffn_w_up, ffn_w_down)
    w16 = tuple(w[0, 0].astype(BF16) for w in w32)

    assert N_EVEN == 2 and N_ODD == 2
    even_state = mlstm_state = odd_state = None
    for l in range(DEPTH):
        mod_l = mod[l]
        g = norm_g[l]
        i = l // 2
        (x,), w16 = _ffn(xs, mod_l, g[0:2], w16, 0, next_w=(w32, l, 1))
        if l % 2 == 0:
            wp, wq, wk, wv, gb = _even_weights(even_w_in[i], mla_w_qb[i], mla_w_kvb[i], mlstm_gate_b[i])
            q, ckv, kr, mq, mk, mv, mo, mg, *even_state = _even_in(
                x, mod_l, g[2:3], wp, mla_g_qa[i].reshape(1, -1), mla_g_kva[i].reshape(1, -1), wq, cos_e, sin_e,
                prev=even_state)
            cache_kr = jnp.pad(cache_mla_krope[:, i], ((0, 0), (0, 0), (KR_LANE, LANES - KR_LANE - MLA_ROPE)))
            att_c = _mla(q, ckv, kr, wk, wv, row0=0, n_batch=BATCH, t_own=SEQ, tq=SEQ)
            att_l = _mla(q, ckv, kr, wk, wv, row0=N_CTX, n_batch=DEC_BATCH, t_own=DEC_SEQ, tq=512,
                         cache=(cache_mla_ckv[:, i], cache_kr))
            g_out = mlstm_g_out[i].reshape(1, -1)
            mo_c, *mlstm_state = _mlstm(mq, mk, mv, mo, mg, gb, g_out, row0=0, n_batch=BATCH, t_len=SEQ,
                                        nb=M_CTX_SEQ_PER_STEP, prev=mlstm_state)
            n_st = 2 * M_HEADS
            state = (state_mlstm_C[:, i].reshape(DEC_BATCH, n_st, M_DK, M_DV),
                     state_mlstm_n[:, i].reshape(DEC_BATCH, n_st, M_DK),
                     jnp.broadcast_to(state_mlstm_m[:, i].reshape(DEC_BATCH, n_st, 1), (DEC_BATCH, n_st, LANES)))
            mo_l, _, _, _ = _mlstm(mq, mk, mv, mo, mg, gb, g_out, row0=N_CTX, n_batch=DEC_BATCH, t_len=DEC_SEQ,
                                   nb=M_SEQ_PER_STEP, state=state)
            w_out = even_w_out[i].astype(BF16)
            n_att = MLA_HEADS * MLA_V
            acts, act_w = [(att_c, att_l), (mo_c, mo_l)], [w_out[:n_att], w_out[n_att:]]
        else:
            q, kd, vd, *odd_state = _odd_in(x, mod_l, g[2:3], _odd_weights(odd_w_in[i]), cos_o, sin_o,
                                            prev=odd_state)
            dup = lambda a: jnp.concatenate([a, a], axis=-1).astype(BF16)
            o_c = _gqa_ctx(q, kd, vd, swa_sink[i])
            o_l = _gqa_lat(q, kd, vd, dup(cache_swa_k[:, i]), dup(cache_swa_v[:, i]), swa_sink[i])
            acts, act_w = [(o_c, o_l)], [odd_w_out[i].astype(BF16)]
        last = l == DEPTH - 1
        xs, w16 = _ffn((x,), mod_l, g[3:6], w16, 1, acts=acts, act_w=act_w, dual_out=last,
                       next_w=None if last else (w32, l + 1, 0))

    return (xs[0].reshape(BATCH, SEQ, D_MODEL), xs[1].reshape(DEC_BATCH, DEC_SEQ, D_MODEL),
            even_state[0], even_state[1],
            mlstm_state[0].reshape(BATCH, N_EVEN, 2, M_HEADS, M_DK, M_DV),
            mlstm_state[1].reshape(BATCH, N_EVEN, 2, M_HEADS, M_DK),
            mlstm_state[2][..., 0].reshape(BATCH, N_EVEN, 2, M_HEADS),
            odd_state[0], odd_state[1])
```

```python
import functools

import jax
import jax.numpy as jnp
import numpy as np
from jax import lax
from jax.experimental import pallas as pl
from jax.experimental.pallas import tpu as pltpu

F32 = jnp.float32
BF16 = jnp.bfloat16

D_MODEL = 1024
BATCH = 32
SEQ = 256
DEPTH = 4
N_EVEN = (DEPTH + 1) // 2
N_ODD = DEPTH // 2
DEC_BATCH = 4
DEC_SEQ = 1024
PAST_LEN = 512
GRID_W = 64
N_MOD = 9
D_FF = 2816
EPS = 1e-6
ROPE_BASE = 10000.0
NEG = -1e30
MLA_HEADS = 8
MLA_NOPE = 64
MLA_ROPE = 32
MLA_QK = MLA_NOPE + MLA_ROPE
MLA_V = 64
MLA_Q_LORA = 256
MLA_KV_LORA = 128
M_HEADS = 4
M_DK = 64
M_DV = 128
M_CHUNK = 128
S_HEADS = 16
S_KV_HEADS = 4
S_GROUP = S_HEADS // S_KV_HEADS
S_HEAD_DIM = 64
S_WINDOW = 128
S_BLOCK = 128
assert S_WINDOW == S_BLOCK

N_CTX = BATCH * SEQ
N_LAT = DEC_BATCH * DEC_SEQ
N_TOK = N_CTX + N_LAT
N_GROUPS = 1 + DEC_BATCH
COND_ROWS = 8

LANES = 128
FF_CHUNK = 256
N_FF_CHUNKS = D_FF // FF_CHUNK
assert N_FF_CHUNKS * FF_CHUNK == D_FF

TM = 512
ADA_TN = 1536
M_SEQ_PER_STEP = 2
M_CTX_SEQ_PER_STEP = 4
S_SEQ_PER_STEP = 4
S_LAT_BLOCKS_PER_STEP = 4
FFN_PARTS = 2
IN_PARTS = 2
VMEM_LIMIT = 52 * 1024 * 1024

E_QA = 0
E_KVA = E_QA + MLA_Q_LORA
E_KR = E_KVA + MLA_KV_LORA
E_MQ = E_KR + LANES
E_MK = E_MQ + M_HEADS * M_DK
E_MV = E_MK + M_HEADS * M_DK
E_MO = E_MV + M_HEADS * M_DV
E_MG = E_MO + M_HEADS * M_DV
E_COLS = E_MG + LANES
KR_LANE = MLA_NOPE


def _cparams(*sem):
    return pltpu.CompilerParams(dimension_semantics=sem, vmem_limit_bytes=VMEM_LIMIT)


def _group_of_block(i, tm):
    n_ctx = N_CTX // tm
    per_lat = DEC_SEQ // tm
    return jnp.where(i < n_ctx, 0, 1 + (i - n_ctx) // per_lat)


def _rope_block(i, tm):
    n_ctx = N_CTX // tm
    per_lat = DEC_SEQ // tm
    return jnp.where(i < n_ctx, 0, 1 + (i - n_ctx) % per_lat)


def _rms(x, g):
    return x * lax.rsqrt(jnp.mean(x * x, axis=-1, keepdims=True) + EPS) * g


def _modulated(x, g_pre, mod_ref, sub):
    shift = mod_ref[0, 3 * sub:3 * sub + 1, :]
    scale = mod_ref[0, 3 * sub + 1:3 * sub + 2, :]
    return _rms(x, g_pre) * (1.0 + scale) + shift


def _silu(x):
    return x * jax.nn.sigmoid(x)


def _swap_halves(x, half):
    lane = lax.broadcasted_iota(jnp.int32, x.shape, 1)
    up = pltpu.roll(x, LANES - half, 1)
    down = pltpu.roll(x, half, 1)
    return jnp.where(lane % (2 * half) < half, up, down)


def _rope_tile(x, cos, sin, half):
    return x * cos + _swap_halves(x, half) * sin


def _run_staged(stages, skew=0):
    pending, live, tick, started = list(stages), [], 0, 0
    while pending or live:
        while pending and tick >= started * skew:
            live.append(pending.pop(0))
            started += 1
        live = [s for s in live if next(s, True) is None]
        tick += 1


def _adaln_kernel(cond_ref, w_ref, b_ref, o_ref):
    s = _silu(cond_ref[...]).astype(BF16)
    o_ref[0] = jnp.dot(s, w_ref[0].astype(BF16), preferred_element_type=F32) + b_ref[0]


def _adaln(cond, ada_w, ada_b):
    n_out = N_MOD * D_MODEL
    out = pl.pallas_call(
        _adaln_kernel,
        grid=(DEPTH, n_out // ADA_TN),
        in_specs=[
            pl.BlockSpec((COND_ROWS, D_MODEL), lambda l, j: (0, 0)),
            pl.BlockSpec((1, D_MODEL, ADA_TN), lambda l, j: (l, 0, j)),
            pl.BlockSpec((1, 1, ADA_TN), lambda l, j: (l, 0, j)),
        ],
        out_specs=pl.BlockSpec((1, COND_ROWS, ADA_TN), lambda l, j: (l, 0, j)),
        out_shape=jax.ShapeDtypeStruct((DEPTH, COND_ROWS, n_out), F32),
        compiler_params=_cparams("arbitrary", "arbitrary"),
        name="adaln",
    )(cond, ada_w, ada_b.reshape(DEPTH, 1, n_out))
    return out.reshape(DEPTH, COND_ROWS, N_MOD, D_MODEL)


def _ctx_block(i):
    return jnp.minimum(i, N_CTX // TM - 1)


def _lat_block(i):
    return jnp.maximum(i - N_CTX // TM, 0)


def _dual_specs(width):
    return [pl.BlockSpec((TM, width), lambda i: (_ctx_block(i), 0)),
            pl.BlockSpec((TM, width), lambda i: (_lat_block(i), 0))]


def _ffn_kernel(sub, dual_in, dual_out, n_act, convert, *refs):
    it = iter(refs)
    take = lambda n: [next(it) for _ in range(n)]
    a_refs = take(2 * n_act)
    w_refs = take(n_act)
    x_refs = take(2 if dual_in else 1)
    mod_ref, g_ref, wg_ref, wu_ref, wd_ref = take(5)
    nxt_refs = take(3) if convert else []
    o_refs = take(2 if dual_out else 1)
    nxt_out_refs = take(3) if convert else []
    h_ref, xs_ref = take(2)
    step = pl.program_id(0)
    is_ctx = step < N_CTX // TM
    g0 = 1 if n_act else 0

    gate_row = mod_ref[0, 3 * sub + 2:3 * sub + 3, :]
    results = {}

    def rows_of(part):
        rs = slice(part * (TM // FFN_PARTS), (part + 1) * (TM // FFN_PARTS))
        x = jnp.where(is_ctx, x_refs[0][rs, :], x_refs[1][rs, :]) if dual_in else x_refs[0][rs, :]
        if n_act:
            y = None
            for k in range(n_act):
                a = jnp.where(is_ctx, a_refs[2 * k][rs, :], a_refs[2 * k + 1][rs, :])
                d = jnp.dot(a, w_refs[k][...], preferred_element_type=F32)
                y = d if y is None else y + d
            x = x + mod_ref[0, 5:6, :] * _rms(y, g_ref[0:1, :])
        xs_ref[rs, :] = x
        h = _modulated(x, g_ref[g0:g0 + 1, :], mod_ref, sub).astype(BF16)
        yield
        for c in range(N_FF_CHUNKS):
            sl = slice(c * FF_CHUNK, (c + 1) * FF_CHUNK)
            gate = jnp.dot(h, wg_ref[:, sl], preferred_element_type=F32)
            up = jnp.dot(h, wu_ref[:, sl], preferred_element_type=F32)
            h_ref[rs, sl] = (_silu(gate) * up).astype(BF16)
            yield
        y = jnp.dot(h_ref[rs, :], wd_ref[...], preferred_element_type=F32)
        yield
        results[part] = xs_ref[rs, :] + 0.5 * gate_row * _rms(y, g_ref[g0 + 1:g0 + 2, :])

    _run_staged(rows_of(part) for part in range(FFN_PARTS))
    result = jnp.concatenate([results[part] for part in range(FFN_PARTS)], axis=0)

    if dual_out:
        @pl.when(is_ctx)
        def _ctx():
            o_refs[0][...] = result

        @pl.when(jnp.logical_not(is_ctx))
        def _lat():
            o_refs[1][...] = result
    else:
        o_refs[0][...] = result

    if convert:
        @pl.when(step < N_FF_CHUNKS)
        def _next_weights():
            for src_ref, dst_ref in zip(nxt_refs, nxt_out_refs):
                dst_ref[...] = src_ref[...].astype(BF16)


def _ffn(xs, mod_l, g_rows, w16, half, *, acts=(), act_w=(), dual_out=False, next_w=None):
    dual_in = len(xs) == 2
    n_act = len(acts)
    convert = next_w is not None
    sub = 2 * half
    const = lambda i: (0, 0)
    once = dict(pipeline_mode=pl.Buffered(1))
    in_specs, args = [], []
    for a_c, a_l in acts:
        in_specs += _dual_specs(a_c.shape[1])
        args += [a_c, a_l]
    in_specs += [pl.BlockSpec(w.shape, const, **once) for w in act_w]
    args += list(act_w)
    in_specs += _dual_specs(D_MODEL) if dual_in else [pl.BlockSpec((TM, D_MODEL), lambda i: (i, 0))]
    args += list(xs)
    in_specs += [
        pl.BlockSpec((1, N_MOD, D_MODEL), lambda i: (_group_of_block(i, TM), 0, 0)),
        pl.BlockSpec(g_rows.shape, const),
        pl.BlockSpec((D_MODEL, D_FF), const, **once),
        pl.BlockSpec((D_MODEL, D_FF), const, **once),
        pl.BlockSpec((D_FF, D_MODEL), const, **once),
    ]
    args += [mod_l, g_rows, *w16]
    if dual_out:
        out_specs = _dual_specs(D_MODEL)
        out_shape = [jax.ShapeDtypeStruct((N_CTX, D_MODEL), F32), jax.ShapeDtypeStruct((N_LAT, D_MODEL), F32)]
    else:
        out_specs = [pl.BlockSpec((TM, D_MODEL), lambda i: (i, 0))]
        out_shape = [jax.ShapeDtypeStruct((N_TOK, D_MODEL), F32)]
    if convert:
        (wg32, wu32, wd32), ln, hn = next_w
        chunk = lambda i: jnp.minimum(i, N_FF_CHUNKS - 1)
        in_specs += [
            pl.BlockSpec((None, None, D_MODEL, FF_CHUNK), lambda i: (ln, hn, 0, chunk(i))),
            pl.BlockSpec((None, None, D_MODEL, FF_CHUNK), lambda i: (ln, hn, 0, chunk(i))),
            pl.BlockSpec((None, None, FF_CHUNK, D_MODEL), lambda i: (ln, hn, chunk(i), 0)),
        ]
        args += [wg32, wu32, wd32]
        out_specs += [
            pl.BlockSpec((D_MODEL, FF_CHUNK), lambda i: (0, chunk(i))),
            pl.BlockSpec((D_MODEL, FF_CHUNK), lambda i: (0, chunk(i))),
            pl.BlockSpec((FF_CHUNK, D_MODEL), lambda i: (chunk(i), 0)),
        ]
        out_shape += [jax.ShapeDtypeStruct((D_MODEL, D_FF), BF16), jax.ShapeDtypeStruct((D_MODEL, D_FF), BF16),
                      jax.ShapeDtypeStruct((D_FF, D_MODEL), BF16)]
    outs = pl.pallas_call(
        functools.partial(_ffn_kernel, sub, dual_in, dual_out, n_act, convert),
        grid=(N_TOK // TM,),
        in_specs=in_specs,
        out_specs=out_specs,
        out_shape=out_shape,
        scratch_shapes=[pltpu.VMEM((TM, D_FF), BF16), pltpu.VMEM((TM, D_MODEL), F32)],
        compiler_params=_cparams("arbitrary"),
        name="mix_ffn" if n_act else "ffn",
    )(*args)
    n_stream = 2 if dual_out else 1
    stream = tuple(outs[:n_stream])
    return stream, (tuple(outs[n_stream:]) if convert else None)


def _even_in_kernel(has_prev, x_ref, mod_ref, g_ref, wp_ref, gqa_ref, gkva_ref, wq_ref, cos_ref, sin_ref, *refs):
    prev_refs, refs = (refs[:2], refs[2:]) if has_prev else ((), refs)
    q_ref, ckv_ref, kr_ref, mq_ref, mk_ref, mv_ref, mo_ref, mg_ref, ckv_ctx_ref, kr_ctx_ref = refs
    half = MLA_ROPE // 4
    assert TM // IN_PARTS == SEQ
    ctx_state = {}

    def rows_of(part):
        rs = slice(part * SEQ, (part + 1) * SEQ)
        h = _modulated(x_ref[rs, :], g_ref[...], mod_ref, 1).astype(BF16)
        p = jnp.dot(h, wp_ref[...], preferred_element_type=F32)
        cos = cos_ref[rs, :]
        sin = sin_ref[rs, :]
        yield
        qn = _rms(p[:, E_QA:E_QA + MLA_Q_LORA], gqa_ref[...]).astype(BF16)
        q = jnp.dot(qn, wq_ref[...], preferred_element_type=F32)
        ckv = _rms(p[:, E_KVA:E_KVA + MLA_KV_LORA], gkva_ref[...])
        ckv_ref[rs, :] = ckv.astype(BF16)
        kr = _rope_tile(p[:, E_KR:E_KR + LANES], cos, sin, half)
        kr_ref[rs, :] = kr.astype(BF16)
        ctx_state[part] = (ckv, kr[:, KR_LANE:KR_LANE + MLA_ROPE])
        mq_ref[rs, :] = p[:, E_MQ:E_MK].astype(BF16)
        mk_ref[rs, :] = p[:, E_MK:E_MV].astype(BF16)
        mv_ref[rs, :] = p[:, E_MV:E_MO].astype(BF16)
        mo_ref[rs, :] = p[:, E_MO:E_MG]
        mg_ref[rs, :] = p[:, E_MG:E_COLS]
        yield
        for hd in range(MLA_HEADS):
            sl = slice(hd * LANES, (hd + 1) * LANES)
            q_ref[rs, sl] = _rope_tile(q[:, sl], cos, sin, half).astype(BF16)

    _run_staged((rows_of(part) for part in range(IN_PARTS)), skew=1)

    @pl.when(pl.program_id(0) < N_CTX // TM)
    def _ctx_state():
        for bb in range(IN_PARTS):
            ckv, kr = ctx_state[bb]
            if has_prev:
                ckv_ctx_ref[bb, 0] = prev_refs[0][bb]
                kr_ctx_ref[bb, 0] = prev_refs[1][bb]
                ckv_ctx_ref[bb, 1] = ckv
                kr_ctx_ref[bb, 1] = kr
            else:
                ckv_ctx_ref[bb] = ckv
                kr_ctx_ref[bb] = kr


def _state_specs(shapes, prev):
    nseq = TM // SEQ
    zeros = lambda s: (0,) * len(s)
    in_specs = [pl.BlockSpec((nseq,) + s, lambda i, s=s: (_ctx_block(i),) + zeros(s)) for s in shapes] if prev else []
    lead = (2,) if prev else ()
    out_specs = [pl.BlockSpec((nseq,) + lead + s, lambda i, s=s: (_ctx_block(i),) + zeros(lead + s)) for s in shapes]
    out_shape = [jax.ShapeDtypeStruct((BATCH,) + lead + s, F32) for s in shapes]
    return in_specs, out_specs, out_shape


def _even_in(x, mod_l, g_pre, wp, g_qa, g_kva, wq, cos_t, sin_t, prev=None):
    row = lambda i: (i, 0)
    const = lambda i: (0, 0)
    st_in, st_out, st_shape = _state_specs([(SEQ, MLA_KV_LORA), (SEQ, MLA_ROPE)], prev)
    widths = (MLA_HEADS * LANES, LANES, LANES, M_HEADS * M_DK, M_HEADS * M_DK, M_HEADS * M_DV,
              M_HEADS * M_DV, LANES)
    dtypes = (BF16, BF16, BF16, BF16, BF16, BF16, F32, F32)
    return pl.pallas_call(
        functools.partial(_even_in_kernel, prev is not None),
        grid=(N_TOK // TM,),
        in_specs=[
            pl.BlockSpec((TM, D_MODEL), row),
            pl.BlockSpec((1, N_MOD, D_MODEL), lambda i: (_group_of_block(i, TM), 0, 0)),
            pl.BlockSpec((1, D_MODEL), const),
            pl.BlockSpec((D_MODEL, E_COLS), const),
            pl.BlockSpec((1, MLA_Q_LORA), const),
            pl.BlockSpec((1, MLA_KV_LORA), const),
            pl.BlockSpec((MLA_Q_LORA, MLA_HEADS * LANES), const),
            pl.BlockSpec((TM, LANES), lambda i: (_rope_block(i, TM), 0)),
            pl.BlockSpec((TM, LANES), lambda i: (_rope_block(i, TM), 0)),
        ] + st_in,
        out_specs=[pl.BlockSpec((TM, w), row) for w in widths] + st_out,
        out_shape=[jax.ShapeDtypeStruct((N_TOK, w), dt) for w, dt in zip(widths, dtypes)] + st_shape,
        compiler_params=_cparams("arbitrary"),
        name="even_in",
    )(x, mod_l, g_pre, wp, g_qa, g_kva, wq, cos_t, sin_t, *(prev or ()))


def _mla_kernel(n_cache, tq, *refs):
    if n_cache:
        q_ref, ckv_ref, kr_ref, cckv_ref, ckr_ref, wk_ref, wv_ref, o_ref, ks_ref, vs_ref = refs
    else:
        q_ref, ckv_ref, kr_ref, wk_ref, wv_ref, o_ref, ks_ref, vs_ref = refs

    @pl.when(pl.program_id(1) == 0)
    def _expand():
        if n_cache:
            ckv16 = jnp.concatenate([cckv_ref[0].astype(BF16), ckv_ref[...]], axis=0)
            kr = jnp.concatenate([ckr_ref[0].astype(BF16), kr_ref[...]], axis=0)
        else:
            ckv16, kr = ckv_ref[...], kr_ref[...]
        for hd in range(MLA_HEADS):
            ks_ref[hd] = (jnp.dot(ckv16, wk_ref[hd], preferred_element_type=F32) + kr).astype(BF16)
        for pr in range(MLA_HEADS // 2):
            vs_ref[pr] = jnp.dot(ckv16, wv_ref[pr], preferred_element_type=F32).astype(BF16)

    scale = MLA_QK ** -0.5
    lane = lax.broadcasted_iota(jnp.int32, (tq, LANES), 1)
    for pr in range(MLA_HEADS // 2):
        outs = []
        for j in range(2):
            hd = 2 * pr + j
            qh = q_ref[:, hd * LANES:(hd + 1) * LANES]
            s = lax.dot_general(qh, ks_ref[hd], (((1,), (1,)), ((), ())), preferred_element_type=F32) * scale
            m = jnp.max(s, axis=-1, keepdims=True)
            e = jnp.exp(s - m)
            l = jnp.sum(e, axis=-1, keepdims=True)
            outs.append(jnp.dot(e.astype(BF16), vs_ref[pr], preferred_element_type=F32) / l)
        o_ref[:, pr * LANES:(pr + 1) * LANES] = jnp.where(lane < MLA_V, outs[0], outs[1]).astype(BF16)


def _mla(q, ckv, kr, wk, wv, *, row0, n_batch, t_own, tq, cache=None):
    n_cache = 0 if cache is None else cache[0].shape[1]
    nq = t_own // tq
    qb0 = row0 // tq
    kb0 = row0 // t_own
    in_specs = [
        pl.BlockSpec((tq, MLA_HEADS * LANES), lambda b, j: (qb0 + b * nq + j, 0)),
        pl.BlockSpec((t_own, LANES), lambda b, j: (kb0 + b, 0)),
        pl.BlockSpec((t_own, LANES), lambda b, j: (kb0 + b, 0)),
    ]
    args = [q, ckv, kr]
    if n_cache:
        in_specs += [pl.BlockSpec((1, n_cache, LANES), lambda b, j: (b, 0, 0))] * 2
        args += list(cache)
    in_specs += [
        pl.BlockSpec((MLA_HEADS, LANES, LANES), lambda b, j: (0, 0, 0)),
        pl.BlockSpec((MLA_HEADS // 2, LANES, LANES), lambda b, j: (0, 0, 0)),
    ]
    args += [wk, wv]
    t_keys = n_cache + t_own
    return pl.pallas_call(
        functools.partial(_mla_kernel, n_cache, tq),
        grid=(n_batch, nq),
        in_specs=in_specs,
        out_specs=pl.BlockSpec((tq, MLA_HEADS * MLA_V), lambda b, j: (b * nq + j, 0)),
        out_shape=jax.ShapeDtypeStruct((n_batch * t_own, MLA_HEADS * MLA_V), BF16),
        scratch_shapes=[pltpu.VMEM((MLA_HEADS, t_keys, LANES), BF16),
                        pltpu.VMEM((MLA_HEADS // 2, t_keys, LANES), BF16)],
        compiler_params=_cparams("arbitrary", "arbitrary"),
        name="mla_lat" if n_cache else "mla_ctx",
    )(*args)


def _log_sigmoid(x):
    return jnp.minimum(x, 0.0) - jnp.log1p(jnp.exp(-jnp.abs(x)))


def _mlstm_chunk(rev, r0, st, hd, allowed, cum, e_rows, e_cols, mq_ref, mk_ref, mv_ref, ct_ref, ns_ref, ms_ref,
                 h_ref):
    L = M_CHUNK
    nt = (((1,), (1,)), ((), ()))
    gi = (2 * M_HEADS if rev else 0) + hd
    gf = gi + M_HEADS
    edge = 0 if rev else L - 1
    rows = pl.ds(r0, L)
    q16 = mq_ref[rows, hd * M_DK:(hd + 1) * M_DK]
    k16 = mk_ref[rows, hd * M_DK:(hd + 1) * M_DK] * (M_DK ** -0.5)
    v_t16 = mv_ref[rows, hd * M_DV:(hd + 1) * M_DV].T
    v_t = v_t16.astype(F32)
    ct_prev = ct_ref[st, :, 0:M_DK]
    n_prev = ns_ref[st, 0:1, 0:M_DK]
    m_prev = ms_ref[st, 0:1, 0:1]
    yield

    b_row = cum[gf:gf + 1, :]
    e_row = e_rows[gi:gi + 1, :]
    e_col = e_cols[:, gi:gi + 1]
    dmat = jnp.where(allowed, b_row + e_col, NEG)
    inter = m_prev + b_row
    mt = jnp.maximum(inter, jnp.max(dmat, axis=0, keepdims=True))
    w_inter = jnp.exp(inter - mt)
    qk = lax.dot_general(k16, q16, nt, preferred_element_type=F32)
    a = qk * jnp.exp(dmat - mt)
    yield
    n8 = jnp.broadcast_to(n_prev, (8, M_DK)).astype(BF16)
    nq = lax.dot_general(n8, q16, nt, preferred_element_type=F32)[0:1]
    num = w_inter * lax.dot_general(ct_prev.astype(BF16), q16, nt, preferred_element_type=F32) \
        + jnp.dot(v_t16, a.astype(BF16), preferred_element_type=F32)
    den = w_inter * nq + jnp.sum(a, axis=0, keepdims=True)
    h_t = num * (1.0 / jnp.maximum(jnp.abs(den), jnp.exp(-mt)))
    h_ref[rows, hd * M_DV:(hd + 1) * M_DV] = h_t.T
    yield

    m_new = mt[:, edge:edge + 1]
    b_last = b_row[:, edge:edge + 1]
    w_s = jnp.exp(b_last + e_row - m_new)
    w_c = jnp.exp(m_prev + b_last - m_new)
    ct_ref[st, :, 0:M_DK] = w_c * ct_prev + jnp.dot((v_t * w_s).astype(BF16), k16, preferred_element_type=F32)
    w8 = jnp.broadcast_to(w_s, (8, L)).astype(BF16)
    ns_ref[st, 0:1, 0:M_DK] = w_c * n_prev + jnp.dot(w8, k16, preferred_element_type=F32)[0:1]
    ms_ref[st, 0:1, :] = jnp.broadcast_to(m_new, (1, LANES))


def _mlstm_kernel(t_len, has_state, has_prev, nb, *refs):
    n_st = 2 * M_HEADS
    prev_refs = ()
    if has_prev:
        prev_refs, refs = refs[7:10], refs[:7] + refs[10:]
    if has_state:
        (mq_ref, mk_ref, mv_ref, mo_ref, mg_ref, gb_ref, gout_ref, c0_ref, n0_ref, m0_ref,
         out_ref, c_ref, n_ref, m_ref, hf_ref, hb_ref, ct_ref, ns_ref, ms_ref) = refs
        pad = jnp.zeros((LANES - M_DK, M_DV), F32)
        for bb in range(nb):
            for st in range(n_st):
                ct_ref[bb * n_st + st] = jnp.concatenate([c0_ref[bb, st], pad], axis=0).T
                ns_ref[bb * n_st + st, 0:1, 0:M_DK] = n0_ref[bb, st:st + 1, :]
                ms_ref[bb * n_st + st, 0:1, :] = m0_ref[bb, st:st + 1, :]
    else:
        (mq_ref, mk_ref, mv_ref, mo_ref, mg_ref, gb_ref, gout_ref,
         out_ref, c_ref, n_ref, m_ref, hf_ref, hb_ref, ct_ref, ns_ref, ms_ref) = refs
        ct_ref[...] = jnp.zeros_like(ct_ref)
        ns_ref[...] = jnp.zeros_like(ns_ref)
        ms_ref[...] = jnp.zeros_like(ms_ref)
    L = M_CHUNK
    nc = t_len // L
    r_idx = lax.broadcasted_iota(jnp.int32, (L, L), 0)
    c_idx = lax.broadcasted_iota(jnp.int32, (L, L), 1)
    lower = c_idx <= r_idx
    upper = c_idx >= r_idx
    lane = lax.broadcasted_iota(jnp.int32, (L, LANES), 1)
    is_forget = (lane % (2 * M_HEADS)) >= M_HEADS
    n_gate = 4 * M_HEADS
    zeros_below = jnp.zeros((L - n_gate, L), F32)

    def gate_sums(r0, rev):
        g = mg_ref[pl.ds(r0, L), :] + gb_ref[...]
        g = jnp.where(is_forget, _log_sigmoid(g), g)
        rows = g.T[0:n_gate, :]
        tri = (lower if rev else upper).astype(BF16)
        hi = rows.astype(BF16)
        rest = rows - hi.astype(F32)
        mid = rest.astype(BF16)
        lo = (rest - mid.astype(F32)).astype(BF16)
        cum = (jnp.dot(hi, tri, preferred_element_type=F32) + jnp.dot(mid, tri, preferred_element_type=F32)
               + jnp.dot(lo, tri, preferred_element_type=F32))
        e_rows = rows - pltpu.roll(cum, n_gate - M_HEADS, 0)
        e_cols = jnp.concatenate([e_rows, zeros_below], axis=0).T
        return cum, e_rows, e_cols

    def step(c, carry):
        chunks = []
        for bb in range(nb):
            for rev in (False, True):
                r0 = pl.multiple_of(bb * t_len + (nc - 1 - c if rev else c) * L, L)
                cum, e_rows, e_cols = gate_sums(r0, rev)
                allowed = lower if rev else upper
                for hd in range(M_HEADS):
                    st = bb * n_st + (M_HEADS if rev else 0) + hd
                    chunks.append(_mlstm_chunk(rev, r0, st, hd, allowed, cum, e_rows, e_cols, mq_ref, mk_ref, mv_ref,
                                               ct_ref, ns_ref, ms_ref, hb_ref if rev else hf_ref))
        _run_staged(chunks)
        return carry

    lax.fori_loop(0, nc, step, 0)
    for bb in range(nb):
        if has_prev:
            for o_ref, p_ref in zip((c_ref, n_ref, m_ref), prev_refs):
                o_ref[bb, 0] = p_ref[bb]
        own = (bb, 1) if has_prev else (bb,)
        for st in range(n_st):
            c_ref[own + (st,)] = ct_ref[bb * n_st + st].T[0:M_DK, :]
            n_ref[own + (slice(st, st + 1), slice(None))] = ns_ref[bb * n_st + st, 0:1, 0:M_DK]
            m_ref[own + (slice(st, st + 1), slice(None))] = ms_ref[bb * n_st + st, 0:1, :]
    for hd in range(M_HEADS):
        sl = slice(hd * M_DV, (hd + 1) * M_DV)
        hm = hf_ref[:, sl] + hb_ref[:, sl]
        hm = hm * lax.rsqrt(jnp.mean(hm * hm, axis=-1, keepdims=True) + EPS) * gout_ref[:, sl]
        out_ref[:, sl] = (jax.nn.sigmoid(mo_ref[:, sl]) * hm).astype(BF16)


def _mlstm(mq, mk, mv, mo, mg, gate_b, g_out, *, row0, n_batch, t_len, nb, state=None, prev=None):
    rows = nb * t_len
    kb0 = row0 // rows
    row = lambda b: (kb0 + b, 0)
    const = lambda b: (0, 0)
    n_st = 2 * M_HEADS
    in_specs = [
        pl.BlockSpec((rows, M_HEADS * M_DK), row),
        pl.BlockSpec((rows, M_HEADS * M_DK), row),
        pl.BlockSpec((rows, M_HEADS * M_DV), row),
        pl.BlockSpec((rows, M_HEADS * M_DV), row),
        pl.BlockSpec((rows, LANES), row),
        pl.BlockSpec((1, LANES), const),
        pl.BlockSpec((1, M_HEADS * M_DV), const),
    ]
    args = [mq, mk, mv, mo, mg, gate_b, g_out]
    state_shapes = [(n_st, M_DK, M_DV), (n_st, M_DK), (n_st, LANES)]
    zeros = lambda s: (0,) * len(s)
    state_specs = [pl.BlockSpec((nb,) + s, lambda b, s=s: (b,) + zeros(s)) for s in state_shapes]
    lead = (2,) if prev is not None else ()
    out_state_specs = [pl.BlockSpec((nb,) + lead + s, lambda b, s=s: (b,) + zeros(lead + s)) for s in state_shapes]
    if prev is not None:
        in_specs += state_specs
        args += list(prev)
    if state is not None:
        in_specs += state_specs
        args += list(state)
    return pl.pallas_call(
        functools.partial(_mlstm_kernel, t_len, state is not None, prev is not None, nb),
        grid=(n_batch // nb,),
        in_specs=in_specs,
        out_specs=[pl.BlockSpec((rows, M_HEADS * M_DV), lambda b: (b, 0))] + out_state_specs,
        out_shape=[jax.ShapeDtypeStruct((n_batch * t_len, M_HEADS * M_DV), BF16)]
        + [jax.ShapeDtypeStruct((n_batch,) + lead + s, F32) for s in state_shapes],
        scratch_shapes=[pltpu.VMEM((rows, M_HEADS * M_DV), F32), pltpu.VMEM((rows, M_HEADS * M_DV), F32),
                        pltpu.VMEM((nb * n_st, M_DV, LANES), F32),
                        pltpu.VMEM((nb * n_st, 8, LANES), F32), pltpu.VMEM((nb * n_st, 8, LANES), F32)],
        compiler_params=_cparams("arbitrary"),
        name="mlstm_lat" if state is not None else "mlstm_ctx",
    )(*args)


def _odd_in_kernel(has_prev, x_ref, mod_ref, g_ref, wp_ref, cos_ref, sin_ref, *refs):
    prev_refs, refs = (refs[:2], refs[2:]) if has_prev else ((), refs)
    q_ref, k_ref, v_ref, k_ctx_ref, v_ctx_ref = refs
    half = S_HEAD_DIM // 4
    n_q = S_HEADS * S_HEAD_DIM
    n_k = S_KV_HEADS * LANES
    assert TM // IN_PARTS == SEQ
    ctx_state = {}

    def rows_of(part):
        rs = slice(part * SEQ, (part + 1) * SEQ)
        h = _modulated(x_ref[rs, :], g_ref[...], mod_ref, 1).astype(BF16)
        p = jnp.dot(h, wp_ref[...], preferred_element_type=F32)
        cos = cos_ref[rs, :]
        sin = sin_ref[rs, :]
        yield
        for t in range(n_q // LANES):
            sl = slice(t * LANES, (t + 1) * LANES)
            q_ref[rs, sl] = _rope_tile(p[:, sl], cos, sin, half).astype(BF16)
        k_tiles = [_rope_tile(p[:, n_q + g * LANES:n_q + (g + 1) * LANES], cos, sin, half) for g in range(S_KV_HEADS)]
        v_tiles = [p[:, n_q + n_k + g * LANES:n_q + n_k + (g + 1) * LANES] for g in range(S_KV_HEADS)]
        for g in range(S_KV_HEADS):
            k_ref[rs, g * LANES:(g + 1) * LANES] = k_tiles[g].astype(BF16)
            v_ref[rs, g * LANES:(g + 1) * LANES] = v_tiles[g].astype(BF16)
        ctx_state[part] = (k_tiles, v_tiles)

    _run_staged((rows_of(part) for part in range(IN_PARTS)), skew=1)

    @pl.when(pl.program_id(0) < N_CTX // TM)
    def _ctx_state():
        for bb in range(IN_PARTS):
            k_tiles, v_tiles = ctx_state[bb]
            if has_prev:
                k_ctx_ref[bb, 0] = prev_refs[0][bb]
                v_ctx_ref[bb, 0] = prev_refs[1][bb]
            for g in range(S_KV_HEADS):
                slot = (bb, 1, g) if has_prev else (bb, g)
                k_ctx_ref[slot] = k_tiles[g][:, 0:S_HEAD_DIM]
                v_ctx_ref[slot] = v_tiles[g][:, 0:S_HEAD_DIM]


def _odd_in(x, mod_l, g_pre, wp, cos_t, sin_t, prev=None):
    row = lambda i: (i, 0)
    const = lambda i: (0, 0)
    st_in, st_out, st_shape = _state_specs([(S_KV_HEADS, SEQ, S_HEAD_DIM)] * 2, prev)
    widths = (S_HEADS * S_HEAD_DIM, S_KV_HEADS * LANES, S_KV_HEADS * LANES)
    return pl.pallas_call(
        functools.partial(_odd_in_kernel, prev is not None),
        grid=(N_TOK // TM,),
        in_specs=[
            pl.BlockSpec((TM, D_MODEL), row),
            pl.BlockSpec((1, N_MOD, D_MODEL), lambda i: (_group_of_block(i, TM), 0, 0)),
            pl.BlockSpec((1, D_MODEL), const),
            pl.BlockSpec((D_MODEL, sum(widths)), const),
            pl.BlockSpec((TM, LANES), lambda i: (_rope_block(i, TM), 0)),
            pl.BlockSpec((TM, LANES), lambda i: (_rope_block(i, TM), 0)),
        ] + st_in,
        out_specs=[pl.BlockSpec((TM, w), row) for w in widths] + st_out,
        out_shape=[jax.ShapeDtypeStruct((N_TOK, w), BF16) for w in widths] + st_shape,
        compiler_params=_cparams("arbitrary"),
        name="odd_in",
    )(x, mod_l, g_pre, wp, cos_t, sin_t, *(prev or ()))


def _gqa_heads(q_ref, keys, vals, sink_ref, mask, o_ref, tq, row0=0):
    scale = S_HEAD_DIM ** -0.5
    lane = lax.broadcasted_iota(jnp.int32, (tq, LANES), 1)
    low = lane < S_HEAD_DIM
    rs = slice(row0, row0 + tq)

    def kv_group(g):
        k2, v2 = keys(g), vals(g)
        col = g * S_GROUP * S_HEAD_DIM
        tiles = [q_ref[rs, col:col + LANES].astype(F32) * scale, q_ref[rs, col + LANES:col + 2 * LANES].astype(F32) * scale]
        q4 = jnp.concatenate([jnp.where(low, tiles[0], 0.0), jnp.where(low, 0.0, tiles[0]),
                              jnp.where(low, tiles[1], 0.0), jnp.where(low, 0.0, tiles[1])], axis=0).astype(BF16)
        yield
        s = lax.dot_general(k2, q4, (((1,), (1,)), ((), ())), preferred_element_type=F32)
        if mask is not None:
            pieces, row = [], 0
            for start, keep in sorted(mask.items()):
                if start > row:
                    pieces.append(s[row:start])
                pieces.append(jnp.where(keep, s[start:start + keep.shape[0]], NEG))
                row = start + keep.shape[0]
            if row < s.shape[0]:
                pieces.append(s[row:])
            s = jnp.concatenate(pieces, axis=0)
        yield
        sk = jnp.concatenate([jnp.full((1, tq), sink_ref[g * S_GROUP + j], F32) for j in range(S_GROUP)], axis=1)
        m = jnp.maximum(jnp.max(s, axis=0, keepdims=True), sk)
        e = jnp.exp(s - m)
        yield
        l = jnp.sum(e, axis=0, keepdims=True) + jnp.exp(sk - m)
        o_t = lax.dot_general(v2, e.astype(BF16), (((0,), (0,)), ((), ())), preferred_element_type=F32)
        yield
        o4 = (o_t * (1.0 / l)).T
        o_ref[rs, col:col + LANES] = jnp.where(low, o4[0:tq], o4[tq:2 * tq]).astype(BF16)
        o_ref[rs, col + LANES:col + 2 * LANES] = jnp.where(low, o4[2 * tq:3 * tq], o4[3 * tq:4 * tq]).astype(BF16)

    return [kv_group(g) for g in range(S_KV_HEADS)]


def _gqa_ctx_kernel(sink_ref, q_ref, k_ref, v_ref, o_ref):
    groups = []
    for bb in range(S_SEQ_PER_STEP):
        rows = slice(bb * SEQ, (bb + 1) * SEQ)
        keys = lambda g, rows=rows: k_ref[rows, g * LANES:(g + 1) * LANES]
        vals = lambda g, rows=rows: v_ref[rows, g * LANES:(g + 1) * LANES]
        groups += _gqa_heads(q_ref, keys, vals, sink_ref, None, o_ref, SEQ, row0=bb * SEQ)
    _run_staged(groups)


def _gqa_ctx(q, kd, vd, sink):
    row = lambda b: (b, 0)
    return pl.pallas_call(
        _gqa_ctx_kernel,
        grid=(BATCH // S_SEQ_PER_STEP,),
        in_specs=[
            pl.BlockSpec(memory_space=pltpu.SMEM),
            pl.BlockSpec((S_SEQ_PER_STEP * SEQ, S_HEADS * S_HEAD_DIM), row),
            pl.BlockSpec((S_SEQ_PER_STEP * SEQ, S_KV_HEADS * LANES), row),
            pl.BlockSpec((S_SEQ_PER_STEP * SEQ, S_KV_HEADS * LANES), row),
        ],
        out_specs=pl.BlockSpec((S_SEQ_PER_STEP * SEQ, S_HEADS * S_HEAD_DIM), row),
        out_shape=jax.ShapeDtypeStruct((N_CTX, S_HEADS * S_HEAD_DIM), BF16),
        compiler_params=_cparams("arbitrary"),
        name="gqa_ctx",
    )(sink, q, kd, vd)


def _gqa_lat_kernel(sink_ref, q_ref, k_ref, v_ref, ck_ref, cv_ref, o_ref):
    nb = DEC_SEQ // S_BLOCK
    B = S_BLOCK
    t_idx = lax.broadcasted_iota(jnp.int32, (B, S_GROUP * B), 1) % B
    s_idx = lax.broadcasted_iota(jnp.int32, (B, S_GROUP * B), 0)
    far = jnp.int32(4 * B)
    groups = []
    for qb in range(S_LAT_BLOCKS_PER_STEP):
        n = pl.program_id(1) * S_LAT_BLOCKS_PER_STEP + qb
        prev0 = pl.multiple_of(jnp.maximum(n - 1, 0) * B, B)
        cur0 = pl.multiple_of(n * B, B)
        next0 = pl.multiple_of(jnp.minimum(n + 1, nb - 1) * B, B)
        keep_prev = s_idx >= t_idx + jnp.where(n == 0, far, 0)
        keep_next = s_idx + jnp.where(n == nb - 1, far, 0) <= t_idx
        mask = {PAST_LEN: keep_prev, PAST_LEN + 2 * B: keep_next}

        def gather(ref, cache_ref, g, starts=(prev0, cur0, next0)):
            sl = slice(g * LANES, (g + 1) * LANES)
            return jnp.concatenate([cache_ref[0, g]] + [ref[pl.ds(r0, B), sl] for r0 in starts], axis=0)

        keys = lambda g, gather=gather: gather(k_ref, ck_ref, g)
        vals = lambda g, gather=gather: gather(v_ref, cv_ref, g)
        groups += _gqa_heads(q_ref, keys, vals, sink_ref, mask, o_ref, B, row0=qb * B)
    _run_staged(groups)


def _gqa_lat(q, kd, vd, cache_k2, cache_v2, sink):
    rows = S_LAT_BLOCKS_PER_STEP * S_BLOCK
    nb = DEC_SEQ // rows
    qb0 = N_CTX // rows
    kb0 = N_CTX // DEC_SEQ
    return pl.pallas_call(
        _gqa_lat_kernel,
        grid=(DEC_BATCH, nb),
        in_specs=[
            pl.BlockSpec(memory_space=pltpu.SMEM),
            pl.BlockSpec((rows, S_HEADS * S_HEAD_DIM), lambda b, n: (qb0 + b * nb + n, 0)),
            pl.BlockSpec((DEC_SEQ, S_KV_HEADS * LANES), lambda b, n: (kb0 + b, 0)),
            pl.BlockSpec((DEC_SEQ, S_KV_HEADS * LANES), lambda b, n: (kb0 + b, 0)),
            pl.BlockSpec((1, S_KV_HEADS, PAST_LEN, LANES), lambda b, n: (b, 0, 0, 0)),
            pl.BlockSpec((1, S_KV_HEADS, PAST_LEN, LANES), lambda b, n: (b, 0, 0, 0)),
        ],
        out_specs=pl.BlockSpec((rows, S_HEADS * S_HEAD_DIM), lambda b, n: (b * nb + n, 0)),
        out_shape=jax.ShapeDtypeStruct((N_LAT, S_HEADS * S_HEAD_DIM), BF16),
        compiler_params=_cparams("arbitrary", "arbitrary"),
        name="gqa_lat",
    )(sink, q, kd, vd, cache_k2, cache_v2)


def _rope_tables(rot_dim, lane_off, reps):
    nf = rot_dim // 4
    inv = np.float32(ROPE_BASE) ** (-np.arange(nf, dtype=np.float32) / np.float32(nf))
    pos = np.arange(DEC_SEQ)
    ang_r = (pos // GRID_W).astype(np.float32)[:, None] * inv
    ang_c = (pos % GRID_W).astype(np.float32)[:, None] * inv
    cos_g = np.concatenate([np.cos(ang_r), np.cos(ang_r), np.cos(ang_c), np.cos(ang_c)], axis=1)
    sin_g = np.concatenate([-np.sin(ang_r), np.sin(ang_r), -np.sin(ang_c), np.sin(ang_c)], axis=1)
    cos_t = np.ones((TM + DEC_SEQ, LANES), np.float32)
    sin_t = np.zeros((TM + DEC_SEQ, LANES), np.float32)
    for r in range(reps):
        lo = lane_off + r * rot_dim
        cos_t[TM:, lo:lo + rot_dim] = cos_g
        sin_t[TM:, lo:lo + rot_dim] = sin_g
    return jnp.asarray(cos_t, F32), jnp.asarray(sin_t, F32)


def _even_weights(w_in, w_qb, w_kvb, gate_b):
    z = lambda n: jnp.zeros((D_MODEL, n), F32)
    idx = np.cumsum([MLA_Q_LORA, MLA_KV_LORA, MLA_ROPE, M_HEADS * M_DK, M_HEADS * M_DK, M_HEADS * M_DV,
                     M_HEADS * M_DV])
    q_a, kv_a, k_rope, mq, mk, mv, mo, mg = jnp.split(w_in, idx, axis=1)
    wp = jnp.concatenate([q_a, kv_a, z(KR_LANE), k_rope, z(LANES - KR_LANE - MLA_ROPE), mq, mk, mv, mo, mg,
                          z(LANES - 4 * M_HEADS)], axis=1).astype(BF16)
    wq = jnp.pad(w_qb.reshape(MLA_Q_LORA, MLA_HEADS, MLA_QK), ((0, 0), (0, 0), (0, LANES - MLA_QK)))
    wq = wq.reshape(MLA_Q_LORA, MLA_HEADS * LANES).astype(BF16)
    kvb = w_kvb.reshape(MLA_KV_LORA, MLA_HEADS, MLA_NOPE + MLA_V)
    wk = jnp.pad(kvb[:, :, :MLA_NOPE], ((0, 0), (0, 0), (0, LANES - MLA_NOPE))).transpose(1, 0, 2).astype(BF16)
    wv = kvb[:, :, MLA_NOPE:].reshape(MLA_KV_LORA, MLA_HEADS // 2, 2 * MLA_V).transpose(1, 0, 2).astype(BF16)
    gb = jnp.pad(gate_b, (0, LANES - 4 * M_HEADS)).reshape(1, LANES)
    return wp, wq, wk, wv, gb


def _dup_heads(w):
    w3 = w.reshape(D_MODEL, S_KV_HEADS, S_HEAD_DIM)
    return jnp.concatenate([w3, w3], axis=-1).reshape(D_MODEL, S_KV_HEADS * LANES)


def _odd_weights(w_in):
    n_q = S_HEADS * S_HEAD_DIM
    n_kv = S_KV_HEADS * S_HEAD_DIM
    return jnp.concatenate([w_in[:, :n_q], _dup_heads(w_in[:, n_q:n_q + n_kv]), _dup_heads(w_in[:, n_q + n_kv:])],
                           axis=1).astype(BF16)


def kernel(x_prompt, x_sample, cache_mla_ckv, cache_mla_krope, state_mlstm_C, state_mlstm_n, state_mlstm_m,
           cache_swa_k, cache_swa_v, c, c_ctx, ada_w, ada_b, norm_g, ffn_w_gate, ffn_w_up, ffn_w_down,
           even_w_in, mla_g_qa, mla_g_kva, mla_w_qb, mla_w_kvb, mlstm_gate_b, mlstm_g_out, even_w_out,
           odd_w_in, swa_sink, odd_w_out):
    xs = (x_prompt.reshape(N_CTX, D_MODEL), x_sample.reshape(N_LAT, D_MODEL))
    cond = jnp.concatenate([c_ctx[None], c, jnp.zeros((COND_ROWS - N_GROUPS, D_MODEL), F32)], axis=0)
    mod = _adaln(cond, ada_w, ada_b)
    cos_e, sin_e = _rope_tables(MLA_ROPE, KR_LANE, 1)
    cos_o, sin_o = _rope_tables(S_HEAD_DIM, 0, LANES // S_HEAD_DIM)
    w32 = (ffn_w_gate, ffn_w_up, ffn_w_down)
    w16 = tuple(w[0, 0].astype(BF16) for w in w32)

    assert N_EVEN == 2 and N_ODD == 2
    even_state = mlstm_state = odd_state = None
    for l in range(DEPTH):
        mod_l = mod[l]
        g = norm_g[l]
        i = l // 2
        (x,), w16 = _ffn(xs, mod_l, g[0:2], w16, 0, next_w=(w32, l, 1))
        if l % 2 == 0:
            wp, wq, wk, wv, gb = _even_weights(even_w_in[i], mla_w_qb[i], mla_w_kvb[i], mlstm_gate_b[i])
            q, ckv, kr, mq, mk, mv, mo, mg, *even_state = _even_in(
                x, mod_l, g[2:3], wp, mla_g_qa[i].reshape(1, -1), mla_g_kva[i].reshape(1, -1), wq, cos_e, sin_e,
                prev=even_state)
            cache_kr = jnp.pad(cache_mla_krope[:, i], ((0, 0), (0, 0), (KR_LANE, LANES - KR_LANE - MLA_ROPE)))
            att_c = _mla(q, ckv, kr, wk, wv, row0=0, n_batch=BATCH, t_own=SEQ, tq=SEQ)
            att_l = _mla(q, ckv, kr, wk, wv, row0=N_CTX, n_batch=DEC_BATCH, t_own=DEC_SEQ, tq=512,
                         cache=(cache_mla_ckv[:, i], cache_kr))
            g_out = mlstm_g_out[i].reshape(1, -1)
            mo_c, *mlstm_state = _mlstm(mq, mk, mv, mo, mg, gb, g_out, row0=0, n_batch=BATCH, t_len=SEQ,
                                        nb=M_CTX_SEQ_PER_STEP, prev=mlstm_state)
            n_st = 2 * M_HEADS
            state = (state_mlstm_C[:, i].reshape(DEC_BATCH, n_st, M_DK, M_DV),
                     state_mlstm_n[:, i].reshape(DEC_BATCH, n_st, M_DK),
                     jnp.broadcast_to(state_mlstm_m[:, i].reshape(DEC_BATCH, n_st, 1), (DEC_BATCH, n_st, LANES)))
            mo_l, _, _, _ = _mlstm(mq, mk, mv, mo, mg, gb, g_out, row0=N_CTX, n_batch=DEC_BATCH, t_len=DEC_SEQ,
                                   nb=M_SEQ_PER_STEP, state=state)
            w_out = even_w_out[i].astype(BF16)
            n_att = MLA_HEADS * MLA_V
            acts, act_w = [(att_c, att_l), (mo_c, mo_l)], [w_out[:n_att], w_out[n_att:]]
        else:
            q, kd, vd, *odd_state = _odd_in(x, mod_l, g[2:3], _odd_weights(odd_w_in[i]), cos_o, sin_o,
                                            prev=odd_state)
            dup = lambda a: jnp.concatenate([a, a], axis=-1).astype(BF16)
            o_c = _gqa_ctx(q, kd, vd, swa_sink[i])
            o_l = _gqa_lat(q, kd, vd, dup(cache_swa_k[:, i]), dup(cache_swa_v[:, i]), swa_sink[i])
            acts, act_w = [(o_c, o_l)], [odd_w_out[i].astype(BF16)]
        last = l == DEPTH - 1
        xs, w16 = _ffn((x,), mod_l, g[3:6], w16, 1, acts=acts, act_w=act_w, dual_out=last,
                       next_w=None if last else (w32, l + 1, 0))

    return (xs[0].reshape(BATCH, SEQ, D_MODEL), xs[1].reshape(DEC_BATCH, DEC_SEQ, D_MODEL),
            even_state[0], even_state[1],
            mlstm_state[0].reshape(BATCH, N_EVEN, 2, M_HEADS, M_DK, M_DV),
            mlstm_state[1].reshape(BATCH, N_EVEN, 2, M_HEADS, M_DK),
            mlstm_state[2][..., 0].reshape(BATCH, N_EVEN, 2, M_HEADS),
            odd_state[0], odd_state[1])
```

```python
import functools

import jax
import jax.numpy as jnp
import numpy as np
from jax import lax
from jax.experimental import pallas as pl
from jax.experimental.pallas import tpu as pltpu

F32 = jnp.float32
BF16 = jnp.bfloat16

D_MODEL = 1024
BATCH = 32
SEQ = 256
DEPTH = 4
N_EVEN = (DEPTH + 1) // 2
N_ODD = DEPTH // 2
DEC_BATCH = 4
DEC_SEQ = 1024
PAST_LEN = 512
GRID_W = 64
N_MOD = 9
D_FF = 2816
EPS = 1e-6
ROPE_BASE = 10000.0
NEG = -1e30
MLA_HEADS = 8
MLA_NOPE = 64
MLA_ROPE = 32
MLA_QK = MLA_NOPE + MLA_ROPE
MLA_V = 64
MLA_Q_LORA = 256
MLA_KV_LORA = 128
M_HEADS = 4
M_DK = 64
M_DV = 128
M_CHUNK = 128
S_HEADS = 16
S_KV_HEADS = 4
S_GROUP = S_HEADS // S_KV_HEADS
S_HEAD_DIM = 64
S_WINDOW = 128
S_BLOCK = 128
assert S_WINDOW == S_BLOCK

N_CTX = BATCH * SEQ
N_LAT = DEC_BATCH * DEC_SEQ
N_TOK = N_CTX + N_LAT
N_GROUPS = 1 + DEC_BATCH
COND_ROWS = 8

LANES = 128
FF_CHUNK = 256
N_FF_CHUNKS = D_FF // FF_CHUNK
assert N_FF_CHUNKS * FF_CHUNK == D_FF

TM = 512
ADA_TN = 1536
M_SEQ_PER_STEP = 2
M_CTX_SEQ_PER_STEP = 4
S_SEQ_PER_STEP = 4
S_LAT_BLOCKS_PER_STEP = 4
FFN_PARTS = 2
IN_PARTS = 2
VMEM_LIMIT = 52 * 1024 * 1024

E_QA = 0
E_KVA = E_QA + MLA_Q_LORA
E_KR = E_KVA + MLA_KV_LORA
E_MQ = E_KR + LANES
E_MK = E_MQ + M_HEADS * M_DK
E_MV = E_MK + M_HEADS * M_DK
E_MO = E_MV + M_HEADS * M_DV
E_MG = E_MO + M_HEADS * M_DV
E_COLS = E_MG + LANES
KR_LANE = MLA_NOPE


def _cparams(*sem):
    return pltpu.CompilerParams(dimension_semantics=sem, vmem_limit_bytes=VMEM_LIMIT)


def _group_of_block(i, tm):
    n_ctx = N_CTX // tm
    per_lat = DEC_SEQ // tm
    return jnp.where(i < n_ctx, 0, 1 + (i - n_ctx) // per_lat)


def _rope_block(i, tm):
    n_ctx = N_CTX // tm
    per_lat = DEC_SEQ // tm
    return jnp.where(i < n_ctx, 0, 1 + (i - n_ctx) % per_lat)


def _rms(x, g):
    return x * lax.rsqrt(jnp.mean(x * x, axis=-1, keepdims=True) + EPS) * g


def _modulated(x, g_pre, mod_ref, sub):
    shift = mod_ref[0, 3 * sub:3 * sub + 1, :]
    scale = mod_ref[0, 3 * sub + 1:3 * sub + 2, :]
    return _rms(x, g_pre) * (1.0 + scale) + shift


def _silu(x):
    return x * jax.nn.sigmoid(x)


def _swap_halves(x, half):
    lane = lax.broadcasted_iota(jnp.int32, x.shape, 1)
    up = pltpu.roll(x, LANES - half, 1)
    down = pltpu.roll(x, half, 1)
    return jnp.where(lane % (2 * half) < half, up, down)


def _rope_tile(x, cos, sin, half):
    return x * cos + _swap_halves(x, half) * sin


def _run_staged(stages, skew=0):
    pending, live, tick, started = list(stages), [], 0, 0
    while pending or live:
        while pending and tick >= started * skew:
            live.append(pending.pop(0))
            started += 1
        live = [s for s in live if next(s, True) is None]
        tick += 1


def _adaln_kernel(cond_ref, w_ref, b_ref, o_ref):
    s = _silu(cond_ref[...]).astype(BF16)
    o_ref[0] = jnp.dot(s, w_ref[0].astype(BF16), preferred_element_type=F32) + b_ref[0]


def _adaln(cond, ada_w, ada_b):
    n_out = N_MOD * D_MODEL
    out = pl.pallas_call(
        _adaln_kernel,
        grid=(DEPTH, n_out // ADA_TN),
        in_specs=[
            pl.BlockSpec((COND_ROWS, D_MODEL), lambda l, j: (0, 0)),
            pl.BlockSpec((1, D_MODEL, ADA_TN), lambda l, j: (l, 0, j)),
            pl.BlockSpec((1, 1, ADA_TN), lambda l, j: (l, 0, j)),
        ],
        out_specs=pl.BlockSpec((1, COND_ROWS, ADA_TN), lambda l, j: (l, 0, j)),
        out_shape=jax.ShapeDtypeStruct((DEPTH, COND_ROWS, n_out), F32),
        compiler_params=_cparams("arbitrary", "arbitrary"),
        name="adaln",
    )(cond, ada_w, ada_b.reshape(DEPTH, 1, n_out))
    return out.reshape(DEPTH, COND_ROWS, N_MOD, D_MODEL)


def _ctx_block(i):
    return jnp.minimum(i, N_CTX // TM - 1)


def _lat_block(i):
    return jnp.maximum(i - N_CTX // TM, 0)


def _dual_specs(width):
    return [pl.BlockSpec((TM, width), lambda i: (_ctx_block(i), 0)),
            pl.BlockSpec((TM, width), lambda i: (_lat_block(i), 0))]


def _ffn_kernel(sub, dual_in, dual_out, n_act, convert, *refs):
    it = iter(refs)
    take = lambda n: [next(it) for _ in range(n)]
    a_refs = take(2 * n_act)
    w_refs = take(n_act)
    x_refs = take(2 if dual_in else 1)
    mod_ref, g_ref, wg_ref, wu_ref, wd_ref = take(5)
    nxt_refs = take(3) if convert else []
    o_refs = take(2 if dual_out else 1)
    nxt_out_refs = take(3) if convert else []
    h_ref, xs_ref = take(2)
    step = pl.program_id(0)
    is_ctx = step < N_CTX // TM
    g0 = 1 if n_act else 0

    gate_row = mod_ref[0, 3 * sub + 2:3 * sub + 3, :]
    results = {}

    def rows_of(part):
        rs = slice(part * (TM // FFN_PARTS), (part + 1) * (TM // FFN_PARTS))
        x = jnp.where(is_ctx, x_refs[0][rs, :], x_refs[1][rs, :]) if dual_in else x_refs[0][rs, :]
        if n_act:
            y = None
            for k in range(n_act):
                a = jnp.where(is_ctx, a_refs[2 * k][rs, :], a_refs[2 * k + 1][rs, :])
                d = jnp.dot(a, w_refs[k][...], preferred_element_type=F32)
                y = d if y is None else y + d
            x = x + mod_ref[0, 5:6, :] * _rms(y, g_ref[0:1, :])
        xs_ref[rs, :] = x
        h = _modulated(x, g_ref[g0:g0 + 1, :], mod_ref, sub).astype(BF16)
        yield
        for c in range(N_FF_CHUNKS):
            sl = slice(c * FF_CHUNK, (c + 1) * FF_CHUNK)
            gate = jnp.dot(h, wg_ref[:, sl], preferred_element_type=F32)
            up = jnp.dot(h, wu_ref[:, sl], preferred_element_type=F32)
            h_ref[rs, sl] = (_silu(gate) * up).astype(BF16)
            yield
        y = jnp.dot(h_ref[rs, :], wd_ref[...], preferred_element_type=F32)
        yield
        results[part] = xs_ref[rs, :] + 0.5 * gate_row * _rms(y, g_ref[g0 + 1:g0 + 2, :])

    _run_staged(rows_of(part) for part in range(FFN_PARTS))
    result = jnp.concatenate([results[part] for part in range(FFN_PARTS)], axis=0)

    if dual_out:
        @pl.when(is_ctx)
        def _ctx():
            o_refs[0][...] = result

        @pl.when(jnp.logical_not(is_ctx))
        def _lat():
            o_refs[1][...] = result
    else:
        o_refs[0][...] = result

    if convert:
        @pl.when(step < N_FF_CHUNKS)
        def _next_weights():
            for src_ref, dst_ref in zip(nxt_refs, nxt_out_refs):
                dst_ref[...] = src_ref[...].astype(BF16)


def _ffn(xs, mod_l, g_rows, w16, half, *, acts=(), act_w=(), dual_out=False, next_w=None):
    dual_in = len(xs) == 2
    n_act = len(acts)
    convert = next_w is not None
    sub = 2 * half
    const = lambda i: (0, 0)
    once = dict(pipeline_mode=pl.Buffered(1))
    in_specs, args = [], []
    for a_c, a_l in acts:
        in_specs += _dual_specs(a_c.shape[1])
        args += [a_c, a_l]
    in_specs += [pl.BlockSpec(w.shape, const, **once) for w in act_w]
    args += list(act_w)
    in_specs += _dual_specs(D_MODEL) if dual_in else [pl.BlockSpec((TM, D_MODEL), lambda i: (i, 0))]
    args += list(xs)
    in_specs += [
        pl.BlockSpec((1, N_MOD, D_MODEL), lambda i: (_group_of_block(i, TM), 0, 0)),
        pl.BlockSpec(g_rows.shape, const),
        pl.BlockSpec((D_MODEL, D_FF), const, **once),
        pl.BlockSpec((D_MODEL, D_FF), const, **once),
        pl.BlockSpec((D_FF, D_MODEL), const, **once),
    ]
    args += [mod_l, g_rows, *w16]
    if dual_out:
        out_specs = _dual_specs(D_MODEL)
        out_shape = [jax.ShapeDtypeStruct((N_CTX, D_MODEL), F32), jax.ShapeDtypeStruct((N_LAT, D_MODEL), F32)]
    else:
        out_specs = [pl.BlockSpec((TM, D_MODEL), lambda i: (i, 0))]
        out_shape = [jax.ShapeDtypeStruct((N_TOK, D_MODEL), F32)]
    if convert:
        (wg32, wu32, wd32), ln, hn = next_w
        chunk = lambda i: jnp.minimum(i, N_FF_CHUNKS - 1)
        in_specs += [
            pl.BlockSpec((None, None, D_MODEL, FF_CHUNK), lambda i: (ln, hn, 0, chunk(i))),
            pl.BlockSpec((None, None, D_MODEL, FF_CHUNK), lambda i: (ln, hn, 0, chunk(i))),
            pl.BlockSpec((None, None, FF_CHUNK, D_MODEL), lambda i: (ln, hn, chunk(i), 0)),
        ]
        args += [wg32, wu32, wd32]
        out_specs += [
            pl.BlockSpec((D_MODEL, FF_CHUNK), lambda i: (0, chunk(i))),
            pl.BlockSpec((D_MODEL, FF_CHUNK), lambda i: (0, chunk(i))),
            pl.BlockSpec((FF_CHUNK, D_MODEL), lambda i: (chunk(i), 0)),
        ]
        out_shape += [jax.ShapeDtypeStruct((D_MODEL, D_FF), BF16), jax.ShapeDtypeStruct((D_MODEL, D_FF), BF16),
                      jax.ShapeDtypeStruct((D_FF, D_MODEL), BF16)]
    outs = pl.pallas_call(
        functools.partial(_ffn_kernel, sub, dual_in, dual_out, n_act, convert),
        grid=(N_TOK // TM,),
        in_specs=in_specs,
        out_specs=out_specs,
        out_shape=out_shape,
        scratch_shapes=[pltpu.VMEM((TM, D_FF), BF16), pltpu.VMEM((TM, D_MODEL), F32)],
        compiler_params=_cparams("arbitrary"),
        name="mix_ffn" if n_act else "ffn",
    )(*args)
    n_stream = 2 if dual_out else 1
    stream = tuple(outs[:n_stream])
    return stream, (tuple(outs[n_stream:]) if convert else None)


def _even_in_kernel(has_prev, x_ref, mod_ref, g_ref, wp_ref, gqa_ref, gkva_ref, wq_ref, cos_ref, sin_ref, *refs):
    prev_refs, refs = (refs[:2], refs[2:]) if has_prev else ((), refs)
    q_ref, ckv_ref, kr_ref, mq_ref, mk_ref, mv_ref, mo_ref, mg_ref, ckv_ctx_ref, kr_ctx_ref = refs
    half = MLA_ROPE // 4
    assert TM // IN_PARTS == SEQ
    ctx_state = {}

    def rows_of(part):
        rs = slice(part * SEQ, (part + 1) * SEQ)
        h = _modulated(x_ref[rs, :], g_ref[...], mod_ref, 1).astype(BF16)
        p = jnp.dot(h, wp_ref[...], preferred_element_type=F32)
        cos = cos_ref[rs, :]
        sin = sin_ref[rs, :]
        yield
        qn = _rms(p[:, E_QA:E_QA + MLA_Q_LORA], gqa_ref[...]).astype(BF16)
        q = jnp.dot(qn, wq_ref[...], preferred_element_type=F32)
        ckv = _rms(p[:, E_KVA:E_KVA + MLA_KV_LORA], gkva_ref[...])
        ckv_ref[rs, :] = ckv.astype(BF16)
        kr = _rope_tile(p[:, E_KR:E_KR + LANES], cos, sin, half)
        kr_ref[rs, :] = kr.astype(BF16)
        ctx_state[part] = (ckv, kr[:, KR_LANE:KR_LANE + MLA_ROPE])
        mq_ref[rs, :] = p[:, E_MQ:E_MK].astype(BF16)
        mk_ref[rs, :] = p[:, E_MK:E_MV].astype(BF16)
        mv_ref[rs, :] = p[:, E_MV:E_MO].astype(BF16)
        mo_ref[rs, :] = p[:, E_MO:E_MG]
        mg_ref[rs, :] = p[:, E_MG:E_COLS]
        yield
        for hd in range(MLA_HEADS):
            sl = slice(hd * LANES, (hd + 1) * LANES)
            q_ref[rs, sl] = _rope_tile(q[:, sl], cos, sin, half).astype(BF16)

    _run_staged((rows_of(part) for part in range(IN_PARTS)), skew=1)

    @pl.when(pl.program_id(0) < N_CTX // TM)
    def _ctx_state():
        for bb in range(IN_PARTS):
            ckv, kr = ctx_state[bb]
            if has_prev:
                ckv_ctx_ref[bb, 0] = prev_refs[0][bb]
                kr_ctx_ref[bb, 0] = prev_refs[1][bb]
                ckv_ctx_ref[bb, 1] = ckv
                kr_ctx_ref[bb, 1] = kr
            else:
                ckv_ctx_ref[bb] = ckv
                kr_ctx_ref[bb] = kr


def _state_specs(shapes, prev):
    nseq = TM // SEQ
    zeros = lambda s: (0,) * len(s)
    in_specs = [pl.BlockSpec((nseq,) + s, lambda i, s=s: (_ctx_block(i),) + zeros(s)) for s in shapes] if prev else []
    lead = (2,) if prev else ()
    out_specs = [pl.BlockSpec((nseq,) + lead + s, lambda i, s=s: (_ctx_block(i),) + zeros(lead + s)) for s in shapes]
    out_shape = [jax.ShapeDtypeStruct((BATCH,) + lead + s, F32) for s in shapes]
    return in_specs, out_specs, out_shape


def _even_in(x, mod_l, g_pre, wp, g_qa, g_kva, wq, cos_t, sin_t, prev=None):
    row = lambda i: (i, 0)
    const = lambda i: (0, 0)
    st_in, st_out, st_shape = _state_specs([(SEQ, MLA_KV_LORA), (SEQ, MLA_ROPE)], prev)
    widths = (MLA_HEADS * LANES, LANES, LANES, M_HEADS * M_DK, M_HEADS * M_DK, M_HEADS * M_DV,
              M_HEADS * M_DV, LANES)
    dtypes = (BF16, BF16, BF16, BF16, BF16, BF16, F32, F32)
    return pl.pallas_call(
        functools.partial(_even_in_kernel, prev is not None),
        grid=(N_TOK // TM,),
        in_specs=[
            pl.BlockSpec((TM, D_MODEL), row),
            pl.BlockSpec((1, N_MOD, D_MODEL), lambda i: (_group_of_block(i, TM), 0, 0)),
            pl.BlockSpec((1, D_MODEL), const),
            pl.BlockSpec((D_MODEL, E_COLS), const),
            pl.BlockSpec((1, MLA_Q_LORA), const),
            pl.BlockSpec((1, MLA_KV_LORA), const),
            pl.BlockSpec((MLA_Q_LORA, MLA_HEADS * LANES), const),
            pl.BlockSpec((TM, LANES), lambda i: (_rope_block(i, TM), 0)),
            pl.BlockSpec((TM, LANES), lambda i: (_rope_block(i, TM), 0)),
        ] + st_in,
        out_specs=[pl.BlockSpec((TM, w), row) for w in widths] + st_out,
        out_shape=[jax.ShapeDtypeStruct((N_TOK, w), dt) for w, dt in zip(widths, dtypes)] + st_shape,
        compiler_params=_cparams("arbitrary"),
        name="even_in",
    )(x, mod_l, g_pre, wp, g_qa, g_kva, wq, cos_t, sin_t, *(prev or ()))


def _mla_kernel(n_cache, tq, *refs):
    if n_cache:
        q_ref, ckv_ref, kr_ref, cckv_ref, ckr_ref, wk_ref, wv_ref, o_ref, ks_ref, vs_ref = refs
    else:
        q_ref, ckv_ref, kr_ref, wk_ref, wv_ref, o_ref, ks_ref, vs_ref = refs

    @pl.when(pl.program_id(1) == 0)
    def _expand():
        if n_cache:
            ckv16 = jnp.concatenate([cckv_ref[0].astype(BF16), ckv_ref[...]], axis=0)
            kr = jnp.concatenate([ckr_ref[0].astype(BF16), kr_ref[...]], axis=0)
        else:
            ckv16, kr = ckv_ref[...], kr_ref[...]
        for hd in range(MLA_HEADS):
            ks_ref[hd] = (jnp.dot(ckv16, wk_ref[hd], preferred_element_type=F32) + kr).astype(BF16)
        for pr in range(MLA_HEADS // 2):
            vs_ref[pr] = jnp.dot(ckv16, wv_ref[pr], preferred_element_type=F32).astype(BF16)

    scale = MLA_QK ** -0.5
    lane = lax.broadcasted_iota(jnp.int32, (tq, LANES), 1)
    for pr in range(MLA_HEADS // 2):
        outs = []
        for j in range(2):
            hd = 2 * pr + j
            qh = q_ref[:, hd * LANES:(hd + 1) * LANES]
            s = lax.dot_general(qh, ks_ref[hd], (((1,), (1,)), ((), ())), preferred_element_type=F32) * scale
            m = jnp.max(s, axis=-1, keepdims=True)
            e = jnp.exp(s - m)
            l = jnp.sum(e, axis=-1, keepdims=True)
            outs.append(jnp.dot(e.astype(BF16), vs_ref[pr], preferred_element_type=F32) / l)
        o_ref[:, pr * LANES:(pr + 1) * LANES] = jnp.where(lane < MLA_V, outs[0], outs[1]).astype(BF16)


def _mla(q, ckv, kr, wk, wv, *, row0, n_batch, t_own, tq, cache=None):
    n_cache = 0 if cache is None else cache[0].shape[1]
    nq = t_own // tq
    qb0 = row0 // tq
    kb0 = row0 // t_own
    in_specs = [
        pl.BlockSpec((tq, MLA_HEADS * LANES), lambda b, j: (qb0 + b * nq + j, 0)),
        pl.BlockSpec((t_own, LANES), lambda b, j: (kb0 + b, 0)),
        pl.BlockSpec((t_own, LANES), lambda b, j: (kb0 + b, 0)),
    ]
    args = [q, ckv, kr]
    if n_cache:
        in_specs += [pl.BlockSpec((1, n_cache, LANES), lambda b, j: (b, 0, 0))] * 2
        args += list(cache)
    in_specs += [
        pl.BlockSpec((MLA_HEADS, LANES, LANES), lambda b, j: (0, 0, 0)),
        pl.BlockSpec((MLA_HEADS // 2, LANES, LANES), lambda b, j: (0, 0, 0)),
    ]
    args += [wk, wv]
    t_keys = n_cache + t_own
    return pl.pallas_call(
        functools.partial(_mla_kernel, n_cache, tq),
        grid=(n_batch, nq),
        in_specs=in_specs,
        out_specs=pl.BlockSpec((tq, MLA_HEADS * MLA_V), lambda b, j: (b * nq + j, 0)),
        out_shape=jax.ShapeDtypeStruct((n_batch * t_own, MLA_HEADS * MLA_V), BF16),
        scratch_shapes=[pltpu.VMEM((MLA_HEADS, t_keys, LANES), BF16),
                        pltpu.VMEM((MLA_HEADS // 2, t_keys, LANES), BF16)],
        compiler_params=_cparams("arbitrary", "arbitrary"),
        name="mla_lat" if n_cache else "mla_ctx",
    )(*args)


def _log_sigmoid(x):
    return jnp.minimum(x, 0.0) - jnp.log1p(jnp.exp(-jnp.abs(x)))


def _mlstm_chunk(rev, r0, st, hd, allowed, cum, e_rows, e_cols, mq_ref, mk_ref, mv_ref, ct_ref, ns_ref, ms_ref,
                 h_ref, h_slot):
    L = M_CHUNK
    nt = (((1,), (1,)), ((), ()))
    gi = (2 * M_HEADS if rev else 0) + hd
    gf = gi + M_HEADS
    edge = 0 if rev else L - 1
    rows = pl.ds(r0, L)
    q16 = mq_ref[rows, hd * M_DK:(hd + 1) * M_DK]
    k16 = mk_ref[rows, hd * M_DK:(hd + 1) * M_DK] * (M_DK ** -0.5)
    v_t16 = mv_ref[rows, hd * M_DV:(hd + 1) * M_DV].T
    v_t = v_t16.astype(F32)
    ct_prev = ct_ref[st, :, 0:M_DK]
    n_prev = ns_ref[st, 0:1, 0:M_DK]
    m_prev = ms_ref[st, 0:1, 0:1]
    yield

    b_row = cum[gf:gf + 1, :]
    e_row = e_rows[gi:gi + 1, :]
    e_col = e_cols[:, gi:gi + 1]
    dmat = jnp.where(allowed, b_row + e_col, NEG)
    inter = m_prev + b_row
    mt = jnp.maximum(inter, jnp.max(dmat, axis=0, keepdims=True))
    w_inter = jnp.exp(inter - mt)
    qk = lax.dot_general(k16, q16, nt, preferred_element_type=F32)
    a = qk * jnp.exp(dmat - mt)
    yield
    n8 = jnp.broadcast_to(n_prev, (8, M_DK)).astype(BF16)
    nq = lax.dot_general(n8, q16, nt, preferred_element_type=F32)[0:1]
    num = w_inter * lax.dot_general(ct_prev.astype(BF16), q16, nt, preferred_element_type=F32) \
        + jnp.dot(v_t16, a.astype(BF16), preferred_element_type=F32)
    den = w_inter * nq + jnp.sum(a, axis=0, keepdims=True)
    h_t = num * (1.0 / jnp.maximum(jnp.abs(den), jnp.exp(-mt)))
    h_ref[h_slot] = h_t
    yield

    m_new = mt[:, edge:edge + 1]
    b_last = b_row[:, edge:edge + 1]
    w_s = jnp.exp(b_last + e_row - m_new)
    w_c = jnp.exp(m_prev + b_last - m_new)
    ct_ref[st, :, 0:M_DK] = w_c * ct_prev + jnp.dot((v_t * w_s).astype(BF16), k16, preferred_element_type=F32)
    w8 = jnp.broadcast_to(w_s, (8, L)).astype(BF16)
    ns_ref[st, 0:1, 0:M_DK] = w_c * n_prev + jnp.dot(w8, k16, preferred_element_type=F32)[0:1]
    ms_ref[st, 0:1, :] = jnp.broadcast_to(m_new, (1, LANES))


def _mlstm_kernel(t_len, has_state, has_prev, nb, *refs):
    n_st = 2 * M_HEADS
    prev_refs = ()
    if has_prev:
        prev_refs, refs = refs[7:10], refs[:7] + refs[10:]
    if has_state:
        (mq_ref, mk_ref, mv_ref, mo_ref, mg_ref, gb_ref, gout_ref, c0_ref, n0_ref, m0_ref,
         out_ref, c_ref, n_ref, m_ref, hf_ref, hb_ref, ct_ref, ns_ref, ms_ref) = refs
        pad = jnp.zeros((LANES - M_DK, M_DV), F32)
        for bb in range(nb):
            for st in range(n_st):
                ct_ref[bb * n_st + st] = jnp.concatenate([c0_ref[bb, st], pad], axis=0).T
                ns_ref[bb * n_st + st, 0:1, 0:M_DK] = n0_ref[bb, st:st + 1, :]
                ms_ref[bb * n_st + st, 0:1, :] = m0_ref[bb, st:st + 1, :]
    else:
        (mq_ref, mk_ref, mv_ref, mo_ref, mg_ref, gb_ref, gout_ref,
         out_ref, c_ref, n_ref, m_ref, hf_ref, hb_ref, ct_ref, ns_ref, ms_ref) = refs
        ct_ref[...] = jnp.zeros_like(ct_ref)
        ns_ref[...] = jnp.zeros_like(ns_ref)
        ms_ref[...] = jnp.zeros_like(ms_ref)
    L = M_CHUNK
    nc = t_len // L
    r_idx = lax.broadcasted_iota(jnp.int32, (L, L), 0)
    c_idx = lax.broadcasted_iota(jnp.int32, (L, L), 1)
    lower = c_idx <= r_idx
    upper = c_idx >= r_idx
    lane = lax.broadcasted_iota(jnp.int32, (L, LANES), 1)
    is_forget = (lane % (2 * M_HEADS)) >= M_HEADS
    n_gate = 4 * M_HEADS
    zeros_below = jnp.zeros((L - n_gate, L), F32)

    def gate_sums(r0, rev):
        g = mg_ref[pl.ds(r0, L), :] + gb_ref[...]
        g = jnp.where(is_forget, _log_sigmoid(g), g)
        rows = g.T[0:n_gate, :]
        tri = (lower if rev else upper).astype(BF16)
        hi = rows.astype(BF16)
        rest = rows - hi.astype(F32)
        mid = rest.astype(BF16)
        lo = (rest - mid.astype(F32)).astype(BF16)
        cum = (jnp.dot(hi, tri, preferred_element_type=F32) + jnp.dot(mid, tri, preferred_element_type=F32)
               + jnp.dot(lo, tri, preferred_element_type=F32))
        e_rows = rows - pltpu.roll(cum, n_gate - M_HEADS, 0)
        e_cols = jnp.concatenate([e_rows, zeros_below], axis=0).T
        return cum, e_rows, e_cols

    def step(c, carry):
        chunks = []
        for bb in range(nb):
            for rev in (False, True):
                chunk = nc - 1 - c if rev else c
                r0 = pl.multiple_of(bb * t_len + chunk * L, L)
                cum, e_rows, e_cols = gate_sums(r0, rev)
                allowed = lower if rev else upper
                for hd in range(M_HEADS):
                    st = bb * n_st + (M_HEADS if rev else 0) + hd
                    chunks.append(_mlstm_chunk(rev, r0, st, hd, allowed, cum, e_rows, e_cols, mq_ref, mk_ref, mv_ref,
                                               ct_ref, ns_ref, ms_ref, hb_ref if rev else hf_ref,
                                               (bb * nc + chunk) * M_HEADS + hd))
        _run_staged(chunks)
        return carry

    lax.fori_loop(0, nc, step, 0)
    for bb in range(nb):
        if has_prev:
            for o_ref, p_ref in zip((c_ref, n_ref, m_ref), prev_refs):
                o_ref[bb, 0] = p_ref[bb]
        own = (bb, 1) if has_prev else (bb,)
        for st in range(n_st):
            c_ref[own + (st,)] = ct_ref[bb * n_st + st].T[0:M_DK, :]
            n_ref[own + (slice(st, st + 1), slice(None))] = ns_ref[bb * n_st + st, 0:1, 0:M_DK]
            m_ref[own + (slice(st, st + 1), slice(None))] = ms_ref[bb * n_st + st, 0:1, :]
    for tile in range(nb * nc):
        rows = slice(tile * L, (tile + 1) * L)
        for hd in range(M_HEADS):
            sl = slice(hd * M_DV, (hd + 1) * M_DV)
            hm_t = hf_ref[tile * M_HEADS + hd] + hb_ref[tile * M_HEADS + hd]
            hm_t = hm_t * lax.rsqrt(jnp.mean(hm_t * hm_t, axis=0, keepdims=True) + EPS)
            out_ref[rows, sl] = (jax.nn.sigmoid(mo_ref[rows, sl]) * (hm_t.T * gout_ref[:, sl])).astype(BF16)


def _mlstm(mq, mk, mv, mo, mg, gate_b, g_out, *, row0, n_batch, t_len, nb, state=None, prev=None):
    rows = nb * t_len
    kb0 = row0 // rows
    row = lambda b: (kb0 + b, 0)
    const = lambda b: (0, 0)
    n_st = 2 * M_HEADS
    in_specs = [
        pl.BlockSpec((rows, M_HEADS * M_DK), row),
        pl.BlockSpec((rows, M_HEADS * M_DK), row),
        pl.BlockSpec((rows, M_HEADS * M_DV), row),
        pl.BlockSpec((rows, M_HEADS * M_DV), row),
        pl.BlockSpec((rows, LANES), row),
        pl.BlockSpec((1, LANES), const),
        pl.BlockSpec((1, M_HEADS * M_DV), const),
    ]
    args = [mq, mk, mv, mo, mg, gate_b, g_out]
    state_shapes = [(n_st, M_DK, M_DV), (n_st, M_DK), (n_st, LANES)]
    zeros = lambda s: (0,) * len(s)
    state_specs = [pl.BlockSpec((nb,) + s, lambda b, s=s: (b,) + zeros(s)) for s in state_shapes]
    lead = (2,) if prev is not None else ()
    out_state_specs = [pl.BlockSpec((nb,) + lead + s, lambda b, s=s: (b,) + zeros(lead + s)) for s in state_shapes]
    if prev is not None:
        in_specs += state_specs
        args += list(prev)
    if state is not None:
        in_specs += state_specs
        args += list(state)
    return pl.pallas_call(
        functools.partial(_mlstm_kernel, t_len, state is not None, prev is not None, nb),
        grid=(n_batch // nb,),
        in_specs=in_specs,
        out_specs=[pl.BlockSpec((rows, M_HEADS * M_DV), lambda b: (b, 0))] + out_state_specs,
        out_shape=[jax.ShapeDtypeStruct((n_batch * t_len, M_HEADS * M_DV), BF16)]
        + [jax.ShapeDtypeStruct((n_batch,) + lead + s, F32) for s in state_shapes],
        scratch_shapes=[pltpu.VMEM((rows // M_CHUNK * M_HEADS, M_DV, M_CHUNK), F32),
                        pltpu.VMEM((rows // M_CHUNK * M_HEADS, M_DV, M_CHUNK), F32),
                        pltpu.VMEM((nb * n_st, M_DV, LANES), F32),
                        pltpu.VMEM((nb * n_st, 8, LANES), F32), pltpu.VMEM((nb * n_st, 8, LANES), F32)],
        compiler_params=_cparams("arbitrary"),
        name="mlstm_lat" if state is not None else "mlstm_ctx",
    )(*args)


def _odd_in_kernel(has_prev, x_ref, mod_ref, g_ref, wp_ref, cos_ref, sin_ref, *refs):
    prev_refs, refs = (refs[:2], refs[2:]) if has_prev else ((), refs)
    q_ref, k_ref, v_ref, k_ctx_ref, v_ctx_ref = refs
    half = S_HEAD_DIM // 4
    n_q = S_HEADS * S_HEAD_DIM
    n_k = S_KV_HEADS * LANES
    assert TM // IN_PARTS == SEQ
    ctx_state = {}

    def rows_of(part):
        rs = slice(part * SEQ, (part + 1) * SEQ)
        h = _modulated(x_ref[rs, :], g_ref[...], mod_ref, 1).astype(BF16)
        p = jnp.dot(h, wp_ref[...], preferred_element_type=F32)
        cos = cos_ref[rs, :]
        sin = sin_ref[rs, :]
        yield
        for t in range(n_q // LANES):
            sl = slice(t * LANES, (t + 1) * LANES)
            q_ref[rs, sl] = _rope_tile(p[:, sl], cos, sin, half).astype(BF16)
        k_tiles = [_rope_tile(p[:, n_q + g * LANES:n_q + (g + 1) * LANES], cos, sin, half) for g in range(S_KV_HEADS)]
        v_tiles = [p[:, n_q + n_k + g * LANES:n_q + n_k + (g + 1) * LANES] for g in range(S_KV_HEADS)]
        for g in range(S_KV_HEADS):
            k_ref[rs, g * LANES:(g + 1) * LANES] = k_tiles[g].astype(BF16)
            v_ref[rs, g * LANES:(g + 1) * LANES] = v_tiles[g].astype(BF16)
        ctx_state[part] = (k_tiles, v_tiles)

    _run_staged((rows_of(part) for part in range(IN_PARTS)), skew=1)

    @pl.when(pl.program_id(0) < N_CTX // TM)
    def _ctx_state():
        for bb in range(IN_PARTS):
            k_tiles, v_tiles = ctx_state[bb]
            if has_prev:
                k_ctx_ref[bb, 0] = prev_refs[0][bb]
                v_ctx_ref[bb, 0] = prev_refs[1][bb]
            for g in range(S_KV_HEADS):
                slot = (bb, 1, g) if has_prev else (bb, g)
                k_ctx_ref[slot] = k_tiles[g][:, 0:S_HEAD_DIM]
                v_ctx_ref[slot] = v_tiles[g][:, 0:S_HEAD_DIM]


def _odd_in(x, mod_l, g_pre, wp, cos_t, sin_t, prev=None):
    row = lambda i: (i, 0)
    const = lambda i: (0, 0)
    st_in, st_out, st_shape = _state_specs([(S_KV_HEADS, SEQ, S_HEAD_DIM)] * 2, prev)
    widths = (S_HEADS * S_HEAD_DIM, S_KV_HEADS * LANES, S_KV_HEADS * LANES)
    return pl.pallas_call(
        functools.partial(_odd_in_kernel, prev is not None),
        grid=(N_TOK // TM,),
        in_specs=[
            pl.BlockSpec((TM, D_MODEL), row),
            pl.BlockSpec((1, N_MOD, D_MODEL), lambda i: (_group_of_block(i, TM), 0, 0)),
            pl.BlockSpec((1, D_MODEL), const),
            pl.BlockSpec((D_MODEL, sum(widths)), const),
            pl.BlockSpec((TM, LANES), lambda i: (_rope_block(i, TM), 0)),
            pl.BlockSpec((TM, LANES), lambda i: (_rope_block(i, TM), 0)),
        ] + st_in,
        out_specs=[pl.BlockSpec((TM, w), row) for w in widths] + st_out,
        out_shape=[jax.ShapeDtypeStruct((N_TOK, w), BF16) for w in widths] + st_shape,
        compiler_params=_cparams("arbitrary"),
        name="odd_in",
    )(x, mod_l, g_pre, wp, cos_t, sin_t, *(prev or ()))


def _gqa_heads(q_ref, keys, vals, sink_ref, mask, o_ref, tq, row0=0):
    scale = S_HEAD_DIM ** -0.5
    lane = lax.broadcasted_iota(jnp.int32, (tq, LANES), 1)
    low = lane < S_HEAD_DIM
    rs = slice(row0, row0 + tq)

    def kv_group(g):
        k2, v2 = keys(g), vals(g)
        col = g * S_GROUP * S_HEAD_DIM
        tiles = [q_ref[rs, col:col + LANES].astype(F32) * scale, q_ref[rs, col + LANES:col + 2 * LANES].astype(F32) * scale]
        q4 = jnp.concatenate([jnp.where(low, tiles[0], 0.0), jnp.where(low, 0.0, tiles[0]),
                              jnp.where(low, tiles[1], 0.0), jnp.where(low, 0.0, tiles[1])], axis=0).astype(BF16)
        yield
        s = lax.dot_general(k2, q4, (((1,), (1,)), ((), ())), preferred_element_type=F32)
        if mask is not None:
            pieces, row = [], 0
            for start, keep in sorted(mask.items()):
                if start > row:
                    pieces.append(s[row:start])
                pieces.append(jnp.where(keep, s[start:start + keep.shape[0]], NEG))
                row = start + keep.shape[0]
            if row < s.shape[0]:
                pieces.append(s[row:])
            s = jnp.concatenate(pieces, axis=0)
        yield
        sk = jnp.concatenate([jnp.full((1, tq), sink_ref[g * S_GROUP + j], F32) for j in range(S_GROUP)], axis=1)
        m = jnp.maximum(jnp.max(s, axis=0, keepdims=True), sk)
        e = jnp.exp(s - m)
        yield
        l = jnp.sum(e, axis=0, keepdims=True) + jnp.exp(sk - m)
        o_t = lax.dot_general(v2, e.astype(BF16), (((0,), (0,)), ((), ())), preferred_element_type=F32)
        yield
        o4 = (o_t * (1.0 / l)).T
        o_ref[rs, col:col + LANES] = jnp.where(low, o4[0:tq], o4[tq:2 * tq]).astype(BF16)
        o_ref[rs, col + LANES:col + 2 * LANES] = jnp.where(low, o4[2 * tq:3 * tq], o4[3 * tq:4 * tq]).astype(BF16)

    return [kv_group(g) for g in range(S_KV_HEADS)]


def _gqa_ctx_kernel(sink_ref, q_ref, k_ref, v_ref, o_ref):
    groups = []
    for bb in range(S_SEQ_PER_STEP):
        rows = slice(bb * SEQ, (bb + 1) * SEQ)
        keys = lambda g, rows=rows: k_ref[rows, g * LANES:(g + 1) * LANES]
        vals = lambda g, rows=rows: v_ref[rows, g * LANES:(g + 1) * LANES]
        groups += _gqa_heads(q_ref, keys, vals, sink_ref, None, o_ref, SEQ, row0=bb * SEQ)
    _run_staged(groups)


def _gqa_ctx(q, kd, vd, sink):
    row = lambda b: (b, 0)
    return pl.pallas_call(
        _gqa_ctx_kernel,
        grid=(BATCH // S_SEQ_PER_STEP,),
        in_specs=[
            pl.BlockSpec(memory_space=pltpu.SMEM),
            pl.BlockSpec((S_SEQ_PER_STEP * SEQ, S_HEADS * S_HEAD_DIM), row),
            pl.BlockSpec((S_SEQ_PER_STEP * SEQ, S_KV_HEADS * LANES), row),
            pl.BlockSpec((S_SEQ_PER_STEP * SEQ, S_KV_HEADS * LANES), row),
        ],
        out_specs=pl.BlockSpec((S_SEQ_PER_STEP * SEQ, S_HEADS * S_HEAD_DIM), row),
        out_shape=jax.ShapeDtypeStruct((N_CTX, S_HEADS * S_HEAD_DIM), BF16),
        compiler_params=_cparams("arbitrary"),
        name="gqa_ctx",
    )(sink, q, kd, vd)


def _gqa_lat_kernel(sink_ref, q_ref, k_ref, v_ref, ck_ref, cv_ref, o_ref):
    nb = DEC_SEQ // S_BLOCK
    B = S_BLOCK
    t_idx = lax.broadcasted_iota(jnp.int32, (B, S_GROUP * B), 1) % B
    s_idx = lax.broadcasted_iota(jnp.int32, (B, S_GROUP * B), 0)
    far = jnp.int32(4 * B)
    groups = []
    for qb in range(S_LAT_BLOCKS_PER_STEP):
        n = pl.program_id(1) * S_LAT_BLOCKS_PER_STEP + qb
        prev0 = pl.multiple_of(jnp.maximum(n - 1, 0) * B, B)
        cur0 = pl.multiple_of(n * B, B)
        next0 = pl.multiple_of(jnp.minimum(n + 1, nb - 1) * B, B)
        keep_prev = s_idx >= t_idx + jnp.where(n == 0, far, 0)
        keep_next = s_idx + jnp.where(n == nb - 1, far, 0) <= t_idx
        mask = {PAST_LEN: keep_prev, PAST_LEN + 2 * B: keep_next}

        def gather(ref, cache_ref, g, starts=(prev0, cur0, next0)):
            sl = slice(g * LANES, (g + 1) * LANES)
            return jnp.concatenate([cache_ref[0, g]] + [ref[pl.ds(r0, B), sl] for r0 in starts], axis=0)

        keys = lambda g, gather=gather: gather(k_ref, ck_ref, g)
        vals = lambda g, gather=gather: gather(v_ref, cv_ref, g)
        groups += _gqa_heads(q_ref, keys, vals, sink_ref, mask, o_ref, B, row0=qb * B)
    _run_staged(groups)


def _gqa_lat(q, kd, vd, cache_k2, cache_v2, sink):
    rows = S_LAT_BLOCKS_PER_STEP * S_BLOCK
    nb = DEC_SEQ // rows
    qb0 = N_CTX // rows
    kb0 = N_CTX // DEC_SEQ
    return pl.pallas_call(
        _gqa_lat_kernel,
        grid=(DEC_BATCH, nb),
        in_specs=[
            pl.BlockSpec(memory_space=pltpu.SMEM),
            pl.BlockSpec((rows, S_HEADS * S_HEAD_DIM), lambda b, n: (qb0 + b * nb + n, 0)),
            pl.BlockSpec((DEC_SEQ, S_KV_HEADS * LANES), lambda b, n: (kb0 + b, 0)),
            pl.BlockSpec((DEC_SEQ, S_KV_HEADS * LANES), lambda b, n: (kb0 + b, 0)),
            pl.BlockSpec((1, S_KV_HEADS, PAST_LEN, LANES), lambda b, n: (b, 0, 0, 0)),
            pl.BlockSpec((1, S_KV_HEADS, PAST_LEN, LANES), lambda b, n: (b, 0, 0, 0)),
        ],
        out_specs=pl.BlockSpec((rows, S_HEADS * S_HEAD_DIM), lambda b, n: (b * nb + n, 0)),
        out_shape=jax.ShapeDtypeStruct((N_LAT, S_HEADS * S_HEAD_DIM), BF16),
        compiler_params=_cparams("arbitrary", "arbitrary"),
        name="gqa_lat",
    )(sink, q, kd, vd, cache_k2, cache_v2)


def _rope_tables(rot_dim, lane_off, reps):
    nf = rot_dim // 4
    inv = np.float32(ROPE_BASE) ** (-np.arange(nf, dtype=np.float32) / np.float32(nf))
    pos = np.arange(DEC_SEQ)
    ang_r = (pos // GRID_W).astype(np.float32)[:, None] * inv
    ang_c = (pos % GRID_W).astype(np.float32)[:, None] * inv
    cos_g = np.concatenate([np.cos(ang_r), np.cos(ang_r), np.cos(ang_c), np.cos(ang_c)], axis=1)
    sin_g = np.concatenate([-np.sin(ang_r), np.sin(ang_r), -np.sin(ang_c), np.sin(ang_c)], axis=1)
    cos_t = np.ones((TM + DEC_SEQ, LANES), np.float32)
    sin_t = np.zeros((TM + DEC_SEQ, LANES), np.float32)
    for r in range(reps):
        lo = lane_off + r * rot_dim
        cos_t[TM:, lo:lo + rot_dim] = cos_g
        sin_t[TM:, lo:lo + rot_dim] = sin_g
    return jnp.asarray(cos_t, F32), jnp.asarray(sin_t, F32)


def _even_weights(w_in, w_qb, w_kvb, gate_b):
    z = lambda n: jnp.zeros((D_MODEL, n), F32)
    idx = np.cumsum([MLA_Q_LORA, MLA_KV_LORA, MLA_ROPE, M_HEADS * M_DK, M_HEADS * M_DK, M_HEADS * M_DV,
                     M_HEADS * M_DV])
    q_a, kv_a, k_rope, mq, mk, mv, mo, mg = jnp.split(w_in, idx, axis=1)
    wp = jnp.concatenate([q_a, kv_a, z(KR_LANE), k_rope, z(LANES - KR_LANE - MLA_ROPE), mq, mk, mv, mo, mg,
                          z(LANES - 4 * M_HEADS)], axis=1).astype(BF16)
    wq = jnp.pad(w_qb.reshape(MLA_Q_LORA, MLA_HEADS, MLA_QK), ((0, 0), (0, 0), (0, LANES - MLA_QK)))
    wq = wq.reshape(MLA_Q_LORA, MLA_HEADS * LANES).astype(BF16)
    kvb = w_kvb.reshape(MLA_KV_LORA, MLA_HEADS, MLA_NOPE + MLA_V)
    wk = jnp.pad(kvb[:, :, :MLA_NOPE], ((0, 0), (0, 0), (0, LANES - MLA_NOPE))).transpose(1, 0, 2).astype(BF16)
    wv = kvb[:, :, MLA_NOPE:].reshape(MLA_KV_LORA, MLA_HEADS // 2, 2 * MLA_V).transpose(1, 0, 2).astype(BF16)
    gb = jnp.pad(gate_b, (0, LANES - 4 * M_HEADS)).reshape(1, LANES)
    return wp, wq, wk, wv, gb


def _dup_heads(w):
    w3 = w.reshape(D_MODEL, S_KV_HEADS, S_HEAD_DIM)
    return jnp.concatenate([w3, w3], axis=-1).reshape(D_MODEL, S_KV_HEADS * LANES)


def _odd_weights(w_in):
    n_q = S_HEADS * S_HEAD_DIM
    n_kv = S_KV_HEADS * S_HEAD_DIM
    return jnp.concatenate([w_in[:, :n_q], _dup_heads(w_in[:, n_q:n_q + n_kv]), _dup_heads(w_in[:, n_q + n_kv:])],
                           axis=1).astype(BF16)


def kernel(x_prompt, x_sample, cache_mla_ckv, cache_mla_krope, state_mlstm_C, state_mlstm_n, state_mlstm_m,
           cache_swa_k, cache_swa_v, c, c_ctx, ada_w, ada_b, norm_g, ffn_w_gate, ffn_w_up, ffn_w_down,
           even_w_in, mla_g_qa, mla_g_kva, mla_w_qb, mla_w_kvb, mlstm_gate_b, mlstm_g_out, even_w_out,
           odd_w_in, swa_sink, odd_w_out):
    xs = (x_prompt.reshape(N_CTX, D_MODEL), x_sample.reshape(N_LAT, D_MODEL))
    cond = jnp.concatenate([c_ctx[None], c, jnp.zeros((COND_ROWS - N_GROUPS, D_MODEL), F32)], axis=0)
    mod = _adaln(cond, ada_w, ada_b)
    cos_e, sin_e = _rope_tables(MLA_ROPE, KR_LANE, 1)
    cos_o, sin_o = _rope_tables(S_HEAD_DIM, 0, LANES // S_HEAD_DIM)
    w32 = (ffn_w_gate, ffn_w_up, ffn_w_down)
    w16 = tuple(w[0, 0].astype(BF16) for w in w32)

    assert N_EVEN == 2 and N_ODD == 2
    even_state = mlstm_state = odd_state = None
    for l in range(DEPTH):
        mod_l = mod[l]
        g = norm_g[l]
        i = l // 2
        (x,), w16 = _ffn(xs, mod_l, g[0:2], w16, 0, next_w=(w32, l, 1))
        if l % 2 == 0:
            wp, wq, wk, wv, gb = _even_weights(even_w_in[i], mla_w_qb[i], mla_w_kvb[i], mlstm_gate_b[i])
            q, ckv, kr, mq, mk, mv, mo, mg, *even_state = _even_in(
                x, mod_l, g[2:3], wp, mla_g_qa[i].reshape(1, -1), mla_g_kva[i].reshape(1, -1), wq, cos_e, sin_e,
                prev=even_state)
            cache_kr = jnp.pad(cache_mla_krope[:, i], ((0, 0), (0, 0), (KR_LANE, LANES - KR_LANE - MLA_ROPE)))
            att_c = _mla(q, ckv, kr, wk, wv, row0=0, n_batch=BATCH, t_own=SEQ, tq=SEQ)
            att_l = _mla(q, ckv, kr, wk, wv, row0=N_CTX, n_batch=DEC_BATCH, t_own=DEC_SEQ, tq=512,
                         cache=(cache_mla_ckv[:, i], cache_kr))
            g_out = mlstm_g_out[i].reshape(1, -1)
            mo_c, *mlstm_state = _mlstm(mq, mk, mv, mo, mg, gb, g_out, row0=0, n_batch=BATCH, t_len=SEQ,
                                        nb=M_CTX_SEQ_PER_STEP, prev=mlstm_state)
            n_st = 2 * M_HEADS
            state = (state_mlstm_C[:, i].reshape(DEC_BATCH, n_st, M_DK, M_DV),
                     state_mlstm_n[:, i].reshape(DEC_BATCH, n_st, M_DK),
                     jnp.broadcast_to(state_mlstm_m[:, i].reshape(DEC_BATCH, n_st, 1), (DEC_BATCH, n_st, LANES)))
            mo_l, _, _, _ = _mlstm(mq, mk, mv, mo, mg, gb, g_out, row0=N_CTX, n_batch=DEC_BATCH, t_len=DEC_SEQ,
                                   nb=M_SEQ_PER_STEP, state=state)
            w_out = even_w_out[i].astype(BF16)
            n_att = MLA_HEADS * MLA_V
            acts, act_w = [(att_c, att_l), (mo_c, mo_l)], [w_out[:n_att], w_out[n_att:]]
        else:
            q, kd, vd, *odd_state = _odd_in(x, mod_l, g[2:3], _odd_weights(odd_w_in[i]), cos_o, sin_o,
                                            prev=odd_state)
            dup = lambda a: jnp.concatenate([a, a], axis=-1).astype(BF16)
            o_c = _gqa_ctx(q, kd, vd, swa_sink[i])
            o_l = _gqa_lat(q, kd, vd, dup(cache_swa_k[:, i]), dup(cache_swa_v[:, i]), swa_sink[i])
            acts, act_w = [(o_c, o_l)], [odd_w_out[i].astype(BF16)]
        last = l == DEPTH - 1
        xs, w16 = _ffn((x,), mod_l, g[3:6], w16, 1, acts=acts, act_w=act_w, dual_out=last,
                       next_w=None if last else (w32, l + 1, 0))

    return (xs[0].reshape(BATCH, SEQ, D_MODEL), xs[1].reshape(DEC_BATCH, DEC_SEQ, D_MODEL),
            even_state[0], even_state[1],
            mlstm_state[0].reshape(BATCH, N_EVEN, 2, M_HEADS, M_DK, M_DV),
            mlstm_state[1].reshape(BATCH, N_EVEN, 2, M_HEADS, M_DK),
            mlstm_state[2][..., 0].reshape(BATCH, N_EVEN, 2, M_HEADS),
            odd_state[0], odd_state[1])
```
